```python
import jax, jax.numpy as jnp
from jax import lax
import numpy as np

D_MODEL = 1024
BATCH = 2
SEQ = 8192
DEPTH = 2

GRID_W = 64
CTX_LEN = 256
EPS = 1e-6

ATT_HEADS = 8
ATT_KV_HEADS = 2
ATT_GROUP = ATT_HEADS // ATT_KV_HEADS
HEAD_DIM = 64
ATT_WIDTH = ATT_HEADS * HEAD_DIM
KV_WIDTH = ATT_KV_HEADS * HEAD_DIM
Q_BLOCK = 128
ROPE_THETA = 10000.0

LRU_WIDTH = 256
LRU_BLOCKS = 4
LRU_BLOCK_DIM = LRU_WIDTH // LRU_BLOCKS
LRU_CONV = 4
LRU_C = 8.0

DN_HEADS = 4
DN_HEAD_DIM = 64
DN_WIDTH = DN_HEADS * DN_HEAD_DIM
DN_CONV = 4
DN_CHUNK = 64

N_DIR = 2
MIX_WIDTH = ATT_WIDTH + LRU_WIDTH + DN_WIDTH
FFN_HIDDEN = ((8 * D_MODEL + 3 * 256 - 1) // (3 * 256)) * 256

SPLIT_SIZES = (ATT_WIDTH, KV_WIDTH, KV_WIDTH, LRU_WIDTH, LRU_WIDTH, 3 * DN_WIDTH, DN_WIDTH, N_DIR * DN_HEADS, N_DIR * DN_HEADS)
SPLIT_AT = tuple(int(v) for v in np.cumsum(SPLIT_SIZES)[:-1])
IN_COLS = sum(SPLIT_SIZES)

kernel_name = 'hybrid_parallel_groups_flow_block'


def rmsnorm(x, gain):
    xf = x.astype(jnp.float32)
    y = xf * lax.rsqrt(jnp.mean(xf * xf, axis=-1, keepdims=True) + EPS)
    return (y * gain.astype(jnp.float32)).astype(x.dtype)


def l2norm(x):
    xf = x.astype(jnp.float32)
    return xf * lax.rsqrt(jnp.sum(xf * xf, axis=-1, keepdims=True) + EPS)


def axial_rope(rows):
    t = jnp.arange(rows * GRID_W)
    row = (t // GRID_W).astype(jnp.float32)
    col = (t % GRID_W).astype(jnp.float32)
    n_freq = HEAD_DIM // 4
    inv_freq = ROPE_THETA ** (-jnp.arange(n_freq, dtype=jnp.float32) / n_freq)
    ang = jnp.concatenate([row[:, None] * inv_freq, col[:, None] * inv_freq], axis=-1)
    return jnp.cos(ang), jnp.sin(ang)


def apply_rope(x, cos, sin):
    xf = x.astype(jnp.float32)
    x1, x2 = jnp.split(xf, 2, axis=-1)
    cs = cos[None, :, None, :]
    sn = sin[None, :, None, :]
    return jnp.concatenate([x1 * cs - x2 * sn, x2 * cs + x1 * sn], axis=-1).astype(x.dtype)


def dwconv_centred(x, w):
    k = w.shape[0]
    left = (k - 1) // 2
    return lax.conv_general_dilated(x, w[:, None, :].astype(x.dtype), (1,), [(left, k - 1 - left)],
                                    dimension_numbers=('NWC', 'WIO', 'NWC'), feature_group_count=x.shape[-1])


def flip_dir(t, d):
    return t[:, ::-1] if d == 1 else t


def attend(q, k, v):
    s = jnp.einsum('bqhgd,bkhd->bhgqk', q, k, preferred_element_type=jnp.float32) * HEAD_DIM ** -0.5
    p = jax.nn.softmax(s, axis=-1).astype(v.dtype)
    return jnp.einsum('bhgqk,bkhd->bqhgd', p, v)


def latent_attention(q, k_all, v_all):
    b, n = q.shape[0], q.shape[1]
    nb = n // Q_BLOCK
    qb = q.reshape(b, nb, Q_BLOCK, ATT_KV_HEADS, ATT_GROUP, HEAD_DIM).swapaxes(0, 1)
    o = lax.map(lambda blk: attend(blk, k_all, v_all), qb)
    return o.swapaxes(0, 1).reshape(b, n, ATT_WIDTH)


def linear_scan(a, u, h0):
    def combine(e, l):
        return e[0] * l[0], l[0] * e[1] + l[1]
    a_cum, h = lax.associative_scan(combine, (a, u), axis=1)
    return h + a_cum * h0[:, None, :]


def rglru(x, w_r, b_r, w_i, b_i, lam, h0):
    bsz, n, _ = x.shape
    xb = x.reshape(bsz, n, LRU_BLOCKS, LRU_BLOCK_DIM)
    r = jax.nn.sigmoid(jnp.einsum('bnkd,kde->bnke', xb, w_r).reshape(bsz, n, LRU_WIDTH) + b_r)
    i = jax.nn.sigmoid(jnp.einsum('bnkd,kde->bnke', xb, w_i).reshape(bsz, n, LRU_WIDTH) + b_i)
    log_a = -LRU_C * r * jax.nn.softplus(-lam)
    a = jnp.exp(log_a)
    mult = jnp.sqrt(-jnp.expm1(2.0 * log_a))
    h = linear_scan(a, mult * (i * x), h0)
    return h, h[:, -1]


def rglru_bidirectional(x_ctx, x_lat, w_r, b_r, w_i, b_i, lam):
    zeros = jnp.zeros((x_ctx.shape[0], LRU_WIDTH), jnp.float32)
    y_ctx, y_lat = [], []
    for d in range(N_DIR):
        h_c, s_c = rglru(flip_dir(x_ctx, d), w_r[d], b_r[d], w_i[d], b_i[d], lam[d], zeros)
        h_l, _ = rglru(flip_dir(x_lat, d), w_r[d], b_r[d], w_i[d], b_i[d], lam[d], s_c)
        y_ctx.append(flip_dir(h_c, d))
        y_lat.append(flip_dir(h_l, d))
    return y_ctx[0] + y_ctx[1], y_lat[0] + y_lat[1]


def gated_delta_chunked(q, k, v, g, beta, s0):
    bsz, n, h, dk = q.shape
    dv = v.shape[-1]
    nc = n // DN_CHUNK

    def chunks(t):
        t = t.reshape((bsz, nc, DN_CHUNK, h) + t.shape[3:])
        return jnp.moveaxis(t, 3, 1)

    q = chunks(q * dk ** -0.5)
    k = chunks(k)
    v = chunks(v)
    beta = chunks(beta)
    gc = jnp.cumsum(chunks(g), axis=-1)
    incl = jnp.tril(jnp.ones((DN_CHUNK, DN_CHUNK), dtype=bool))
    strict = jnp.tril(jnp.ones((DN_CHUNK, DN_CHUNK), dtype=bool), k=-1)
    decay = jnp.exp(jnp.where(incl, gc[..., :, None] - gc[..., None, :], -jnp.inf))
    kb = k * beta[..., None]
    lower = jnp.where(strict, jnp.einsum('bhncd,bhnkd->bhnck', kb, k) * decay, 0.0)
    eye = jnp.eye(DN_CHUNK, dtype=jnp.float32)
    rhs = jnp.concatenate([v * beta[..., None], kb * jnp.exp(gc)[..., None]], axis=-1)
    sol = lax.linalg.triangular_solve(lower + eye, rhs, left_side=True, lower=True)
    u, w = sol[..., :dv], sol[..., dv:]
    attn = jnp.where(incl, jnp.einsum('bhncd,bhnkd->bhnck', q, k) * decay, 0.0)
    q_dec = q * jnp.exp(gc)[..., None]
    k_tail = k * jnp.exp(gc[..., -1:] - gc)[..., None]
    g_tail = jnp.exp(gc[..., -1])

    def step(s, inp):
        u_c, w_c, qd_c, a_c, kt_c, gt_c = inp
        v_new = u_c - jnp.einsum('bhcd,bhde->bhce', w_c, s)
        o = jnp.einsum('bhcd,bhde->bhce', qd_c, s) + jnp.einsum('bhck,bhke->bhce', a_c, v_new)
        s = s * gt_c[..., None, None] + jnp.einsum('bhcd,bhce->bhde', kt_c, v_new)
        return s, o

    xs = tuple(jnp.moveaxis(t, 2, 0) for t in (u, w, q_dec, attn, k_tail, g_tail))
    s_fin, o = lax.scan(step, s0, xs)
    o = jnp.moveaxis(jnp.moveaxis(o, 0, 2), 1, 3).reshape(bsz, n, h, dv)
    return o, s_fin


def deltanet_bidirectional(qc, kc, vc, ac, bc, ql, kl, vl, al, bl, a_log, dt_bias):
    s0 = jnp.zeros((qc.shape[0], DN_HEADS, DN_HEAD_DIM, DN_HEAD_DIM), jnp.float32)
    o_ctx, o_lat = [], []
    for d in range(N_DIR):
        a_d = jnp.exp(a_log[d])
        g_c = -a_d * jax.nn.softplus(ac[:, :, d] + dt_bias[d])
        g_l = -a_d * jax.nn.softplus(al[:, :, d] + dt_bias[d])
        beta_c = jax.nn.sigmoid(bc[:, :, d])
        beta_l = jax.nn.sigmoid(bl[:, :, d])
        oc, s_c = gated_delta_chunked(*(flip_dir(t, d) for t in (qc, kc, vc, g_c, beta_c)), s0)
        ol, _ = gated_delta_chunked(*(flip_dir(t, d) for t in (ql, kl, vl, g_l, beta_l)), s_c)
        o_ctx.append(flip_dir(oc, d))
        o_lat.append(flip_dir(ol, d))
    return o_ctx[0] + o_ctx[1], o_lat[0] + o_lat[1]


def dn_inputs(dqkv, da, db, dn_conv_w):
    bsz, n, _ = dqkv.shape
    qkv = jax.nn.silu(dwconv_centred(dqkv, dn_conv_w).astype(jnp.float32))
    q, k, v = jnp.split(qkv.reshape(bsz, n, 3, DN_HEADS, DN_HEAD_DIM), 3, axis=2)
    q = l2norm(q[:, :, 0])
    k = l2norm(k[:, :, 0])
    a = da.astype(jnp.float32).reshape(bsz, n, N_DIR, DN_HEADS)
    b = db.astype(jnp.float32).reshape(bsz, n, N_DIR, DN_HEADS)
    return q, k, v[:, :, 0], a, b


def dn_output(o, z, g_dn_out):
    bsz, n = o.shape[0], o.shape[1]
    zf = z.astype(jnp.float32).reshape(bsz, n, DN_HEADS, DN_HEAD_DIM)
    return (rmsnorm(o, g_dn_out) * jax.nn.silu(zf)).reshape(bsz, n, DN_WIDTH).astype(z.dtype)


def merge_groups(att, lru, dn, g_group, w_out):
    y = jnp.concatenate([rmsnorm(att, g_group[:ATT_WIDTH]), rmsnorm(lru, g_group[ATT_WIDTH:]), dn], axis=-1)
    return y @ w_out


def mixer(h_lat, h_ctx, cos, sin, w_in, g_qk, lru_conv_w, lru_conv_b, lru_w_r, lru_b_r, lru_w_i, lru_b_i,
          lru_lambda, dn_conv_w, dn_a_log, dn_dt_bias, g_dn_out, g_group, w_out, need_ctx):
    bsz, n, _ = h_lat.shape
    m = h_ctx.shape[1]
    aq_l, ak_l, av_l, lg_l, lx_l, dqkv_l, dz_l, da_l, db_l = jnp.split(h_lat @ w_in, SPLIT_AT, axis=-1)
    aq_c, ak_c, av_c, lg_c, lx_c, dqkv_c, dz_c, da_c, db_c = jnp.split(h_ctx @ w_in, SPLIT_AT, axis=-1)

    q_lat = apply_rope(rmsnorm(aq_l.reshape(bsz, n, ATT_HEADS, HEAD_DIM), g_qk[0]), cos, sin)
    k_lat = apply_rope(rmsnorm(ak_l.reshape(bsz, n, ATT_KV_HEADS, HEAD_DIM), g_qk[1]), cos, sin)
    k_ctx = rmsnorm(ak_c.reshape(bsz, m, ATT_KV_HEADS, HEAD_DIM), g_qk[1])
    v_ctx = av_c.reshape(bsz, m, ATT_KV_HEADS, HEAD_DIM)
    k_all = jnp.concatenate([k_ctx, k_lat], axis=1)
    v_all = jnp.concatenate([v_ctx, av_l.reshape(bsz, n, ATT_KV_HEADS, HEAD_DIM)], axis=1)
    att_lat = latent_attention(q_lat, k_all, v_all)

    xr_lat = (dwconv_centred(lx_l, lru_conv_w) + lru_conv_b).astype(jnp.float32)
    xr_ctx = (dwconv_centred(lx_c, lru_conv_w) + lru_conv_b).astype(jnp.float32)
    y_ctx, y_lat = rglru_bidirectional(xr_ctx, xr_lat, lru_w_r, lru_b_r, lru_w_i, lru_b_i, lru_lambda)
    lru_lat = (jax.nn.gelu(lg_l.astype(jnp.float32)) * y_lat).astype(h_lat.dtype)

    ql, kl, vl, al, bl = dn_inputs(dqkv_l, da_l, db_l, dn_conv_w)
    qc, kc, vc, ac, bc = dn_inputs(dqkv_c, da_c, db_c, dn_conv_w)
    o_ctx, o_lat = deltanet_bidirectional(qc, kc, vc, ac, bc, ql, kl, vl, al, bl, dn_a_log, dn_dt_bias)
    dn_lat = dn_output(o_lat, dz_l, g_dn_out)

    out_lat = merge_groups(att_lat, lru_lat, dn_lat, g_group, w_out)
    if not need_ctx:
        return out_lat, None
    q_ctx = rmsnorm(aq_c.reshape(bsz, m, ATT_HEADS, HEAD_DIM), g_qk[0])
    att_ctx = attend(q_ctx.reshape(bsz, m, ATT_KV_HEADS, ATT_GROUP, HEAD_DIM), k_ctx, v_ctx).reshape(bsz, m, ATT_WIDTH)
    lru_ctx = (jax.nn.gelu(lg_c.astype(jnp.float32)) * y_ctx).astype(h_ctx.dtype)
    dn_ctx = dn_output(o_ctx, dz_c, g_dn_out)
    out_ctx = merge_groups(att_ctx, lru_ctx, dn_ctx, g_group, w_out)
    return out_lat, out_ctx


def swiglu(h, w_ffn_in, w_ffn_out):
    gate, up = jnp.split(h @ w_ffn_in, 2, axis=-1)
    return (jax.nn.silu(gate) * up) @ w_ffn_out


def setup_inputs(seed: int = 0) -> dict:
    key = jax.random.key(seed)
    ks = iter(jax.random.split(key, 40))
    f32 = jnp.float32

    def nrm(shape, scale):
        return jax.random.normal(next(ks), shape, f32) * scale

    def gain(shape):
        return 1.0 + 0.02 * jax.random.normal(next(ks), shape, f32)

    a0 = jax.random.uniform(next(ks), (DEPTH, N_DIR, LRU_WIDTH), f32, 0.9, 0.999)
    p = a0 ** (1.0 / LRU_C)
    lru_lambda = jnp.log(p) - jnp.log1p(-p)
    dn_a_log = jnp.log(jax.random.uniform(next(ks), (DEPTH, N_DIR, DN_HEADS), f32, 1.0, 16.0))
    dt = jnp.exp(jax.random.uniform(next(ks), (DEPTH, N_DIR, DN_HEADS), f32, np.log(1e-3), np.log(1e-1)))
    dn_dt_bias = dt + jnp.log(-jnp.expm1(-dt))
    return {
        'x': nrm((BATCH, SEQ, D_MODEL), 1.0),
        'c': nrm((BATCH, D_MODEL), 1.0),
        'ctx': nrm((BATCH, CTX_LEN, D_MODEL), 1.0),
        'c_ctx': nrm((D_MODEL,), 1.0),
        'w_ada': nrm((DEPTH, D_MODEL, 6 * D_MODEL), 0.5 * D_MODEL ** -0.5),
        'b_ada': nrm((DEPTH, 6 * D_MODEL), 0.02),
        'g_norm': gain((DEPTH, 4, D_MODEL)),
        'w_in': nrm((DEPTH, D_MODEL, IN_COLS), D_MODEL ** -0.5),
        'g_qk': gain((DEPTH, 2, HEAD_DIM)),
        'lru_conv_w': nrm((DEPTH, LRU_CONV, LRU_WIDTH), LRU_CONV ** -0.5),
        'lru_conv_b': nrm((DEPTH, LRU_WIDTH), 0.02),
        'lru_w_r': nrm((DEPTH, N_DIR, LRU_BLOCKS, LRU_BLOCK_DIM, LRU_BLOCK_DIM), LRU_BLOCK_DIM ** -0.5),
        'lru_b_r': nrm((DEPTH, N_DIR, LRU_WIDTH), 0.02),
        'lru_w_i': nrm((DEPTH, N_DIR, LRU_BLOCKS, LRU_BLOCK_DIM, LRU_BLOCK_DIM), LRU_BLOCK_DIM ** -0.5),
        'lru_b_i': nrm((DEPTH, N_DIR, LRU_WIDTH), 0.02),
        'lru_lambda': lru_lambda,
        'dn_conv_w': nrm((DEPTH, DN_CONV, 3 * DN_WIDTH), DN_CONV ** -0.5),
        'dn_a_log': dn_a_log,
        'dn_dt_bias': dn_dt_bias,
        'g_dn_out': gain((DEPTH, DN_HEAD_DIM)),
        'g_group': gain((DEPTH, ATT_WIDTH + LRU_WIDTH)),
        'w_out': nrm((DEPTH, MIX_WIDTH, D_MODEL), MIX_WIDTH ** -0.5),
        'w_ffn_in': nrm((DEPTH, D_MODEL, 2 * FFN_HIDDEN), D_MODEL ** -0.5),
        'w_ffn_out': nrm((DEPTH, FFN_HIDDEN, D_MODEL), FFN_HIDDEN ** -0.5),
    }


def reference(x, c, ctx, c_ctx, w_ada, b_ada, g_norm, w_in, g_qk, lru_conv_w, lru_conv_b, lru_w_r, lru_b_r,
              lru_w_i, lru_b_i, lru_lambda, dn_conv_w, dn_a_log, dn_dt_bias, g_dn_out, g_group, w_out,
              w_ffn_in, w_ffn_out):
    rows = x.shape[1] // GRID_W
    cos, sin = axial_rope(rows)
    for l in range(DEPTH):
        need_ctx = l < DEPTH - 1
        mod_lat = (jax.nn.silu(c) @ w_ada[l] + b_ada[l])[:, None, :]
        mod_ctx = (jax.nn.silu(c_ctx) @ w_ada[l] + b_ada[l])[None, None, :]
        sh_m, sc_m, gt_m, sh_f, sc_f, gt_f = jnp.split(mod_lat, 6, axis=-1)
        csh_m, csc_m, cgt_m, csh_f, csc_f, cgt_f = jnp.split(mod_ctx, 6, axis=-1)

        h_lat = rmsnorm(x, g_norm[l, 0]) * (1 + sc_m) + sh_m
        h_ctx = rmsnorm(ctx, g_norm[l, 0]) * (1 + csc_m) + csh_m
        mix_lat, mix_ctx = mixer(h_lat, h_ctx, cos, sin, w_in[l], g_qk[l], lru_conv_w[l], lru_conv_b[l],
                                 lru_w_r[l], lru_b_r[l], lru_w_i[l], lru_b_i[l], lru_lambda[l], dn_conv_w[l],
                                 dn_a_log[l], dn_dt_bias[l], g_dn_out[l], g_group[l], w_out[l], need_ctx)
        x = x + gt_m * rmsnorm(mix_lat, g_norm[l, 1])
        h = rmsnorm(x, g_norm[l, 2]) * (1 + sc_f) + sh_f
        x = x + gt_f * rmsnorm(swiglu(h, w_ffn_in[l], w_ffn_out[l]), g_norm[l, 3])

        if need_ctx:
            ctx = ctx + cgt_m * rmsnorm(mix_ctx, g_norm[l, 1])
            hc = rmsnorm(ctx, g_norm[l, 2]) * (1 + csc_f) + csh_f
            ctx = ctx + cgt_f * rmsnorm(swiglu(hc, w_ffn_in[l], w_ffn_out[l]), g_norm[l, 3])
    return x
```

```python
import functools

import jax
import jax.numpy as jnp
from jax import lax
from jax.experimental import pallas as pl
from jax.experimental.pallas import tpu as pltpu

F32 = jnp.float32
BF16 = jnp.bfloat16

D_MODEL = 1024
GRID_W = 64
EPS = 1e-6
ATT_GROUP = 4
ATT_KV_HEADS = 2
HEAD_DIM = 64
ATT_WIDTH = 512
ROPE_THETA = 10000.0
LRU_WIDTH = 256
LRU_BLOCK_DIM = 64
LRU_C = 8.0
DN_HEADS = 4
DN_WIDTH = 256
DN_CHUNK = 64
DN_BASE = 8
N_DIR = 2
FFN_HIDDEN = 2816
FFN_CHUNK = 256

LANES = 128
SUBLANES = 8
TM = 256
ATT_TK = 1024
QKV_COLS = 768
REST_COLS = 1664
IN_COLS_PAD = QKV_COLS + REST_COLS
VMEM_LIMIT = 56 * 1024 * 1024


def _cparams(*sem):
    return pltpu.CompilerParams(dimension_semantics=sem, vmem_limit_bytes=VMEM_LIMIT)


def _rms(x, gain):
    ms = jnp.mean(x * x, axis=-1, keepdims=True)
    return x * lax.rsqrt(ms + EPS) * gain


def _seg_sumsq(x, ones_bd):
    sq = x * x
    hi = sq.astype(BF16)
    lo = (sq - hi.astype(F32)).astype(BF16)
    return (jnp.dot(hi, ones_bd, preferred_element_type=F32)
            + jnp.dot(lo, ones_bd, preferred_element_type=F32))


def _silu(x):
    return x * jax.nn.sigmoid(x)


def _softplus(x):
    return jnp.maximum(x, 0.0) + jnp.log1p(jnp.exp(-jnp.abs(x)))


def _mod_row(b, i):
    return jnp.where(i == 0, 2, b)


def _mod_kernel(c_ref, w_ref, b_ref, o_ref):
    s = _silu(c_ref[...])
    o_ref[0] = jnp.dot(s.astype(BF16), w_ref[0].astype(BF16), preferred_element_type=F32) + b_ref[0]


def _modulation(cvec, w_ada, b_ada):
    depth = w_ada.shape[0]
    d = D_MODEL
    return pl.pallas_call(
        _mod_kernel,
        grid=(depth, 6),
        in_specs=[pl.BlockSpec((SUBLANES, d), lambda l, j: (0, 0)),
                  pl.BlockSpec((1, d, d), lambda l, j: (l, 0, j)),
                  pl.BlockSpec((1, 1, d), lambda l, j: (l, 0, j))],
        out_specs=pl.BlockSpec((1, SUBLANES, d), lambda l, j: (l, 0, j)),
        out_shape=jax.ShapeDtypeStruct((depth, SUBLANES, 6 * d), F32),
        compiler_params=_cparams("arbitrary", "arbitrary"),
        name="modulation",
    )(cvec, w_ada, b_ada.reshape(depth, 1, 6 * d))


def _in_kernel(x_ref, sh_ref, sc_ref, gn_ref, w_ref, gqk_ref, cos_ref, sin_ref, ones_ref,
               q_ref, k_ref, v_ref, rest_ref):
    x = x_ref[0]
    h = _rms(x, gn_ref[...]) * (1.0 + sc_ref[0, 0]) + sh_ref[0, 0]
    hb = h.astype(BF16)
    qkv = jnp.dot(hb, w_ref[:, 0:QKV_COLS], preferred_element_type=F32)
    cosf = cos_ref[...]
    sinf = sin_ref[...]
    lane = lax.broadcasted_iota(jnp.int32, (TM, LANES), 1)
    first_half = (lane & 32) == 0
    ones_bd = ones_ref[...]

    def norm_rope(slab, gain):
        ss = _seg_sumsq(slab, ones_bd)
        y = slab * lax.rsqrt(ss * (1.0 / HEAD_DIM) + EPS) * gain
        partner = jnp.where(first_half, pltpu.roll(y, 96, axis=1), pltpu.roll(y, 32, axis=1))
        return y * cosf + partner * sinf

    for s in range(ATT_WIDTH // LANES):
        qs = norm_rope(qkv[:, LANES * s:LANES * (s + 1)], gqk_ref[0:1, :]) * (HEAD_DIM ** -0.5)
        q_ref[0, :, LANES * s:LANES * (s + 1)] = qs.astype(BF16)
    kk = norm_rope(qkv[:, 512:640], gqk_ref[1:2, :]).astype(BF16)
    vv = qkv[:, 640:768].astype(BF16)
    for g in range(ATT_KV_HEADS):
        k_ref[0, g] = kk[:, HEAD_DIM * g:HEAD_DIM * (g + 1)]
        v_ref[0, g] = vv[:, HEAD_DIM * g:HEAD_DIM * (g + 1)]
    rest_ref[0] = jnp.dot(hb, w_ref[:, QKV_COLS:], preferred_element_type=F32)


def _in_proj(xs, mod4, gn0, w_in_p, gqk2, cosf, sinf, ones128):
    b, t, d = xs.shape
    nb = t // TM
    return pl.pallas_call(
        _in_kernel,
        grid=(b, nb),
        in_specs=[pl.BlockSpec((1, TM, d), lambda bi, i: (bi, i, 0)),
                  pl.BlockSpec((1, 1, 1, d), lambda bi, i: (_mod_row(bi, i), 0, 0, 0)),
                  pl.BlockSpec((1, 1, 1, d), lambda bi, i: (_mod_row(bi, i), 1, 0, 0)),
                  pl.BlockSpec((1, d), lambda bi, i: (0, 0)),
                  pl.BlockSpec((d, IN_COLS_PAD), lambda bi, i: (0, 0)),
                  pl.BlockSpec((2, LANES), lambda bi, i: (0, 0)),
                  pl.BlockSpec((TM, LANES), lambda bi, i: (i, 0)),
                  pl.BlockSpec((TM, LANES), lambda bi, i: (i, 0)),
                  pl.BlockSpec((LANES, LANES), lambda bi, i: (0, 0))],
        out_specs=[pl.BlockSpec((1, TM, ATT_WIDTH), lambda bi, i: (bi, i, 0)),
                   pl.BlockSpec((1, ATT_KV_HEADS, TM, HEAD_DIM), lambda bi, i: (bi, 0, i, 0)),
                   pl.BlockSpec((1, ATT_KV_HEADS, TM, HEAD_DIM), lambda bi, i: (bi, 0, i, 0)),
                   pl.BlockSpec((1, TM, REST_COLS), lambda bi, i: (bi, i, 0))],
        out_shape=[jax.ShapeDtypeStruct((b, t, ATT_WIDTH), BF16),
                   jax.ShapeDtypeStruct((b, ATT_KV_HEADS, t, HEAD_DIM), BF16),
                   jax.ShapeDtypeStruct((b, ATT_KV_HEADS, t, HEAD_DIM), BF16),
                   jax.ShapeDtypeStruct((b, t, REST_COLS), F32)],
        compiler_params=_cparams("arbitrary", "arbitrary"),
        name="in_proj",
    )(xs, mod4, mod4, gn0, w_in_p, gqk2, cosf, sinf, ones128)


def _att_kernel(q_ref, k_ref, v_ref, o_ref, qs_ref, m_ref, l_ref, acc_ref, *, ctx, n_lat, q_off):
    i = pl.program_id(2) + q_off
    tq = q_ref.shape[1]
    for h in range(ATT_GROUP):
        qs_ref[h * tq:(h + 1) * tq, :] = q_ref[0, :, HEAD_DIM * h:HEAD_DIM * (h + 1)]

    def scores(start, size):
        k = k_ref[0, 0, pl.ds(start, size), :]
        return lax.dot_general(qs_ref[...], k, (((1,), (1,)), ((), ())), preferred_element_type=F32)

    s = scores(0, ctx)
    m0 = jnp.max(s, axis=-1, keepdims=True)
    p = jnp.exp(s - m0)
    m_ref[...] = m0
    l_ref[...] = jnp.sum(p, axis=-1, keepdims=True)
    acc_ref[...] = jnp.dot(p.astype(BF16), v_ref[0, 0, 0:ctx, :], preferred_element_type=F32)

    def body(j, carry):
        start = pl.multiple_of(ctx + j * ATT_TK, TM)
        s = scores(start, ATT_TK)
        m_prev = m_ref[...]
        m_new = jnp.maximum(m_prev, jnp.max(s, axis=-1, keepdims=True))
        alpha = jnp.exp(m_prev - m_new)
        p = jnp.exp(s - m_new)
        l_ref[...] = alpha * l_ref[...] + jnp.sum(p, axis=-1, keepdims=True)
        v = v_ref[0, 0, pl.ds(start, ATT_TK), :]
        acc_ref[...] = alpha * acc_ref[...] + jnp.dot(p.astype(BF16), v, preferred_element_type=F32)
        m_ref[...] = m_new
        return carry

    lax.fori_loop(0, jnp.where(i == 0, 0, n_lat // ATT_TK), body, 0)
    out = acc_ref[...] / l_ref[...]
    for h in range(ATT_GROUP):
        o_ref[0, :, HEAD_DIM * h:HEAD_DIM * (h + 1)] = out[h * tq:(h + 1) * tq, :]


def _attention(q, k, v, ctx, q_off):
    b, t, _ = q.shape
    nq = t // TM - q_off
    gw = ATT_GROUP * HEAD_DIM
    kern = functools.partial(_att_kernel, ctx=ctx, n_lat=t - ctx, q_off=q_off)
    return pl.pallas_call(
        kern,
        grid=(b, ATT_KV_HEADS, nq),
        in_specs=[pl.BlockSpec((1, TM, gw), lambda bi, g, i: (bi, i + q_off, g)),
                  pl.BlockSpec((1, 1, t, HEAD_DIM), lambda bi, g, i: (bi, g, 0, 0)),
                  pl.BlockSpec((1, 1, t, HEAD_DIM), lambda bi, g, i: (bi, g, 0, 0))],
        out_specs=pl.BlockSpec((1, TM, gw), lambda bi, g, i: (bi, i, g)),
        out_shape=jax.ShapeDtypeStruct((b, nq * TM, ATT_WIDTH), F32),
        scratch_shapes=[pltpu.VMEM((ATT_GROUP * TM, HEAD_DIM), BF16),
                        pltpu.VMEM((ATT_GROUP * TM, 1), F32),
                        pltpu.VMEM((ATT_GROUP * TM, 1), F32),
                        pltpu.VMEM((ATT_GROUP * TM, HEAD_DIM), F32)],
        compiler_params=_cparams("arbitrary", "arbitrary", "arbitrary"),
        name="attention",
    )(q, k, v)


def _prep_kernel(cur_ref, prv_ref, nxt_ref, cw_ref, cb_ref, ab_ref, dnv_ref, ones_ref,
                 xr_ref, q_ref, k_ref, v_ref, gb_ref, ext_ref):
    i = pl.program_id(1)
    nb = pl.num_programs(1)
    has_prev = i >= 2
    has_next = jnp.logical_and(i >= 1, i < nb - 1)
    ext_ref[0:SUBLANES, :] = jnp.where(has_prev, prv_ref[0], 0.0)
    ext_ref[SUBLANES:SUBLANES + TM, :] = cur_ref[0]
    ext_ref[SUBLANES + TM:2 * SUBLANES + TM, :] = jnp.where(has_next, nxt_ref[0], 0.0)
    conv = cb_ref[...]
    for j in range(4):
        conv = conv + ext_ref[SUBLANES - 1 + j:SUBLANES - 1 + j + TM, :] * cw_ref[j:j + 1, :]
    xr_ref[0] = conv[:, 3 * DN_WIDTH:]
    act = _silu(conv[:, 0:3 * DN_WIDTH])
    qq = act[:, 0:DN_WIDTH]
    kk = act[:, DN_WIDTH:2 * DN_WIDTH]
    vv = act[:, 2 * DN_WIDTH:3 * DN_WIDTH]
    ones_bd = ones_ref[...]
    qn = qq * lax.rsqrt(_seg_sumsq(qq, ones_bd) + EPS) * (DN_CHUNK ** -0.5)
    kn = kk * lax.rsqrt(_seg_sumsq(kk, ones_bd) + EPS)
    for h in range(DN_HEADS):
        sl = slice(DN_CHUNK * h, DN_CHUNK * (h + 1))
        q_ref[0, h] = qn[:, sl]
        k_ref[0, h] = kn[:, sl]
        v_ref[0, h] = vv[:, sl]
    ab = ab_ref[0]
    g = -jnp.exp(dnv_ref[0:1, :]) * _softplus(ab + dnv_ref[1:2, :])
    beta = jax.nn.sigmoid(ab)
    lane = lax.broadcasted_iota(jnp.int32, ab.shape, 1)
    gb_ref[0] = jnp.where(lane < N_DIR * DN_HEADS, g, beta)[:, 0:2 * N_DIR * DN_HEADS]


def _prep(rest, cw, cb, dnv, ones256):
    b, t, _ = rest.shape
    nb = t // TM
    cc = 4 * DN_WIDTH
    r8 = TM // SUBLANES
    hm = jax.ShapeDtypeStruct((b, DN_HEADS, t, DN_CHUNK), F32)
    hm_spec = pl.BlockSpec((1, DN_HEADS, TM, DN_CHUNK), lambda bi, i: (bi, 0, i, 0))
    return pl.pallas_call(
        _prep_kernel,
        grid=(b, nb),
        in_specs=[pl.BlockSpec((1, TM, cc), lambda bi, i: (bi, i, 0)),
                  pl.BlockSpec((1, SUBLANES, cc), lambda bi, i: (bi, jnp.maximum(i * r8 - 1, 0), 0)),
                  pl.BlockSpec((1, SUBLANES, cc), lambda bi, i: (bi, jnp.minimum((i + 1) * r8, nb * r8 - 1), 0)),
                  pl.BlockSpec((4, cc), lambda bi, i: (0, 0)),
                  pl.BlockSpec((1, cc), lambda bi, i: (0, 0)),
                  pl.BlockSpec((1, TM, LANES), lambda bi, i: (bi, i, 12)),
                  pl.BlockSpec((2, LANES), lambda bi, i: (0, 0)),
                  pl.BlockSpec((DN_WIDTH, DN_WIDTH), lambda bi, i: (0, 0))],
        out_specs=[pl.BlockSpec((1, TM, LRU_WIDTH), lambda bi, i: (bi, i, 0)),
                   hm_spec, hm_spec, hm_spec,
                   pl.BlockSpec((1, TM, 2 * N_DIR * DN_HEADS), lambda bi, i: (bi, i, 0))],
        out_shape=[jax.ShapeDtypeStruct((b, t, LRU_WIDTH), F32), hm, hm, hm,
                   jax.ShapeDtypeStruct((b, t, 2 * N_DIR * DN_HEADS), F32)],
        scratch_shapes=[pltpu.VMEM((TM + 2 * SUBLANES, cc), F32)],
        compiler_params=_cparams("arbitrary", "arbitrary"),
        name="conv_prep",
    )(rest, rest, rest, cw, cb, rest, dnv, ones256)


def _lru_kernel(xr_ref, wg_ref, vec_ref, y_ref, a0, u0, a1, u1, *, ctx, n):
    nseg = SUBLANES
    vec = vec_ref[0]
    a_refs = (a0, a1)
    u_refs = (u0, u1)

    def gates(x, d):
        g = jnp.dot(x.astype(BF16), wg_ref[0, :, 2 * LANES * d:2 * LANES * (d + 1)],
                    preferred_element_type=F32)
        r = jax.nn.sigmoid(g[:, :LANES] + vec[2 + 2 * d:3 + 2 * d])
        ig = jax.nn.sigmoid(g[:, LANES:] + vec[3 + 2 * d:4 + 2 * d])
        log_a = -LRU_C * r * _softplus(-vec[d:d + 1])
        a = jnp.exp(log_a)
        mult = jnp.sqrt(-jnp.tanh(log_a) * (a * a + 1.0))
        return a, mult * (ig * x)

    def run(row0, seg, base, init):
        pitch = seg + SUBLANES
        for j in range(nseg):
            x = xr_ref[0, row0 + j * seg:row0 + (j + 1) * seg, :]
            for d in range(N_DIR):
                a, u = gates(x, d)
                a_refs[d][base + j * pitch:base + j * pitch + seg, :] = a
                u_refs[d][base + j * pitch:base + j * pitch + seg, :] = u

        def step(t, carry):
            out = []
            for d in range(N_DIR):
                h, p = carry[2 * d], carry[2 * d + 1]
                r = base + (t if d == 0 else seg - 1 - t)
                idx = pl.ds(r, nseg, stride=pitch)
                a = a_refs[d][idx, :]
                h = a * h + u_refs[d][idx, :]
                p = a * p
                u_refs[d][idx, :] = h
                a_refs[d][idx, :] = p
                out += [h, p]
            return tuple(out)

        z = jnp.zeros((nseg, LANES), F32)
        o = jnp.ones((nseg, LANES), F32)
        fin = lax.fori_loop(0, seg, step, (z, o, z, o), unroll=8)
        carries, finals = [], []
        for d in range(N_DIR):
            hl, pl_ = fin[2 * d], fin[2 * d + 1]
            order = list(range(nseg)) if d == 0 else list(range(nseg - 1, -1, -1))
            c = init[d]
            cs = {}
            for j in order:
                cs[j] = c
                c = hl[j:j + 1] + pl_[j:j + 1] * c
            carries.append(cs)
            finals.append(c)
        for j in range(nseg):
            sl = slice(base + j * pitch, base + j * pitch + seg)
            y = (u0[sl, :] + a0[sl, :] * carries[0][j]) + (u1[sl, :] + a1[sl, :] * carries[1][j])
            y_ref[0, row0 + j * seg:row0 + (j + 1) * seg, :] = y
        return finals

    zero = jnp.zeros((1, LANES), F32)
    seg_c = ctx // nseg
    fin_c = run(0, seg_c, 0, [zero, zero])
    run(ctx, n // nseg, nseg * (seg_c + SUBLANES), fin_c)


def _lru(xr, wg, vec, ctx):
    b, t, _ = xr.shape
    n = t - ctx
    rows = SUBLANES * (ctx // SUBLANES + SUBLANES) + SUBLANES * (n // SUBLANES + SUBLANES)
    kern = functools.partial(_lru_kernel, ctx=ctx, n=n)
    return pl.pallas_call(
        kern,
        grid=(b, LRU_WIDTH // LANES),
        in_specs=[pl.BlockSpec((1, t, LANES), lambda bi, hf: (bi, 0, hf)),
                  pl.BlockSpec((1, LANES, 4 * LANES), lambda bi, hf: (hf, 0, 0)),
                  pl.BlockSpec((1, SUBLANES, LANES), lambda bi, hf: (hf, 0, 0))],
        out_specs=pl.BlockSpec((1, t, LANES), lambda bi, hf: (bi, 0, hf)),
        out_shape=jax.ShapeDtypeStruct((b, t, LRU_WIDTH), F32),
        scratch_shapes=[pltpu.VMEM((rows, LANES), F32)] * 4,
        compiler_params=_cparams("arbitrary", "arbitrary"),
        name="rglru",
    )(xr, wg, vec)


def _dn_kernel(qf_ref, kf_ref, vf_ref, gf_ref, qb_ref, kb_ref, vb_ref, gbk_ref, o0_ref, o1_ref, s_ref):
    c_ = DN_CHUNK

    @pl.when(pl.program_id(1) == 0)
    def _():
        s_ref[...] = jnp.zeros_like(s_ref)

    ii = lax.broadcasted_iota(jnp.int32, (c_, c_), 0)
    jj = lax.broadcasted_iota(jnp.int32, (c_, c_), 1)
    eye = ii == jj
    eye_f = jnp.where(eye, 1.0, 0.0)
    shift = DN_BASE.bit_length() - 1
    same_blk = {DN_BASE: (ii >> shift) == (jj >> shift)}
    off_blk = ({}, {})
    size = DN_BASE
    while size < c_:
        shift = size.bit_length() - 1
        bi, bj = ii >> shift, jj >> shift
        off_blk[0][size] = jnp.logical_and(bi == bj + 1, (bi & 1) == 1)
        off_blk[1][size] = jnp.logical_and(bj == bi + 1, (bj & 1) == 1)
        size *= 2
    nt = (((1,), (1,)), ((), ()))
    tn = (((0,), (0,)), ((), ()))
    per_block = qf_ref.shape[2] // c_
    for d in range(N_DIR):
        q_ref, k_ref, v_ref, g_ref, o_ref = ((qf_ref, kf_ref, vf_ref, gf_ref, o0_ref) if d == 0 else
                                             (qb_ref, kb_ref, vb_ref, gbk_ref, o1_ref))
        incl = (jj <= ii) if d == 0 else (jj >= ii)
        strict = (jj < ii) if d == 0 else (jj > ii)
        incl_t = (ii <= jj) if d == 0 else (ii >= jj)
        for cc in range(per_block):
            c = cc if d == 0 else per_block - 1 - cc
            rows = slice(c_ * c, c_ * (c + 1))
            gbv = g_ref[0, rows, :]
            for h in range(DN_HEADS):
                q = q_ref[0, h, rows, :]
                k = k_ref[0, h, rows, :]
                v = v_ref[0, h, rows, :]
                col = DN_HEADS * d + h
                g_col = gbv[:, col:col + 1]
                b_col = gbv[:, N_DIR * DN_HEADS + col:N_DIR * DN_HEADS + col + 1]
                g_full = jnp.broadcast_to(g_col, (c_, c_))
                g_row = jnp.sum(jnp.where(eye, g_full, 0.0), axis=0, keepdims=True)
                gc_col = jnp.sum(jnp.where(incl, jnp.broadcast_to(g_row, (c_, c_)), 0.0), axis=1, keepdims=True)
                gc_row = jnp.sum(jnp.where(incl_t, g_full, 0.0), axis=0, keepdims=True)
                g_tot = jnp.sum(g_row, axis=1, keepdims=True)
                decay = jnp.where(incl, jnp.exp(jnp.where(incl, gc_col - gc_row, 0.0)), 0.0)
                eg = jnp.exp(gc_col)
                kb = k * b_col
                kbf = k.astype(BF16)
                a1 = lax.dot_general(jnp.concatenate([kb, q], axis=0).astype(BF16), kbf, nt,
                                     preferred_element_type=F32)
                lower = jnp.where(strict, a1[:c_] * decay, 0.0)
                attn = jnp.where(incl, a1[c_:] * decay, 0.0)
                pw = jnp.where(same_blk[DN_BASE], lower, 0.0)
                tm = eye_f - pw
                for _ in range(2):
                    pwb = pw.astype(BF16)
                    pw = jnp.dot(pwb, pwb, preferred_element_type=F32)
                    tm = tm + jnp.dot(tm.astype(BF16), pw.astype(BF16), preferred_element_type=F32)
                size = DN_BASE
                while size < c_:
                    tmb = tm.astype(BF16)
                    a_off = jnp.where(off_blk[d][size], lower, 0.0).astype(BF16)
                    tm = tm - jnp.dot(tmb, jnp.dot(a_off, tmb, preferred_element_type=F32).astype(BF16),
                                      preferred_element_type=F32)
                    size *= 2
                rhs = jnp.concatenate([v * b_col, kb * eg], axis=1)
                sol = jnp.dot(tm.astype(BF16), rhs.astype(BF16), preferred_element_type=F32)
                u = sol[:, :c_]
                w = sol[:, c_:]
                qd = q * eg
                kt = k * jnp.exp(g_tot - gc_col)
                s_old = s_ref[col]
                a2 = jnp.dot(jnp.concatenate([w, qd], axis=0).astype(BF16), s_old.astype(BF16),
                             preferred_element_type=F32)
                v_new = u - a2[:c_]
                vnb = v_new.astype(BF16)
                o = a2[c_:] + jnp.dot(attn.astype(BF16), vnb, preferred_element_type=F32)
                s_ref[col] = s_old * jnp.exp(g_tot) + lax.dot_general(kt.astype(BF16), vnb, tn,
                                                                      preferred_element_type=F32)
                o_ref[0, rows, c_ * h:c_ * (h + 1)] = o


def _deltanet(qh, kh, vh, gb, ctx):
    b, _, t, _ = qh.shape
    blk = 2 * DN_CHUNK
    np_ = t // blk
    npc = ctx // blk

    def bwd(p):
        return jnp.where(p < npc, npc - 1 - p, np_ - 1 - (p - npc))

    hm_f = pl.BlockSpec((1, DN_HEADS, blk, DN_CHUNK), lambda bi, p: (bi, 0, p, 0))
    hm_b = pl.BlockSpec((1, DN_HEADS, blk, DN_CHUNK), lambda bi, p: (bi, 0, bwd(p), 0))
    g_f = pl.BlockSpec((1, blk, 2 * N_DIR * DN_HEADS), lambda bi, p: (bi, p, 0))
    g_b = pl.BlockSpec((1, blk, 2 * N_DIR * DN_HEADS), lambda bi, p: (bi, bwd(p), 0))
    o_shape = jax.ShapeDtypeStruct((b, t, DN_WIDTH), F32)
    return pl.pallas_call(
        _dn_kernel,
        grid=(b, np_),
        in_specs=[hm_f, hm_f, hm_f, g_f, hm_b, hm_b, hm_b, g_b],
        out_specs=[pl.BlockSpec((1, blk, DN_WIDTH), lambda bi, p: (bi, p, 0)),
                   pl.BlockSpec((1, blk, DN_WIDTH), lambda bi, p: (bi, bwd(p), 0))],
        out_shape=[o_shape, o_shape],
        scratch_shapes=[pltpu.VMEM((N_DIR * DN_HEADS, DN_CHUNK, DN_CHUNK), F32)],
        compiler_params=_cparams("arbitrary", "arbitrary"),
        name="deltanet",
    )(qh, kh, vh, gb, qh, kh, vh, gb)


def _out_kernel(x_ref, att_ref, y_ref, lg_ref, o0_ref, o1_ref, dz_ref, gg_ref, gdn_ref, ones_ref, w_ref,
                gt_ref, gn_ref, xo_ref):
    gg = gg_ref[...]
    a_n = _rms(att_ref[0], gg[:, 0:ATT_WIDTH]).astype(BF16)
    lg = lg_ref[0]
    gelu = lg * (0.5 * (1.0 + jnp.tanh(0.7978845608028654 * (lg + 0.044715 * (lg * lg * lg)))))
    l_n = _rms(gelu * y_ref[0], gg[:, ATT_WIDTH:]).astype(BF16)
    o = o0_ref[0] + o1_ref[0]
    ss = _seg_sumsq(o, ones_ref[...])
    dn = ((o * lax.rsqrt(ss * (1.0 / DN_CHUNK) + EPS) * gdn_ref[...]) * _silu(dz_ref[0])).astype(BF16)
    mix = (jnp.dot(a_n, w_ref[0:ATT_WIDTH, :], preferred_element_type=F32)
           + jnp.dot(l_n, w_ref[ATT_WIDTH:ATT_WIDTH + LRU_WIDTH, :], preferred_element_type=F32)
           + jnp.dot(dn, w_ref[ATT_WIDTH + LRU_WIDTH:, :], preferred_element_type=F32))
    xo_ref[0] = x_ref[0] + gt_ref[0, 0] * _rms(mix, gn_ref[...])


def _out_proj(xs, att, y, rest, o0, o1, gg, gdn4, ones256, w_out, mod4, gn1, off):
    b, t, d = xs.shape
    nb = t // TM - off
    att_off = off - (t - att.shape[1]) // TM

    def tok(width, col=0):
        return pl.BlockSpec((1, TM, width), lambda bi, i: (bi, i + off, col))

    def const(shape):
        return pl.BlockSpec(shape, lambda bi, i: (0,) * len(shape))

    return pl.pallas_call(
        _out_kernel,
        grid=(b, nb),
        in_specs=[tok(d),
                  pl.BlockSpec((1, TM, ATT_WIDTH), lambda bi, i: (bi, i + att_off, 0)),
                  tok(LRU_WIDTH), tok(LRU_WIDTH, 4), tok(DN_WIDTH), tok(DN_WIDTH), tok(DN_WIDTH, 5),
                  const((1, ATT_WIDTH + LRU_WIDTH)), const((1, DN_WIDTH)), const((DN_WIDTH, DN_WIDTH)),
                  const((d, d)),
                  pl.BlockSpec((1, 1, 1, d), lambda bi, i: (_mod_row(bi, i + off), 2, 0, 0)),
                  const((1, d))],
        out_specs=pl.BlockSpec((1, TM, d), lambda bi, i: (bi, i, 0)),
        out_shape=jax.ShapeDtypeStruct((b, nb * TM, d), F32),
        compiler_params=_cparams("arbitrary", "arbitrary"),
        name="out_proj",
    )(xs, att, y, rest, o0, o1, rest, gg, gdn4, ones256, w_out, mod4, gn1)


def _ffn_kernel(x_ref, sh_ref, sc_ref, gt_ref, gn2_ref, gn3_ref, wi_ref, wo_ref, o_ref):
    x = x_ref[0]
    h = (_rms(x, gn2_ref[...]) * (1.0 + sc_ref[0, 0]) + sh_ref[0, 0]).astype(BF16)
    acc = jnp.zeros(x.shape, F32)
    for j in range(FFN_HIDDEN // FFN_CHUNK):
        lo = j * FFN_CHUNK
        gate = jnp.dot(h, wi_ref[:, lo:lo + FFN_CHUNK], preferred_element_type=F32)
        up = jnp.dot(h, wi_ref[:, FFN_HIDDEN + lo:FFN_HIDDEN + lo + FFN_CHUNK], preferred_element_type=F32)
        act = (_silu(gate) * up).astype(BF16)
        acc = acc + jnp.dot(act, wo_ref[lo:lo + FFN_CHUNK, :], preferred_element_type=F32)
    o_ref[0] = x + gt_ref[0, 0] * _rms(acc, gn3_ref[...])


def _ffn(x1, mod4, gn2, gn3, w_ffn_in, w_ffn_out, off):
    b, rows, d = x1.shape
    nb = rows // TM

    def mod(k):
        return pl.BlockSpec((1, 1, 1, d), lambda bi, i: (_mod_row(bi, i + off), k, 0, 0))

    def const(shape):
        return pl.BlockSpec(shape, lambda bi, i: (0,) * len(shape))

    return pl.pallas_call(
        _ffn_kernel,
        grid=(b, nb),
        in_specs=[pl.BlockSpec((1, TM, d), lambda bi, i: (bi, i, 0)),
                  mod(3), mod(4), mod(5), const((1, d)), const((1, d)),
                  const((d, 2 * FFN_HIDDEN)), const((FFN_HIDDEN, d))],
        out_specs=pl.BlockSpec((1, TM, d), lambda bi, i: (bi, i, 0)),
        out_shape=jax.ShapeDtypeStruct((b, rows, d), F32),
        compiler_params=_cparams("arbitrary", "arbitrary"),
        name="ffn",
    )(x1, mod4, mod4, mod4, gn2, gn3, w_ffn_in, w_ffn_out)


def _block_diag_ones(n):
    i = jnp.arange(n) // HEAD_DIM
    return (i[:, None] == i[None, :]).astype(BF16)


def _rope_tables(ctx, n):
    t = jnp.arange(n)
    row = (t // GRID_W).astype(F32)
    col = (t % GRID_W).astype(F32)
    n_freq = HEAD_DIM // 4
    inv_freq = ROPE_THETA ** (-jnp.arange(n_freq, dtype=F32) / n_freq)
    ang = jnp.concatenate([row[:, None] * inv_freq, col[:, None] * inv_freq], axis=-1)
    cos = jnp.concatenate([jnp.ones((ctx, HEAD_DIM // 2), F32), jnp.cos(ang)], axis=0)
    sin = jnp.concatenate([jnp.zeros((ctx, HEAD_DIM // 2), F32), jnp.sin(ang)], axis=0)
    return jnp.tile(jnp.concatenate([cos, cos], axis=1), (1, 2)), jnp.tile(jnp.concatenate([-sin, sin], axis=1), (1, 2))


def _lru_gate_weights(w_r, w_i):
    def dense(w):
        z = jnp.zeros((LRU_BLOCK_DIM, LRU_BLOCK_DIM), F32)
        return [jnp.block([[w[2 * hf], z], [z, w[2 * hf + 1]]]) for hf in range(2)]
    halves = [[], []]
    for d in range(N_DIR):
        for w in (w_r[d], w_i[d]):
            for hf, m in enumerate(dense(w)):
                halves[hf].append(m)
    return jnp.stack([jnp.concatenate(h, axis=1) for h in halves]).astype(BF16)


def _lru_vectors(lam, b_r, b_i):
    rows = [lam[0], lam[1], b_r[0], b_i[0], b_r[1], b_i[1], jnp.zeros_like(lam[0]), jnp.zeros_like(lam[0])]
    v = jnp.stack(rows)
    return jnp.stack([v[:, :LANES], v[:, LANES:]])


def kernel(x, c, ctx, c_ctx, w_ada, b_ada, g_norm, w_in, g_qk, lru_conv_w, lru_conv_b, lru_w_r, lru_b_r, lru_w_i, lru_b_i, lru_lambda, dn_conv_w, dn_a_log, dn_dt_bias, g_dn_out, g_group, w_out, w_ffn_in, w_ffn_out):
    depth = w_ada.shape[0]
    bsz, n, d = x.shape
    m = ctx.shape[1]
    assert m == TM and d == D_MODEL and bsz == 2 and n % ATT_TK == 0

    cvec = jnp.concatenate([c, c_ctx[None, :], jnp.zeros((SUBLANES - bsz - 1, d), F32)], axis=0)
    mods = _modulation(cvec, w_ada, b_ada)
    cosf, sinf = _rope_tables(m, n)
    ones128 = _block_diag_ones(LANES)
    ones256 = _block_diag_ones(DN_WIDTH)

    xs = jnp.concatenate([ctx, x], axis=1)
    for l in range(depth):
        last = l == depth - 1
        off = 1 if last else 0
        mod4 = mods[l].reshape(SUBLANES, 6, 1, d)
        wl = w_in[l]
        w_in_p = jnp.concatenate([wl[:, 0:768], wl[:, 1280:2048], wl[:, 1024:1280], wl[:, 768:1024],
                                  wl[:, 2048:2320], jnp.zeros((d, IN_COLS_PAD - 2320), F32)], axis=1).astype(BF16)
        q, k, v, rest = _in_proj(xs, mod4, g_norm[l, 0][None, :], w_in_p, jnp.tile(g_qk[l], (1, 2)),
                                 cosf, sinf, ones128)
        att = _attention(q, k, v, m, off)

        cw = jnp.concatenate([dn_conv_w[l], lru_conv_w[l]], axis=1)
        cb = jnp.concatenate([jnp.zeros((3 * DN_WIDTH,), F32), lru_conv_b[l]])[None, :]
        pad = jnp.zeros((LANES - N_DIR * DN_HEADS,), F32)
        dnv = jnp.stack([jnp.concatenate([dn_a_log[l].reshape(-1), pad]),
                         jnp.concatenate([dn_dt_bias[l].reshape(-1), pad])])
        xr, qh, kh, vh, gb = _prep(rest, cw, cb, dnv, ones256)
        y = _lru(xr, _lru_gate_weights(lru_w_r[l], lru_w_i[l]),
                 _lru_vectors(lru_lambda[l], lru_b_r[l], lru_b_i[l]), m)
        o0, o1 = _deltanet(qh, kh, vh, gb, m)

        x1 = _out_proj(xs, att, y, rest, o0, o1, g_group[l][None, :], jnp.tile(g_dn_out[l], DN_HEADS)[None, :],
                       ones256, w_out[l].astype(BF16), mod4, g_norm[l, 1][None, :], off)
        xs = _ffn(x1, mod4, g_norm[l, 2][None, :], g_norm[l, 3][None, :],
                  w_ffn_in[l].astype(BF16), w_ffn_out[l].astype(BF16), off)
    return xs
```

```python
import functools

import jax
import jax.numpy as jnp
from jax import lax
from jax.experimental import pallas as pl
from jax.experimental.pallas import tpu as pltpu

F32 = jnp.float32
BF16 = jnp.bfloat16

D_MODEL = 1024
GRID_W = 64
EPS = 1e-6
ATT_GROUP = 4
ATT_KV_HEADS = 2
HEAD_DIM = 64
ATT_WIDTH = 512
ROPE_THETA = 10000.0
LRU_WIDTH = 256
LRU_BLOCK_DIM = 64
LRU_C = 8.0
DN_HEADS = 4
DN_WIDTH = 256
DN_CHUNK = 64
DN_BASE = 8
N_DIR = 2
FFN_HIDDEN = 2816
FFN_CHUNK = 256

LANES = 128
SUBLANES = 8
TM = 256
ATT_TK = 512
ATT_VROWS = 80
LOG2E = 1.4426950408889634
QKV_COLS = 768
REST_COLS = 1664
IN_COLS_PAD = QKV_COLS + REST_COLS
VMEM_LIMIT = 56 * 1024 * 1024


def _cparams(*sem):
    return pltpu.CompilerParams(dimension_semantics=sem, vmem_limit_bytes=VMEM_LIMIT)


def _rms(x, gain):
    ms = jnp.mean(x * x, axis=-1, keepdims=True)
    return x * lax.rsqrt(ms + EPS) * gain


def _seg_sumsq(x, ones_bd):
    sq = x * x
    hi = sq.astype(BF16)
    lo = (sq - hi.astype(F32)).astype(BF16)
    return (jnp.dot(hi, ones_bd, preferred_element_type=F32)
            + jnp.dot(lo, ones_bd, preferred_element_type=F32))


def _silu(x):
    return x * jax.nn.sigmoid(x)


def _softplus(x):
    return jnp.maximum(x, 0.0) + jnp.log1p(jnp.exp(-jnp.abs(x)))


def _mod_row(b, i):
    return jnp.where(i == 0, 2, b)


def _mod_kernel(c_ref, w_ref, b_ref, o_ref):
    s = _silu(c_ref[...])
    o_ref[0] = jnp.dot(s.astype(BF16), w_ref[0].astype(BF16), preferred_element_type=F32) + b_ref[0]


def _modulation(cvec, w_ada, b_ada):
    depth = w_ada.shape[0]
    d = D_MODEL
    return pl.pallas_call(
        _mod_kernel,
        grid=(depth, 6),
        in_specs=[pl.BlockSpec((SUBLANES, d), lambda l, j: (0, 0)),
                  pl.BlockSpec((1, d, d), lambda l, j: (l, 0, j)),
                  pl.BlockSpec((1, 1, d), lambda l, j: (l, 0, j))],
        out_specs=pl.BlockSpec((1, SUBLANES, d), lambda l, j: (l, 0, j)),
        out_shape=jax.ShapeDtypeStruct((depth, SUBLANES, 6 * d), F32),
        compiler_params=_cparams("arbitrary", "arbitrary"),
        name="modulation",
    )(cvec, w_ada, b_ada.reshape(depth, 1, 6 * d))


def _in_kernel(x_ref, sh_ref, sc_ref, gn_ref, w_ref, gqk_ref, cos_ref, sin_ref, ones_ref,
               qt_ref, k_ref, vt_ref, rest_ref):
    x = x_ref[0]
    h = _rms(x, gn_ref[...]) * (1.0 + sc_ref[0, 0]) + sh_ref[0, 0]
    hb = h.astype(BF16)
    qkv = jnp.dot(hb, w_ref[:, 0:QKV_COLS], preferred_element_type=F32)
    cosf = cos_ref[...]
    sinf = sin_ref[...]
    lane = lax.broadcasted_iota(jnp.int32, (TM, LANES), 1)
    first_half = (lane & 32) == 0
    ones_bd = ones_ref[...]

    def norm_rope(slab, gain):
        ss = _seg_sumsq(slab, ones_bd)
        y = slab * lax.rsqrt(ss * (1.0 / HEAD_DIM) + EPS) * gain
        partner = jnp.where(first_half, pltpu.roll(y, 96, axis=1), pltpu.roll(y, 32, axis=1))
        return y * cosf + partner * sinf

    for s in range(ATT_WIDTH // LANES):
        qs = norm_rope(qkv[:, LANES * s:LANES * (s + 1)], gqk_ref[0:1, :]) * (LOG2E * HEAD_DIM ** -0.5)
        qst = qs.T.astype(BF16)
        for e in range(2):
            g, hh = divmod(2 * s + e, ATT_GROUP)
            qt_ref[0, g, 0, :, TM * hh:TM * (hh + 1)] = qst[HEAD_DIM * e:HEAD_DIM * (e + 1), :]
    kk = norm_rope(qkv[:, 512:640], gqk_ref[1:2, :]).astype(BF16)
    vvt = qkv[:, 640:768].T.astype(BF16)
    row = lax.broadcasted_iota(jnp.int32, (ATT_VROWS - HEAD_DIM, TM), 0)
    ones_rows = jnp.where(row == 0, 1.0, 0.0).astype(BF16)
    for g in range(ATT_KV_HEADS):
        k_ref[0, g] = kk[:, HEAD_DIM * g:HEAD_DIM * (g + 1)]
        vt_ref[0, g, 0, 0:HEAD_DIM, :] = vvt[HEAD_DIM * g:HEAD_DIM * (g + 1), :]
        vt_ref[0, g, 0, HEAD_DIM:, :] = ones_rows
    rest_ref[0] = jnp.dot(hb, w_ref[:, QKV_COLS:], preferred_element_type=F32)


def _in_proj(xs, mod4, gn0, w_in_p, gqk2, cosf, sinf, ones128):
    b, t, d = xs.shape
    nb = t // TM
    return pl.pallas_call(
        _in_kernel,
        grid=(b, nb),
        in_specs=[pl.BlockSpec((1, TM, d), lambda bi, i: (bi, i, 0)),
                  pl.BlockSpec((1, 1, 1, d), lambda bi, i: (_mod_row(bi, i), 0, 0, 0)),
                  pl.BlockSpec((1, 1, 1, d), lambda bi, i: (_mod_row(bi, i), 1, 0, 0)),
                  pl.BlockSpec((1, d), lambda bi, i: (0, 0)),
                  pl.BlockSpec((d, IN_COLS_PAD), lambda bi, i: (0, 0)),
                  pl.BlockSpec((2, LANES), lambda bi, i: (0, 0)),
                  pl.BlockSpec((TM, LANES), lambda bi, i: (i, 0)),
                  pl.BlockSpec((TM, LANES), lambda bi, i: (i, 0)),
                  pl.BlockSpec((LANES, LANES), lambda bi, i: (0, 0))],
        out_specs=[pl.BlockSpec((1, ATT_KV_HEADS, 1, HEAD_DIM, ATT_GROUP * TM), lambda bi, i: (bi, 0, i, 0, 0)),
                   pl.BlockSpec((1, ATT_KV_HEADS, TM, HEAD_DIM), lambda bi, i: (bi, 0, i, 0)),
                   pl.BlockSpec((1, ATT_KV_HEADS, 1, ATT_VROWS, TM), lambda bi, i: (bi, 0, i, 0, 0)),
                   pl.BlockSpec((1, TM, REST_COLS), lambda bi, i: (bi, i, 0))],
        out_shape=[jax.ShapeDtypeStruct((b, ATT_KV_HEADS, nb, HEAD_DIM, ATT_GROUP * TM), BF16),
                   jax.ShapeDtypeStruct((b, ATT_KV_HEADS, t, HEAD_DIM), BF16),
                   jax.ShapeDtypeStruct((b, ATT_KV_HEADS, nb, ATT_VROWS, TM), BF16),
                   jax.ShapeDtypeStruct((b, t, REST_COLS), F32)],
        compiler_params=_cparams("arbitrary", "arbitrary"),
        name="in_proj",
    )(xs, mod4, mod4, gn0, w_in_p, gqk2, cosf, sinf, ones128)


def _att_kernel(qt_ref, k_ref, vt_ref, o_ref, m_ref, acc_ref, s_ref, *, ctx, n_lat, q_off):
    i = pl.program_id(2) + q_off
    qt = qt_ref[0, 0, 0]
    nsub = ATT_TK // TM

    def scores(start, size):
        return jnp.dot(k_ref[0, 0, pl.ds(start, size), :], qt, preferred_element_type=F32)

    def pv(pt, blk0, nblk):
        out = None
        for j in range(nblk):
            t = jnp.dot(vt_ref[0, 0, blk0 + j], pt[j * TM:(j + 1) * TM, :], preferred_element_type=F32)
            out = t if out is None else out + t
        return out

    st = scores(0, ctx)
    m0 = jnp.max(st, axis=0, keepdims=True)
    m_ref[...] = m0
    acc_ref[...] = pv(jnp.exp2(st - m0).astype(BF16), 0, ctx // TM)

    n_chunks = n_lat // ATT_TK

    def step(j, cur, nxt):
        jn = jnp.minimum(j + 1, n_chunks - 1)
        s_ref[nxt] = scores(pl.multiple_of(ctx + jn * ATT_TK, TM), ATT_TK)
        st = s_ref[cur]
        m_prev = m_ref[...]
        m_new = jnp.maximum(m_prev, jnp.max(st, axis=0, keepdims=True))
        alpha = jnp.exp2(m_prev - m_new)
        pt = jnp.exp2(st - m_new).astype(BF16)
        acc_ref[...] = alpha * acc_ref[...] + pv(pt, ctx // TM + j * nsub, nsub)
        m_ref[...] = m_new

    def body(jj, carry):
        step(2 * jj, 0, 1)
        step(2 * jj + 1, 1, 0)
        return carry

    @pl.when(i > 0)
    def _():
        s_ref[0] = scores(ctx, ATT_TK)
        lax.fori_loop(0, n_chunks // 2, body, 0)

    acc = acc_ref[...]
    out_t = acc[0:HEAD_DIM] / acc[HEAD_DIM:HEAD_DIM + 1]
    for h in range(ATT_GROUP):
        o_ref[0, :, HEAD_DIM * h:HEAD_DIM * (h + 1)] = out_t[:, TM * h:TM * (h + 1)].T


def _attention(qt, k, vt, ctx, q_off):
    b, _, t, _ = k.shape
    nb = t // TM
    nq = nb - q_off
    gw = ATT_GROUP * HEAD_DIM
    kern = functools.partial(_att_kernel, ctx=ctx, n_lat=t - ctx, q_off=q_off)
    return pl.pallas_call(
        kern,
        grid=(b, ATT_KV_HEADS, nq),
        in_specs=[pl.BlockSpec((1, 1, 1, HEAD_DIM, ATT_GROUP * TM), lambda bi, g, i: (bi, g, i + q_off, 0, 0)),
                  pl.BlockSpec((1, 1, t, HEAD_DIM), lambda bi, g, i: (bi, g, 0, 0)),
                  pl.BlockSpec((1, 1, nb, ATT_VROWS, TM), lambda bi, g, i: (bi, g, 0, 0, 0))],
        out_specs=pl.BlockSpec((1, TM, gw), lambda bi, g, i: (bi, i, g)),
        out_shape=jax.ShapeDtypeStruct((b, nq * TM, ATT_WIDTH), F32),
        scratch_shapes=[pltpu.VMEM((1, ATT_GROUP * TM), F32),
                        pltpu.VMEM((ATT_VROWS, ATT_GROUP * TM), F32),
                        pltpu.VMEM((2, ATT_TK, ATT_GROUP * TM), F32)],
        compiler_params=_cparams("arbitrary", "arbitrary", "arbitrary"),
        name="attention",
    )(qt, k, vt)


def _prep_kernel(cur_ref, prv_ref, nxt_ref, cw_ref, cb_ref, ab_ref, dnv_ref, ones_ref,
                 xr_ref, q_ref, k_ref, v_ref, gb_ref, ext_ref):
    i = pl.program_id(1)
    nb = pl.num_programs(1)
    has_prev = i >= 2
    has_next = jnp.logical_and(i >= 1, i < nb - 1)
    ext_ref[0:SUBLANES, :] = jnp.where(has_prev, prv_ref[0], 0.0)
    ext_ref[SUBLANES:SUBLANES + TM, :] = cur_ref[0]
    ext_ref[SUBLANES + TM:2 * SUBLANES + TM, :] = jnp.where(has_next, nxt_ref[0], 0.0)
    conv = cb_ref[...]
    for j in range(4):
        conv = conv + ext_ref[SUBLANES - 1 + j:SUBLANES - 1 + j + TM, :] * cw_ref[j:j + 1, :]
    xr_ref[0] = conv[:, 3 * DN_WIDTH:]
    act = _silu(conv[:, 0:3 * DN_WIDTH])
    qq = act[:, 0:DN_WIDTH]
    kk = act[:, DN_WIDTH:2 * DN_WIDTH]
    vv = act[:, 2 * DN_WIDTH:3 * DN_WIDTH]
    ones_bd = ones_ref[...]
    qn = qq * lax.rsqrt(_seg_sumsq(qq, ones_bd) + EPS) * (DN_CHUNK ** -0.5)
    kn = kk * lax.rsqrt(_seg_sumsq(kk, ones_bd) + EPS)
    for h in range(DN_HEADS):
        sl = slice(DN_CHUNK * h, DN_CHUNK * (h + 1))
        q_ref[0, h] = qn[:, sl]
        k_ref[0, h] = kn[:, sl]
        v_ref[0, h] = vv[:, sl]
    ab = ab_ref[0]
    g = -jnp.exp(dnv_ref[0:1, :]) * _softplus(ab + dnv_ref[1:2, :])
    beta = jax.nn.sigmoid(ab)
    lane = lax.broadcasted_iota(jnp.int32, ab.shape, 1)
    gb_ref[0] = jnp.where(lane < N_DIR * DN_HEADS, g, beta)[:, 0:2 * N_DIR * DN_HEADS]


def _prep(rest, cw, cb, dnv, ones256):
    b, t, _ = rest.shape
    nb = t // TM
    cc = 4 * DN_WIDTH
    r8 = TM // SUBLANES
    hm = jax.ShapeDtypeStruct((b, DN_HEADS, t, DN_CHUNK), F32)
    hm_spec = pl.BlockSpec((1, DN_HEADS, TM, DN_CHUNK), lambda bi, i: (bi, 0, i, 0))
    return pl.pallas_call(
        _prep_kernel,
        grid=(b, nb),
        in_specs=[pl.BlockSpec((1, TM, cc), lambda bi, i: (bi, i, 0)),
                  pl.BlockSpec((1, SUBLANES, cc), lambda bi, i: (bi, jnp.maximum(i * r8 - 1, 0), 0)),
                  pl.BlockSpec((1, SUBLANES, cc), lambda bi, i: (bi, jnp.minimum((i + 1) * r8, nb * r8 - 1), 0)),
                  pl.BlockSpec((4, cc), lambda bi, i: (0, 0)),
                  pl.BlockSpec((1, cc), lambda bi, i: (0, 0)),
                  pl.BlockSpec((1, TM, LANES), lambda bi, i: (bi, i, 12)),
                  pl.BlockSpec((2, LANES), lambda bi, i: (0, 0)),
                  pl.BlockSpec((DN_WIDTH, DN_WIDTH), lambda bi, i: (0, 0))],
        out_specs=[pl.BlockSpec((1, TM, LRU_WIDTH), lambda bi, i: (bi, i, 0)),
                   hm_spec, hm_spec, hm_spec,
                   pl.BlockSpec((1, TM, 2 * N_DIR * DN_HEADS), lambda bi, i: (bi, i, 0))],
        out_shape=[jax.ShapeDtypeStruct((b, t, LRU_WIDTH), F32), hm, hm, hm,
                   jax.ShapeDtypeStruct((b, t, 2 * N_DIR * DN_HEADS), F32)],
        scratch_shapes=[pltpu.VMEM((TM + 2 * SUBLANES, cc), F32)],
        compiler_params=_cparams("arbitrary", "arbitrary"),
        name="conv_prep",
    )(rest, rest, rest, cw, cb, rest, dnv, ones256)


def _lru_kernel(xr_ref, wg_ref, vec_ref, y_ref, a0, u0, a1, u1, *, ctx, n):
    nseg = SUBLANES
    vec = vec_ref[0]
    a_refs = (a0, a1)
    u_refs = (u0, u1)

    def gates(x, d):
        g = jnp.dot(x.astype(BF16), wg_ref[0, :, 2 * LANES * d:2 * LANES * (d + 1)],
                    preferred_element_type=F32)
        r = jax.nn.sigmoid(g[:, :LANES] + vec[2 + 2 * d:3 + 2 * d])
        ig = jax.nn.sigmoid(g[:, LANES:] + vec[3 + 2 * d:4 + 2 * d])
        log_a = -LRU_C * r * _softplus(-vec[d:d + 1])
        a = jnp.exp(log_a)
        mult = jnp.sqrt(-jnp.tanh(log_a) * (a * a + 1.0))
        return a, mult * (ig * x)

    def run(row0, seg, base, init):
        pitch = seg + SUBLANES
        for j in range(nseg):
            x = xr_ref[0, row0 + j * seg:row0 + (j + 1) * seg, :]
            for d in range(N_DIR):
                a, u = gates(x, d)
                a_refs[d][base + j * pitch:base + j * pitch + seg, :] = a
                u_refs[d][base + j * pitch:base + j * pitch + seg, :] = u

        def step(t, carry):
            out = []
            for d in range(N_DIR):
                h, p = carry[2 * d], carry[2 * d + 1]
                r = base + (t if d == 0 else seg - 1 - t)
                idx = pl.ds(r, nseg, stride=pitch)
                a = a_refs[d][idx, :]
                h = a * h + u_refs[d][idx, :]
                p = a * p
                u_refs[d][idx, :] = h
                a_refs[d][idx, :] = p
                out += [h, p]
            return tuple(out)

        z = jnp.zeros((nseg, LANES), F32)
        o = jnp.ones((nseg, LANES), F32)
        fin = lax.fori_loop(0, seg, step, (z, o, z, o), unroll=8)
        carries, finals = [], []
        for d in range(N_DIR):
            hl, pl_ = fin[2 * d], fin[2 * d + 1]
            order = list(range(nseg)) if d == 0 else list(range(nseg - 1, -1, -1))
            c = init[d]
            cs = {}
            for j in order:
                cs[j] = c
                c = hl[j:j + 1] + pl_[j:j + 1] * c
            carries.append(cs)
            finals.append(c)
        for j in range(nseg):
            sl = slice(base + j * pitch, base + j * pitch + seg)
            y = (u0[sl, :] + a0[sl, :] * carries[0][j]) + (u1[sl, :] + a1[sl, :] * carries[1][j])
            y_ref[0, row0 + j * seg:row0 + (j + 1) * seg, :] = y
        return finals

    zero = jnp.zeros((1, LANES), F32)
    seg_c = ctx // nseg
    fin_c = run(0, seg_c, 0, [zero, zero])
    run(ctx, n // nseg, nseg * (seg_c + SUBLANES), fin_c)


def _lru(xr, wg, vec, ctx):
    b, t, _ = xr.shape
    n = t - ctx
    rows = SUBLANES * (ctx // SUBLANES + SUBLANES) + SUBLANES * (n // SUBLANES + SUBLANES)
    kern = functools.partial(_lru_kernel, ctx=ctx, n=n)
    return pl.pallas_call(
        kern,
        grid=(b, LRU_WIDTH // LANES),
        in_specs=[pl.BlockSpec((1, t, LANES), lambda bi, hf: (bi, 0, hf)),
                  pl.BlockSpec((1, LANES, 4 * LANES), lambda bi, hf: (hf, 0, 0)),
                  pl.BlockSpec((1, SUBLANES, LANES), lambda bi, hf: (hf, 0, 0))],
        out_specs=pl.BlockSpec((1, t, LANES), lambda bi, hf: (bi, 0, hf)),
        out_shape=jax.ShapeDtypeStruct((b, t, LRU_WIDTH), F32),
        scratch_shapes=[pltpu.VMEM((rows, LANES), F32)] * 4,
        compiler_params=_cparams("arbitrary", "arbitrary"),
        name="rglru",
    )(xr, wg, vec)


def _dn_kernel(qf_ref, kf_ref, vf_ref, gf_ref, qb_ref, kb_ref, vb_ref, gbk_ref, o0_ref, o1_ref, s_ref):
    c_ = DN_CHUNK

    @pl.when(pl.program_id(1) == 0)
    def _():
        s_ref[...] = jnp.zeros_like(s_ref)

    ii = lax.broadcasted_iota(jnp.int32, (c_, c_), 0)
    jj = lax.broadcasted_iota(jnp.int32, (c_, c_), 1)
    eye = ii == jj
    eye_f = jnp.where(eye, 1.0, 0.0)
    shift = DN_BASE.bit_length() - 1
    same_blk = {DN_BASE: (ii >> shift) == (jj >> shift)}
    off_blk = ({}, {})
    size = DN_BASE
    while size < c_:
        shift = size.bit_length() - 1
        bi, bj = ii >> shift, jj >> shift
        off_blk[0][size] = jnp.logical_and(bi == bj + 1, (bi & 1) == 1)
        off_blk[1][size] = jnp.logical_and(bj == bi + 1, (bj & 1) == 1)
        size *= 2
    nt = (((1,), (1,)), ((), ()))
    tn = (((0,), (0,)), ((), ()))
    per_block = qf_ref.shape[2] // c_
    in_refs = ((qf_ref, kf_ref, vf_ref, gf_ref), (qb_ref, kb_ref, vb_ref, gbk_ref))
    o_refs = (o0_ref, o1_ref)

    def bdot(a, b, dims=None):
        if dims is None:
            return jnp.dot(a.astype(BF16), b.astype(BF16), preferred_element_type=F32)
        return lax.dot_general(a.astype(BF16), b.astype(BF16), dims, preferred_element_type=F32)

    probs = []
    for d in range(N_DIR):
        incl = (jj <= ii) if d == 0 else (jj >= ii)
        strict = (jj < ii) if d == 0 else (jj > ii)
        incl_t = (ii <= jj) if d == 0 else (ii >= jj)
        q_ref, k_ref, v_ref, g_ref = in_refs[d]
        for step in range(per_block):
            c = step if d == 0 else per_block - 1 - step
            rows = slice(c_ * c, c_ * (c + 1))
            gbv = g_ref[0, rows, :]
            for h in range(DN_HEADS):
                col = DN_HEADS * d + h
                q = q_ref[0, h, rows, :]
                k = k_ref[0, h, rows, :]
                v = v_ref[0, h, rows, :]
                g_col = gbv[:, col:col + 1]
                b_col = gbv[:, N_DIR * DN_HEADS + col:N_DIR * DN_HEADS + col + 1]
                g_full = jnp.broadcast_to(g_col, (c_, c_))
                g_row = jnp.sum(jnp.where(eye, g_full, 0.0), axis=0, keepdims=True)
                gc_col = jnp.sum(jnp.where(incl, jnp.broadcast_to(g_row, (c_, c_)), 0.0), axis=1, keepdims=True)
                gc_row = jnp.sum(jnp.where(incl_t, g_full, 0.0), axis=0, keepdims=True)
                g_tot = jnp.sum(g_row, axis=1, keepdims=True)
                decay = jnp.where(incl, jnp.exp(jnp.where(incl, gc_col - gc_row, 0.0)), 0.0)
                eg = jnp.exp(gc_col)
                kb = k * b_col
                probs.append(dict(d=d, step=step, h=h, col=col, rows=rows, incl=incl, strict=strict, k=k,
                                  decay=decay, kbq=jnp.concatenate([kb, q], axis=0),
                                  rhs=jnp.concatenate([v * b_col, kb * eg], axis=1), qd=q * eg,
                                  kt=k * jnp.exp(g_tot - gc_col), gt=jnp.exp(g_tot)))
    for p in probs:
        a1 = bdot(p["kbq"], p["k"], nt)
        p["lower"] = jnp.where(p["strict"], a1[:c_] * p["decay"], 0.0)
        p["attn"] = jnp.where(p["incl"], a1[c_:] * p["decay"], 0.0)
        p["pw"] = jnp.where(same_blk[DN_BASE], p["lower"], 0.0)
        p["tm"] = eye_f - p["pw"]
    for _ in range(2):
        for p in probs:
            p["pw"] = bdot(p["pw"], p["pw"])
        for p in probs:
            p["tm"] = p["tm"] + bdot(p["tm"], p["pw"])
    size = DN_BASE
    while size < c_:
        for p in probs:
            p["tmp"] = bdot(jnp.where(off_blk[p["d"]][size], p["lower"], 0.0), p["tm"])
        for p in probs:
            p["tm"] = p["tm"] - bdot(p["tm"], p["tmp"])
        size *= 2
    for p in probs:
        sol = bdot(p["tm"], p["rhs"])
        p["u"] = sol[:, :c_]
        p["wqd"] = jnp.concatenate([sol[:, c_:], p["qd"]], axis=0)
    for step in range(per_block):
        cur = [p for p in probs if p["step"] == step]
        for p in cur:
            p["s_old"] = s_ref[p["col"]]
            p["a2"] = bdot(p["wqd"], p["s_old"])
        for p in cur:
            p["v_new"] = p["u"] - p["a2"][:c_]
        for p in cur:
            o = p["a2"][c_:] + bdot(p["attn"], p["v_new"])
            o_refs[p["d"]][0, p["rows"], c_ * p["h"]:c_ * (p["h"] + 1)] = o
        for p in cur:
            s_ref[p["col"]] = p["s_old"] * p["gt"] + bdot(p["kt"], p["v_new"], tn)


def _deltanet(qh, kh, vh, gb, ctx):
    b, _, t, _ = qh.shape
    blk = 2 * DN_CHUNK
    np_ = t // blk
    npc = ctx // blk

    def bwd(p):
        return jnp.where(p < npc, npc - 1 - p, np_ - 1 - (p - npc))

    hm_f = pl.BlockSpec((1, DN_HEADS, blk, DN_CHUNK), lambda bi, p: (bi, 0, p, 0))
    hm_b = pl.BlockSpec((1, DN_HEADS, blk, DN_CHUNK), lambda bi, p: (bi, 0, bwd(p), 0))
    g_f = pl.BlockSpec((1, blk, 2 * N_DIR * DN_HEADS), lambda bi, p: (bi, p, 0))
    g_b = pl.BlockSpec((1, blk, 2 * N_DIR * DN_HEADS), lambda bi, p: (bi, bwd(p), 0))
    o_shape = jax.ShapeDtypeStruct((b, t, DN_WIDTH), F32)
    return pl.pallas_call(
        _dn_kernel,
        grid=(b, np_),
        in_specs=[hm_f, hm_f, hm_f, g_f, hm_b, hm_b, hm_b, g_b],
        out_specs=[pl.BlockSpec((1, blk, DN_WIDTH), lambda bi, p: (bi, p, 0)),
                   pl.BlockSpec((1, blk, DN_WIDTH), lambda bi, p: (bi, bwd(p), 0))],
        out_shape=[o_shape, o_shape],
        scratch_shapes=[pltpu.VMEM((N_DIR * DN_HEADS, DN_CHUNK, DN_CHUNK), F32)],
        compiler_params=_cparams("arbitrary", "arbitrary"),
        name="deltanet",
    )(qh, kh, vh, gb, qh, kh, vh, gb)


def _out_kernel(x_ref, att_ref, y_ref, lg_ref, o0_ref, o1_ref, dz_ref, gg_ref, gdn_ref, ones_ref, w_ref,
                gt_ref, gn_ref, xo_ref):
    gg = gg_ref[...]
    a_n = _rms(att_ref[0], gg[:, 0:ATT_WIDTH]).astype(BF16)
    lg = lg_ref[0]
    gelu = lg * (0.5 * (1.0 + jnp.tanh(0.7978845608028654 * (lg + 0.044715 * (lg * lg * lg)))))
    l_n = _rms(gelu * y_ref[0], gg[:, ATT_WIDTH:]).astype(BF16)
    o = o0_ref[0] + o1_ref[0]
    ss = _seg_sumsq(o, ones_ref[...])
    dn = ((o * lax.rsqrt(ss * (1.0 / DN_CHUNK) + EPS) * gdn_ref[...]) * _silu(dz_ref[0])).astype(BF16)
    mix = (jnp.dot(a_n, w_ref[0:ATT_WIDTH, :], preferred_element_type=F32)
           + jnp.dot(l_n, w_ref[ATT_WIDTH:ATT_WIDTH + LRU_WIDTH, :], preferred_element_type=F32)
           + jnp.dot(dn, w_ref[ATT_WIDTH + LRU_WIDTH:, :], preferred_element_type=F32))
    xo_ref[0] = x_ref[0] + gt_ref[0, 0] * _rms(mix, gn_ref[...])


def _out_proj(xs, att, y, rest, o0, o1, gg, gdn4, ones256, w_out, mod4, gn1, off):
    b, t, d = xs.shape
    nb = t // TM - off
    att_off = off - (t - att.shape[1]) // TM

    def tok(width, col=0):
        return pl.BlockSpec((1, TM, width), lambda bi, i: (bi, i + off, col))

    def const(shape):
        return pl.BlockSpec(shape, lambda bi, i: (0,) * len(shape))

    return pl.pallas_call(
        _out_kernel,
        grid=(b, nb),
        in_specs=[tok(d),
                  pl.BlockSpec((1, TM, ATT_WIDTH), lambda bi, i: (bi, i + att_off, 0)),
                  tok(LRU_WIDTH), tok(LRU_WIDTH, 4), tok(DN_WIDTH), tok(DN_WIDTH), tok(DN_WIDTH, 5),
                  const((1, ATT_WIDTH + LRU_WIDTH)), const((1, DN_WIDTH)), const((DN_WIDTH, DN_WIDTH)),
                  const((d, d)),
                  pl.BlockSpec((1, 1, 1, d), lambda bi, i: (_mod_row(bi, i + off), 2, 0, 0)),
                  const((1, d))],
        out_specs=pl.BlockSpec((1, TM, d), lambda bi, i: (bi, i, 0)),
        out_shape=jax.ShapeDtypeStruct((b, nb * TM, d), F32),
        compiler_params=_cparams("arbitrary", "arbitrary"),
        name="out_proj",
    )(xs, att, y, rest, o0, o1, rest, gg, gdn4, ones256, w_out, mod4, gn1)


def _ffn_kernel(x_ref, sh_ref, sc_ref, gt_ref, gn2_ref, gn3_ref, wi_ref, wo_ref, o_ref):
    x = x_ref[0]
    h = (_rms(x, gn2_ref[...]) * (1.0 + sc_ref[0, 0]) + sh_ref[0, 0]).astype(BF16)
    acc = jnp.zeros(x.shape, F32)
    for j in range(FFN_HIDDEN // FFN_CHUNK):
        lo = j * FFN_CHUNK
        gate = jnp.dot(h, wi_ref[:, lo:lo + FFN_CHUNK], preferred_element_type=F32)
        up = jnp.dot(h, wi_ref[:, FFN_HIDDEN + lo:FFN_HIDDEN + lo + FFN_CHUNK], preferred_element_type=F32)
        act = (_silu(gate) * up).astype(BF16)
        acc = acc + jnp.dot(act, wo_ref[lo:lo + FFN_CHUNK, :], preferred_element_type=F32)
    o_ref[0] = x + gt_ref[0, 0] * _rms(acc, gn3_ref[...])


def _ffn(x1, mod4, gn2, gn3, w_ffn_in, w_ffn_out, off):
    b, rows, d = x1.shape
    nb = rows // TM

    def mod(k):
        return pl.BlockSpec((1, 1, 1, d), lambda bi, i: (_mod_row(bi, i + off), k, 0, 0))

    def const(shape):
        return pl.BlockSpec(shape, lambda bi, i: (0,) * len(shape))

    return pl.pallas_call(
        _ffn_kernel,
        grid=(b, nb),
        in_specs=[pl.BlockSpec((1, TM, d), lambda bi, i: (bi, i, 0)),
                  mod(3), mod(4), mod(5), const((1, d)), const((1, d)),
                  const((d, 2 * FFN_HIDDEN)), const((FFN_HIDDEN, d))],
        out_specs=pl.BlockSpec((1, TM, d), lambda bi, i: (bi, i, 0)),
        out_shape=jax.ShapeDtypeStruct((b, rows, d), F32),
        compiler_params=_cparams("arbitrary", "arbitrary"),
        name="ffn",
    )(x1, mod4, mod4, mod4, gn2, gn3, w_ffn_in, w_ffn_out)


def _block_diag_ones(n):
    i = jnp.arange(n) // HEAD_DIM
    return (i[:, None] == i[None, :]).astype(BF16)


def _rope_tables(ctx, n):
    t = jnp.arange(n)
    row = (t // GRID_W).astype(F32)
    col = (t % GRID_W).astype(F32)
    n_freq = HEAD_DIM // 4
    inv_freq = ROPE_THETA ** (-jnp.arange(n_freq, dtype=F32) / n_freq)
    ang = jnp.concatenate([row[:, None] * inv_freq, col[:, None] * inv_freq], axis=-1)
    cos = jnp.concatenate([jnp.ones((ctx, HEAD_DIM // 2), F32), jnp.cos(ang)], axis=0)
    sin = jnp.concatenate([jnp.zeros((ctx, HEAD_DIM // 2), F32), jnp.sin(ang)], axis=0)
    return jnp.tile(jnp.concatenate([cos, cos], axis=1), (1, 2)), jnp.tile(jnp.concatenate([-sin, sin], axis=1), (1, 2))


def _lru_gate_weights(w_r, w_i):
    def dense(w):
        z = jnp.zeros((LRU_BLOCK_DIM, LRU_BLOCK_DIM), F32)
        return [jnp.block([[w[2 * hf], z], [z, w[2 * hf + 1]]]) for hf in range(2)]
    halves = [[], []]
    for d in range(N_DIR):
        for w in (w_r[d], w_i[d]):
            for hf, m in enumerate(dense(w)):
                halves[hf].append(m)
    return jnp.stack([jnp.concatenate(h, axis=1) for h in halves]).astype(BF16)


def _lru_vectors(lam, b_r, b_i):
    rows = [lam[0], lam[1], b_r[0], b_i[0], b_r[1], b_i[1], jnp.zeros_like(lam[0]), jnp.zeros_like(lam[0])]
    v = jnp.stack(rows)
    return jnp.stack([v[:, :LANES], v[:, LANES:]])


def kernel(x, c, ctx, c_ctx, w_ada, b_ada, g_norm, w_in, g_qk, lru_conv_w, lru_conv_b, lru_w_r, lru_b_r, lru_w_i, lru_b_i, lru_lambda, dn_conv_w, dn_a_log, dn_dt_bias, g_dn_out, g_group, w_out, w_ffn_in, w_ffn_out):
    depth = w_ada.shape[0]
    bsz, n, d = x.shape
    m = ctx.shape[1]
    assert m == TM and d == D_MODEL and bsz == 2 and n % (2 * ATT_TK) == 0

    cvec = jnp.concatenate([c, c_ctx[None, :], jnp.zeros((SUBLANES - bsz - 1, d), F32)], axis=0)
    mods = _modulation(cvec, w_ada, b_ada)
    cosf, sinf = _rope_tables(m, n)
    ones128 = _block_diag_ones(LANES)
    ones256 = _block_diag_ones(DN_WIDTH)

    xs = jnp.concatenate([ctx, x], axis=1)
    for l in range(depth):
        last = l == depth - 1
        off = 1 if last else 0
        mod4 = mods[l].reshape(SUBLANES, 6, 1, d)
        wl = w_in[l]
        w_in_p = jnp.concatenate([wl[:, 0:768], wl[:, 1280:2048], wl[:, 1024:1280], wl[:, 768:1024],
                                  wl[:, 2048:2320], jnp.zeros((d, IN_COLS_PAD - 2320), F32)], axis=1).astype(BF16)
        q, k, v, rest = _in_proj(xs, mod4, g_norm[l, 0][None, :], w_in_p, jnp.tile(g_qk[l], (1, 2)),
                                 cosf, sinf, ones128)
        att = _attention(q, k, v, m, off)

        cw = jnp.concatenate([dn_conv_w[l], lru_conv_w[l]], axis=1)
        cb = jnp.concatenate([jnp.zeros((3 * DN_WIDTH,), F32), lru_conv_b[l]])[None, :]
        pad = jnp.zeros((LANES - N_DIR * DN_HEADS,), F32)
        dnv = jnp.stack([jnp.concatenate([dn_a_log[l].reshape(-1), pad]),
                         jnp.concatenate([dn_dt_bias[l].reshape(-1), pad])])
        xr, qh, kh, vh, gb = _prep(rest, cw, cb, dnv, ones256)
        y = _lru(xr, _lru_gate_weights(lru_w_r[l], lru_w_i[l]),
                 _lru_vectors(lru_lambda[l], lru_b_r[l], lru_b_i[l]), m)
        o0, o1 = _deltanet(qh, kh, vh, gb, m)

        x1 = _out_proj(xs, att, y, rest, o0, o1, g_group[l][None, :], jnp.tile(g_dn_out[l], DN_HEADS)[None, :],
                       ones256, w_out[l].astype(BF16), mod4, g_norm[l, 1][None, :], off)
        xs = _ffn(x1, mod4, g_norm[l, 2][None, :], g_norm[l, 3][None, :],
                  w_ffn_in[l].astype(BF16), w_ffn_out[l].astype(BF16), off)
    return xs
```

```python
import functools

import jax
import jax.numpy as jnp
from jax import lax
from jax.experimental import pallas as pl
from jax.experimental.pallas import tpu as pltpu

F32 = jnp.float32
BF16 = jnp.bfloat16

D_MODEL = 1024
GRID_W = 64
EPS = 1e-6
ATT_GROUP = 4
ATT_KV_HEADS = 2
HEAD_DIM = 64
ATT_WIDTH = 512
ROPE_THETA = 10000.0
LRU_WIDTH = 256
LRU_BLOCK_DIM = 64
LRU_C = 8.0
DN_HEADS = 4
DN_WIDTH = 256
DN_CHUNK = 64
DN_BASE = 8
DN_BLOCK = 2 * DN_CHUNK
DN_PACK = 5 * DN_WIDTH
N_DIR = 2
FFN_HIDDEN = 2816
FFN_CHUNK = 256

LANES = 128
SUBLANES = 8
TM = 256
ATT_TK = 1024
ATT_VROWS = 80
LOG2E = 1.4426950408889634
QKV_COLS = 768
REST_COLS = 1664
IN_COLS_PAD = QKV_COLS + REST_COLS
VMEM_LIMIT = 56 * 1024 * 1024


def _cparams(*sem):
    return pltpu.CompilerParams(dimension_semantics=sem, vmem_limit_bytes=VMEM_LIMIT)


def _rms(x, gain):
    ms = jnp.mean(x * x, axis=-1, keepdims=True)
    return x * lax.rsqrt(ms + EPS) * gain


def _seg_sumsq(x, ones_bd):
    sq = x * x
    hi = sq.astype(BF16)
    lo = (sq - hi.astype(F32)).astype(BF16)
    return (jnp.dot(hi, ones_bd, preferred_element_type=F32)
            + jnp.dot(lo, ones_bd, preferred_element_type=F32))


def _silu(x):
    return x * jax.nn.sigmoid(x)


def _softplus(x):
    return jnp.maximum(x, 0.0) + jnp.log1p(jnp.exp(-jnp.abs(x)))


def _mod_row(b, i):
    return jnp.where(i == 0, 2, b)


def _mod_kernel(c_ref, w_ref, b_ref, o_ref):
    s = _silu(c_ref[...])
    o_ref[0] = jnp.dot(s.astype(BF16), w_ref[0].astype(BF16), preferred_element_type=F32) + b_ref[0]


def _modulation(cvec, w_ada, b_ada):
    depth = w_ada.shape[0]
    d = D_MODEL
    return pl.pallas_call(
        _mod_kernel,
        grid=(depth, 6),
        in_specs=[pl.BlockSpec((SUBLANES, d), lambda l, j: (0, 0)),
                  pl.BlockSpec((1, d, d), lambda l, j: (l, 0, j)),
                  pl.BlockSpec((1, 1, d), lambda l, j: (l, 0, j))],
        out_specs=pl.BlockSpec((1, SUBLANES, d), lambda l, j: (l, 0, j)),
        out_shape=jax.ShapeDtypeStruct((depth, SUBLANES, 6 * d), F32),
        compiler_params=_cparams("arbitrary", "arbitrary"),
        name="modulation",
    )(cvec, w_ada, b_ada.reshape(depth, 1, 6 * d))


def _in_kernel(x_ref, sh_ref, sc_ref, gn_ref, w_ref, gqk_ref, cos_ref, sin_ref, ones_ref,
               qt_ref, k_ref, vt_ref, rest_ref):
    x = x_ref[0]
    h = _rms(x, gn_ref[...]) * (1.0 + sc_ref[0, 0]) + sh_ref[0, 0]
    hb = h.astype(BF16)
    qkv = jnp.dot(hb, w_ref[:, 0:QKV_COLS], preferred_element_type=F32)
    cosf = cos_ref[...]
    sinf = sin_ref[...]
    lane = lax.broadcasted_iota(jnp.int32, (TM, LANES), 1)
    first_half = (lane & 32) == 0
    ones_bd = ones_ref[...]

    def norm_rope(slab, gain):
        ss = _seg_sumsq(slab, ones_bd)
        y = slab * lax.rsqrt(ss * (1.0 / HEAD_DIM) + EPS) * gain
        partner = jnp.where(first_half, pltpu.roll(y, 96, axis=1), pltpu.roll(y, 32, axis=1))
        return y * cosf + partner * sinf

    for s in range(ATT_WIDTH // LANES):
        qs = norm_rope(qkv[:, LANES * s:LANES * (s + 1)], gqk_ref[0:1, :]) * (LOG2E * HEAD_DIM ** -0.5)
        qst = qs.T.astype(BF16)
        for e in range(2):
            g, hh = divmod(2 * s + e, ATT_GROUP)
            qt_ref[0, g, 0, :, TM * hh:TM * (hh + 1)] = qst[HEAD_DIM * e:HEAD_DIM * (e + 1), :]
    kk = norm_rope(qkv[:, 512:640], gqk_ref[1:2, :]).astype(BF16)
    vvt = qkv[:, 640:768].T.astype(BF16)
    row = lax.broadcasted_iota(jnp.int32, (ATT_VROWS - HEAD_DIM, TM), 0)
    ones_rows = jnp.where(row == 0, 1.0, 0.0).astype(BF16)
    for g in range(ATT_KV_HEADS):
        k_ref[0, g] = kk[:, HEAD_DIM * g:HEAD_DIM * (g + 1)]
        vt_ref[0, g, 0, 0:HEAD_DIM, :] = vvt[HEAD_DIM * g:HEAD_DIM * (g + 1), :]
        vt_ref[0, g, 0, HEAD_DIM:, :] = ones_rows
    rest_ref[0] = jnp.dot(hb, w_ref[:, QKV_COLS:], preferred_element_type=F32)


def _in_proj(xs, mod4, gn0, w_in_p, gqk2, cosf, sinf, ones128):
    b, t, d = xs.shape
    nb = t // TM
    return pl.pallas_call(
        _in_kernel,
        grid=(b, nb),
        in_specs=[pl.BlockSpec((1, TM, d), lambda bi, i: (bi, i, 0)),
                  pl.BlockSpec((1, 1, 1, d), lambda bi, i: (_mod_row(bi, i), 0, 0, 0)),
                  pl.BlockSpec((1, 1, 1, d), lambda bi, i: (_mod_row(bi, i), 1, 0, 0)),
                  pl.BlockSpec((1, d), lambda bi, i: (0, 0)),
                  pl.BlockSpec((d, IN_COLS_PAD), lambda bi, i: (0, 0)),
                  pl.BlockSpec((2, LANES), lambda bi, i: (0, 0)),
                  pl.BlockSpec((TM, LANES), lambda bi, i: (i, 0)),
                  pl.BlockSpec((TM, LANES), lambda bi, i: (i, 0)),
                  pl.BlockSpec((LANES, LANES), lambda bi, i: (0, 0))],
        out_specs=[pl.BlockSpec((1, ATT_KV_HEADS, 1, HEAD_DIM, ATT_GROUP * TM), lambda bi, i: (bi, 0, i, 0, 0)),
                   pl.BlockSpec((1, ATT_KV_HEADS, TM, HEAD_DIM), lambda bi, i: (bi, 0, i, 0)),
                   pl.BlockSpec((1, ATT_KV_HEADS, 1, ATT_VROWS, TM), lambda bi, i: (bi, 0, i, 0, 0)),
                   pl.BlockSpec((1, TM, REST_COLS), lambda bi, i: (bi, i, 0))],
        out_shape=[jax.ShapeDtypeStruct((b, ATT_KV_HEADS, nb, HEAD_DIM, ATT_GROUP * TM), BF16),
                   jax.ShapeDtypeStruct((b, ATT_KV_HEADS, t, HEAD_DIM), BF16),
                   jax.ShapeDtypeStruct((b, ATT_KV_HEADS, nb, ATT_VROWS, TM), BF16),
                   jax.ShapeDtypeStruct((b, t, REST_COLS), F32)],
        compiler_params=_cparams("arbitrary", "arbitrary"),
        name="in_proj",
    )(xs, mod4, mod4, gn0, w_in_p, gqk2, cosf, sinf, ones128)


def _att_kernel(qt_ref, k_ref, vt_ref, o_ref, m_ref, acc_ref, s_ref, cmax_ref, *, ctx, n_lat, q_off):
    i = pl.program_id(2) + q_off
    qt = qt_ref[0, 0, 0]
    nsub = ATT_TK // TM

    def scores(start, size):
        return jnp.dot(k_ref[0, 0, pl.ds(start, size), :], qt, preferred_element_type=F32)

    def pv(pt, blk0, nblk):
        out = None
        for j in range(nblk):
            t = jnp.dot(vt_ref[0, 0, blk0 + j], pt[j * TM:(j + 1) * TM, :], preferred_element_type=F32)
            out = t if out is None else out + t
        return out

    st = scores(0, ctx)
    m0 = jnp.max(st, axis=0, keepdims=True)
    m_ref[...] = m0
    acc_ref[...] = pv(jnp.exp2(st - m0).astype(BF16), 0, ctx // TM)

    n_chunks = n_lat // ATT_TK

    def step(j, cur, nxt):
        jn = jnp.minimum(j + 1, n_chunks - 1)
        sn = scores(pl.multiple_of(ctx + jn * ATT_TK, TM), ATT_TK)
        s_ref[nxt] = sn
        cmax_ref[nxt] = jnp.max(sn, axis=0, keepdims=True)
        st = s_ref[cur]
        m_prev = m_ref[...]
        m_new = jnp.maximum(m_prev, cmax_ref[cur])
        alpha = jnp.exp2(m_prev - m_new)
        pt = jnp.exp2(st - m_new).astype(BF16)
        acc_ref[...] = alpha * acc_ref[...] + pv(pt, ctx // TM + j * nsub, nsub)
        m_ref[...] = m_new

    def body(jj, carry):
        step(2 * jj, 0, 1)
        step(2 * jj + 1, 1, 0)
        return carry

    @pl.when(i > 0)
    def _():
        s0 = scores(ctx, ATT_TK)
        s_ref[0] = s0
        cmax_ref[0] = jnp.max(s0, axis=0, keepdims=True)
        lax.fori_loop(0, n_chunks // 2, body, 0)

    acc = acc_ref[...]
    out_t = acc[0:HEAD_DIM] / acc[HEAD_DIM:HEAD_DIM + 1]
    for h in range(ATT_GROUP):
        o_ref[0, :, HEAD_DIM * h:HEAD_DIM * (h + 1)] = out_t[:, TM * h:TM * (h + 1)].T


def _attention(qt, k, vt, ctx, q_off):
    b, _, t, _ = k.shape
    nb = t // TM
    nq = nb - q_off
    gw = ATT_GROUP * HEAD_DIM
    kern = functools.partial(_att_kernel, ctx=ctx, n_lat=t - ctx, q_off=q_off)
    return pl.pallas_call(
        kern,
        grid=(b, ATT_KV_HEADS, nq),
        in_specs=[pl.BlockSpec((1, 1, 1, HEAD_DIM, ATT_GROUP * TM), lambda bi, g, i: (bi, g, i + q_off, 0, 0)),
                  pl.BlockSpec((1, 1, t, HEAD_DIM), lambda bi, g, i: (bi, g, 0, 0)),
                  pl.BlockSpec((1, 1, nb, ATT_VROWS, TM), lambda bi, g, i: (bi, g, 0, 0, 0))],
        out_specs=pl.BlockSpec((1, TM, gw), lambda bi, g, i: (bi, i, g)),
        out_shape=jax.ShapeDtypeStruct((b, nq * TM, ATT_WIDTH), F32),
        scratch_shapes=[pltpu.VMEM((1, ATT_GROUP * TM), F32),
                        pltpu.VMEM((ATT_VROWS, ATT_GROUP * TM), F32),
                        pltpu.VMEM((2, ATT_TK, ATT_GROUP * TM), F32),
                        pltpu.VMEM((2, 1, ATT_GROUP * TM), F32)],
        compiler_params=_cparams("arbitrary", "arbitrary", "arbitrary"),
        name="attention",
    )(qt, k, vt)


def _split3(x):
    hi = x.astype(BF16)
    r = x - hi.astype(F32)
    mid = r.astype(BF16)
    lo = (r - mid.astype(F32)).astype(BF16)
    return hi, mid, lo


def _dot_parts(parts, mat, left=False):
    out = None
    for p in parts:
        t = (jnp.dot(mat, p, preferred_element_type=F32) if left else jnp.dot(p, mat, preferred_element_type=F32))
        out = t if out is None else out + t
    return out


def _prep_kernel(cur_ref, prv_ref, nxt_ref, cw_ref, cb_ref, ab_ref, dnv_ref, ones_ref, tri_ref, xg_ref, xb_ref,
                 xr_ref, qk_ref, pk_ref, ge_ref, rows_ref):
    i = pl.program_id(1)
    nb = pl.num_programs(1)
    c_, w_ = DN_CHUNK, DN_WIDTH
    has_prev = i >= 2
    has_next = jnp.logical_and(i >= 1, i < nb - 1)
    cur = cur_ref[0]
    prv = jnp.where(has_prev, prv_ref[0], 0.0)
    nxt = jnp.where(has_next, nxt_ref[0], 0.0)
    trow = lax.broadcasted_iota(jnp.int32, cur.shape, 0)
    x_m1 = jnp.where(trow == 0, prv[SUBLANES - 1:SUBLANES, :], pltpu.roll(cur, 1, axis=0))
    x_p1 = jnp.where(trow == TM - 1, nxt[0:1, :], pltpu.roll(cur, TM - 1, axis=0))
    x_p2 = jnp.where(trow == TM - 2, nxt[0:1, :],
                     jnp.where(trow == TM - 1, nxt[1:2, :], pltpu.roll(cur, TM - 2, axis=0)))
    conv = (cb_ref[...] + x_m1 * cw_ref[0:1, :] + cur * cw_ref[1:2, :]
            + x_p1 * cw_ref[2:3, :] + x_p2 * cw_ref[3:4, :])
    xr_ref[0] = conv[:, 3 * w_:]
    act = _silu(conv[:, 0:3 * w_])
    qq = act[:, 0:w_]
    kk = act[:, w_:2 * w_]
    vv = act[:, 2 * w_:3 * w_]
    ones_bd = ones_ref[...]
    qn = qq * lax.rsqrt(_seg_sumsq(qq, ones_bd) + EPS) * (c_ ** -0.5)
    kn = kk * lax.rsqrt(_seg_sumsq(kk, ones_bd) + EPS)
    qk_ref[0, :, 0:w_] = qn.astype(BF16)
    qk_ref[0, :, w_:2 * w_] = kn.astype(BF16)

    ab = ab_ref[0]
    lane = lax.broadcasted_iota(jnp.int32, ab.shape, 1)
    g = jnp.where(lane < N_DIR * DN_HEADS, -jnp.exp(dnv_ref[0:1, :]) * _softplus(ab + dnv_ref[1:2, :]), 0.0)
    beta = jax.nn.sigmoid(ab)
    g_parts = _split3(g)
    gc = jnp.where(lane < DN_HEADS, _dot_parts(g_parts, tri_ref[0], left=True),
                   _dot_parts(g_parts, tri_ref[1], left=True))
    tot = _dot_parts(g_parts, tri_ref[2], left=True)
    ge = _dot_parts(_split3(gc), xg_ref[...])
    te = _dot_parts(_split3(tot), xg_ref[...])
    be = _dot_parts(_split3(beta)[:2], xb_ref[...])
    ge_ref[0] = ge
    row = lax.broadcasted_iota(jnp.int32, (c_, w_), 0)
    diag = row == (lax.broadcasted_iota(jnp.int32, (c_, w_), 1) & (c_ - 1))
    for d in range(N_DIR):
        ge_d = ge[:, w_ * d:w_ * (d + 1)]
        te_d = te[:, w_ * d:w_ * (d + 1)]
        be_d = be[:, w_ * d:w_ * (d + 1)]
        eg = jnp.exp(ge_d)
        kb = kn * be_d
        pk_ref[0, d, :, 0:w_] = kb.astype(BF16)
        pk_ref[0, d, :, w_:2 * w_] = (vv * be_d).astype(BF16)
        pk_ref[0, d, :, 2 * w_:3 * w_] = (kb * eg).astype(BF16)
        pk_ref[0, d, :, 3 * w_:4 * w_] = (qn * eg).astype(BF16)
        pk_ref[0, d, :, 4 * w_:5 * w_] = (kn * jnp.exp(te_d - ge_d)).astype(BF16)
        for pr in range(TM // DN_BLOCK):
            for c in range(DN_BLOCK // c_):
                r0 = DN_BLOCK * pr + c_ * c
                rows_ref[0, d, pr, c:c + 1, :] = jnp.sum(jnp.where(diag, ge_d[r0:r0 + c_, :], 0.0), axis=0,
                                                         keepdims=True)
                rows_ref[0, d, pr, 2 + c:3 + c, :] = te_d[r0:r0 + 1, :]


def _prep(rest, cw, cb, dnv, ones256, tri, xg, xb):
    b, t, _ = rest.shape
    nb = t // TM
    cc = 4 * DN_WIDTH
    r8 = TM // SUBLANES
    ppb = TM // DN_BLOCK

    def const(shape):
        return pl.BlockSpec(shape, lambda bi, i: (0,) * len(shape))

    return pl.pallas_call(
        _prep_kernel,
        grid=(b, nb),
        in_specs=[pl.BlockSpec((1, TM, cc), lambda bi, i: (bi, i, 0)),
                  pl.BlockSpec((1, SUBLANES, cc), lambda bi, i: (bi, jnp.maximum(i * r8 - 1, 0), 0)),
                  pl.BlockSpec((1, SUBLANES, cc), lambda bi, i: (bi, jnp.minimum((i + 1) * r8, nb * r8 - 1), 0)),
                  const((4, cc)), const((1, cc)),
                  pl.BlockSpec((1, TM, LANES), lambda bi, i: (bi, i, 12)),
                  const((2, LANES)), const((DN_WIDTH, DN_WIDTH)), const((3, TM, TM)),
                  const((LANES, N_DIR * DN_WIDTH)), const((LANES, N_DIR * DN_WIDTH))],
        out_specs=[pl.BlockSpec((1, TM, LRU_WIDTH), lambda bi, i: (bi, i, 0)),
                   pl.BlockSpec((1, TM, 2 * DN_WIDTH), lambda bi, i: (bi, i, 0)),
                   pl.BlockSpec((1, N_DIR, TM, DN_PACK), lambda bi, i: (bi, 0, i, 0)),
                   pl.BlockSpec((1, TM, N_DIR * DN_WIDTH), lambda bi, i: (bi, i, 0)),
                   pl.BlockSpec((1, N_DIR, ppb, 4, DN_WIDTH), lambda bi, i: (bi, 0, i, 0, 0))],
        out_shape=[jax.ShapeDtypeStruct((b, t, LRU_WIDTH), F32),
                   jax.ShapeDtypeStruct((b, t, 2 * DN_WIDTH), BF16),
                   jax.ShapeDtypeStruct((b, N_DIR, t, DN_PACK), BF16),
                   jax.ShapeDtypeStruct((b, t, N_DIR * DN_WIDTH), F32),
                   jax.ShapeDtypeStruct((b, N_DIR, t // DN_BLOCK, 4, DN_WIDTH), F32)],
        compiler_params=_cparams("arbitrary", "arbitrary"),
        name="conv_prep",
    )(rest, rest, rest, cw, cb, rest, dnv, ones256, tri, xg, xb)


def _lru_kernel(xr_ref, wg_ref, vec_ref, y_ref, a0, u0, a1, u1, *, ctx, n):
    nseg = SUBLANES
    vec = vec_ref[0]
    a_refs = (a0, a1)
    u_refs = (u0, u1)

    def gates(x, d):
        g = jnp.dot(x.astype(BF16), wg_ref[0, :, 2 * LANES * d:2 * LANES * (d + 1)],
                    preferred_element_type=F32)
        r = jax.nn.sigmoid(g[:, :LANES] + vec[2 + 2 * d:3 + 2 * d])
        ig = jax.nn.sigmoid(g[:, LANES:] + vec[3 + 2 * d:4 + 2 * d])
        log_a = -LRU_C * r * _softplus(-vec[d:d + 1])
        a = jnp.exp(log_a)
        mult = jnp.sqrt(-jnp.tanh(log_a) * (a * a + 1.0))
        return a, mult * (ig * x)

    def run(row0, seg, base, init):
        pitch = seg + SUBLANES
        for j in range(nseg):
            x = xr_ref[0, row0 + j * seg:row0 + (j + 1) * seg, :]
            for d in range(N_DIR):
                a, u = gates(x, d)
                a_refs[d][base + j * pitch:base + j * pitch + seg, :] = a
                u_refs[d][base + j * pitch:base + j * pitch + seg, :] = u

        def step(t, carry):
            out = []
            for d in range(N_DIR):
                h, p = carry[2 * d], carry[2 * d + 1]
                r = base + (t if d == 0 else seg - 1 - t)
                idx = pl.ds(r, nseg, stride=pitch)
                a = a_refs[d][idx, :]
                h = a * h + u_refs[d][idx, :]
                p = a * p
                u_refs[d][idx, :] = h
                a_refs[d][idx, :] = p
                out += [h, p]
            return tuple(out)

        z = jnp.zeros((nseg, LANES), F32)
        o = jnp.ones((nseg, LANES), F32)
        fin = lax.fori_loop(0, seg, step, (z, o, z, o), unroll=8)
        carries, finals = [], []
        for d in range(N_DIR):
            hl, pl_ = fin[2 * d], fin[2 * d + 1]
            order = list(range(nseg)) if d == 0 else list(range(nseg - 1, -1, -1))
            c = init[d]
            cs = {}
            for j in order:
                cs[j] = c
                c = hl[j:j + 1] + pl_[j:j + 1] * c
            carries.append(cs)
            finals.append(c)
        for j in range(nseg):
            sl = slice(base + j * pitch, base + j * pitch + seg)
            y = (u0[sl, :] + a0[sl, :] * carries[0][j]) + (u1[sl, :] + a1[sl, :] * carries[1][j])
            y_ref[0, row0 + j * seg:row0 + (j + 1) * seg, :] = y
        return finals

    zero = jnp.zeros((1, LANES), F32)
    seg_c = ctx // nseg
    fin_c = run(0, seg_c, 0, [zero, zero])
    run(ctx, n // nseg, nseg * (seg_c + SUBLANES), fin_c)


def _lru(xr, wg, vec, ctx):
    b, t, _ = xr.shape
    n = t - ctx
    rows = SUBLANES * (ctx // SUBLANES + SUBLANES) + SUBLANES * (n // SUBLANES + SUBLANES)
    kern = functools.partial(_lru_kernel, ctx=ctx, n=n)
    return pl.pallas_call(
        kern,
        grid=(b, LRU_WIDTH // LANES),
        in_specs=[pl.BlockSpec((1, t, LANES), lambda bi, hf: (bi, 0, hf)),
                  pl.BlockSpec((1, LANES, 4 * LANES), lambda bi, hf: (hf, 0, 0)),
                  pl.BlockSpec((1, SUBLANES, LANES), lambda bi, hf: (hf, 0, 0))],
        out_specs=pl.BlockSpec((1, t, LANES), lambda bi, hf: (bi, 0, hf)),
        out_shape=jax.ShapeDtypeStruct((b, t, LRU_WIDTH), F32),
        scratch_shapes=[pltpu.VMEM((rows, LANES), F32)] * 4,
        compiler_params=_cparams("arbitrary", "arbitrary"),
        name="rglru",
    )(xr, wg, vec)


def _dn_kernel(qkf_ref, pkf_ref, gef_ref, rwf_ref, qkb_ref, pkb_ref, geb_ref, rwb_ref, o0_ref, o1_ref, s_ref):
    c_, w_ = DN_CHUNK, DN_WIDTH
    n_batch = o0_ref.shape[0]

    @pl.when(pl.program_id(0) == 0)
    def _():
        s_ref[...] = jnp.zeros_like(s_ref)

    row = lax.broadcasted_iota(jnp.int32, (c_, w_), 0)
    lane = lax.broadcasted_iota(jnp.int32, (c_, w_), 1)
    lj = lane & (c_ - 1)
    head_masks = [(lane >> 6) == h for h in range(DN_HEADS)]
    eye_f = jnp.where(row == lj, 1.0, 0.0)
    shift = DN_BASE.bit_length() - 1
    same_base = (row >> shift) == (lj >> shift)
    off_blk = ({}, {})
    size = DN_BASE
    while size < c_:
        shift = size.bit_length() - 1
        bi, bj = row >> shift, lj >> shift
        off_blk[0][size] = jnp.logical_and(bi == bj + 1, (bi & 1) == 1)
        off_blk[1][size] = jnp.logical_and(bj == bi + 1, (bj & 1) == 1)
        size *= 2
    bd_mask = ((lax.broadcasted_iota(jnp.int32, (w_, w_), 0) >> 6)
               == (lax.broadcasted_iota(jnp.int32, (w_, w_), 1) >> 6))
    nt = (((1,), (1,)), ((), ()))
    tn = (((0,), (0,)), ((), ()))

    def bd(x):
        xb = x.astype(BF16)
        zero = jnp.zeros_like(xb)
        return jnp.concatenate([jnp.where(hm, xb, zero) for hm in head_masks], axis=0)

    def bdot(a, b, dims=None):
        if dims is None:
            return jnp.dot(a.astype(BF16), b, preferred_element_type=F32)
        return lax.dot_general(a.astype(BF16), b, dims, preferred_element_type=F32)

    per_block = DN_BLOCK // c_
    in_refs = ((qkf_ref, pkf_ref, gef_ref, rwf_ref), (qkb_ref, pkb_ref, geb_ref, rwb_ref))
    o_refs = (o0_ref, o1_ref)
    probs = []
    for d in range(N_DIR):
        incl = (lj <= row) if d == 0 else (lj >= row)
        strict = (lj < row) if d == 0 else (lj > row)
        qk_ref, pk_ref, ge_ref, rw_ref = in_refs[d]
        for bi, step in [(bi, step) for bi in range(n_batch) for step in range(per_block)]:
            c = step if d == 0 else per_block - 1 - step
            rows = slice(c_ * c, c_ * (c + 1))
            decay = jnp.where(incl, jnp.exp(jnp.where(incl, ge_ref[bi, rows, :] - rw_ref[bi, 0, 0, c:c + 1, :], 0.0)),
                              0.0)
            probs.append(dict(d=d, bi=bi, step=step, rows=rows, incl=incl, strict=strict, decay=decay,
                              st=N_DIR * bi + d,
                              kbq=jnp.concatenate([pk_ref[bi, 0, rows, 0:w_], qk_ref[bi, rows, 0:w_]], axis=0),
                              k=qk_ref[bi, rows, w_:2 * w_], vb=pk_ref[bi, 0, rows, w_:2 * w_],
                              kbe=pk_ref[bi, 0, rows, 2 * w_:3 * w_], qd=pk_ref[bi, 0, rows, 3 * w_:4 * w_],
                              kt=pk_ref[bi, 0, rows, 4 * w_:5 * w_],
                              gt=jnp.exp(rw_ref[bi, 0, 0, per_block + c:per_block + c + 1, :])))
    for p in probs:
        a1 = bdot(p["kbq"], bd(p["k"]), nt)
        p["lower"] = jnp.where(p["strict"], a1[:c_] * p["decay"], 0.0)
        p["attn"] = jnp.where(p["incl"], a1[c_:] * p["decay"], 0.0)
        p["pw"] = jnp.where(same_base, p["lower"], 0.0)
        p["tm"] = eye_f - p["pw"]
    for p in probs:
        p["pw"] = bdot(p["pw"], bd(p["pw"]))
    for p in probs:
        t = bdot(jnp.concatenate([p["tm"], p["pw"]], axis=0), bd(p["pw"]))
        p["tm"] = p["tm"] + t[:c_]
        p["pw"] = t[c_:]
    for p in probs:
        p["tm"] = p["tm"] + bdot(p["tm"], bd(p["pw"]))
    size = DN_BASE
    while size < c_:
        for p in probs:
            p["tmp"] = bdot(jnp.where(off_blk[p["d"]][size], p["lower"], 0.0), bd(p["tm"]))
        for p in probs:
            p["tm"] = p["tm"] - bdot(p["tm"], bd(p["tmp"]))
        size *= 2
    for p in probs:
        sol = bdot(p["tm"], jnp.concatenate([bd(p["vb"]), bd(p["kbe"])], axis=1))
        p["u"] = sol[:, :w_]
        p["wqd"] = jnp.concatenate([sol[:, w_:].astype(BF16), p["qd"]], axis=0)
    for step in range(per_block):
        cur = [p for p in probs if p["step"] == step]
        for p in cur:
            p["s_old"] = s_ref[p["st"]]
            p["a2"] = bdot(p["wqd"], p["s_old"].astype(BF16))
        for p in cur:
            p["v_new"] = p["u"] - p["a2"][:c_]
        for p in cur:
            o_refs[p["d"]][p["bi"], p["rows"], :] = p["a2"][c_:] + bdot(p["attn"], bd(p["v_new"]))
        for p in cur:
            upd = bdot(p["kt"], p["v_new"].astype(BF16), tn)
            s_ref[p["st"]] = p["s_old"] * p["gt"] + jnp.where(bd_mask, upd, 0.0)


def _deltanet(qk, pk, ge, rws, ctx):
    b, t, _ = qk.shape
    np_ = t // DN_BLOCK
    npc = ctx // DN_BLOCK

    def bwd(p):
        return jnp.where(p < npc, npc - 1 - p, np_ - 1 - (p - npc))

    def specs(d, pos):
        return [pl.BlockSpec((b, DN_BLOCK, 2 * DN_WIDTH), lambda p: (0, pos(p), 0)),
                pl.BlockSpec((b, 1, DN_BLOCK, DN_PACK), lambda p: (0, d, pos(p), 0)),
                pl.BlockSpec((b, DN_BLOCK, DN_WIDTH), lambda p: (0, pos(p), d)),
                pl.BlockSpec((b, 1, 1, 4, DN_WIDTH), lambda p: (0, d, pos(p), 0, 0))]

    o_shape = jax.ShapeDtypeStruct((b, t, DN_WIDTH), F32)
    return pl.pallas_call(
        _dn_kernel,
        grid=(np_,),
        in_specs=specs(0, lambda p: p) + specs(1, bwd),
        out_specs=[pl.BlockSpec((b, DN_BLOCK, DN_WIDTH), lambda p: (0, p, 0)),
                   pl.BlockSpec((b, DN_BLOCK, DN_WIDTH), lambda p: (0, bwd(p), 0))],
        out_shape=[o_shape, o_shape],
        scratch_shapes=[pltpu.VMEM((b * N_DIR, DN_WIDTH, DN_WIDTH), F32)],
        compiler_params=_cparams("arbitrary"),
        name="deltanet",
    )(qk, pk, ge, rws, qk, pk, ge, rws)


def _out_kernel(x_ref, att_ref, y_ref, lg_ref, o0_ref, o1_ref, dz_ref, gg_ref, gdn_ref, ones_ref, w_ref,
                gt_ref, gn_ref, xo_ref):
    gg = gg_ref[...]
    a_n = _rms(att_ref[0], gg[:, 0:ATT_WIDTH]).astype(BF16)
    lg = lg_ref[0]
    gelu = lg * (0.5 * (1.0 + jnp.tanh(0.7978845608028654 * (lg + 0.044715 * (lg * lg * lg)))))
    l_n = _rms(gelu * y_ref[0], gg[:, ATT_WIDTH:]).astype(BF16)
    o = o0_ref[0] + o1_ref[0]
    ss = _seg_sumsq(o, ones_ref[...])
    dn = ((o * lax.rsqrt(ss * (1.0 / DN_CHUNK) + EPS) * gdn_ref[...]) * _silu(dz_ref[0])).astype(BF16)
    mix = (jnp.dot(a_n, w_ref[0:ATT_WIDTH, :], preferred_element_type=F32)
           + jnp.dot(l_n, w_ref[ATT_WIDTH:ATT_WIDTH + LRU_WIDTH, :], preferred_element_type=F32)
           + jnp.dot(dn, w_ref[ATT_WIDTH + LRU_WIDTH:, :], preferred_element_type=F32))
    xo_ref[0] = x_ref[0] + gt_ref[0, 0] * _rms(mix, gn_ref[...])


def _out_proj(xs, att, y, rest, o0, o1, gg, gdn4, ones256, w_out, mod4, gn1, off):
    b, t, d = xs.shape
    nb = t // TM - off
    att_off = off - (t - att.shape[1]) // TM

    def tok(width, col=0):
        return pl.BlockSpec((1, TM, width), lambda bi, i: (bi, i + off, col))

    def const(shape):
        return pl.BlockSpec(shape, lambda bi, i: (0,) * len(shape))

    return pl.pallas_call(
        _out_kernel,
        grid=(b, nb),
        in_specs=[tok(d),
                  pl.BlockSpec((1, TM, ATT_WIDTH), lambda bi, i: (bi, i + att_off, 0)),
                  tok(LRU_WIDTH), tok(LRU_WIDTH, 4), tok(DN_WIDTH), tok(DN_WIDTH), tok(DN_WIDTH, 5),
                  const((1, ATT_WIDTH + LRU_WIDTH)), const((1, DN_WIDTH)), const((DN_WIDTH, DN_WIDTH)),
                  const((d, d)),
                  pl.BlockSpec((1, 1, 1, d), lambda bi, i: (_mod_row(bi, i + off), 2, 0, 0)),
                  const((1, d))],
        out_specs=pl.BlockSpec((1, TM, d), lambda bi, i: (bi, i, 0)),
        out_shape=jax.ShapeDtypeStruct((b, nb * TM, d), F32),
        compiler_params=_cparams("arbitrary", "arbitrary"),
        name="out_proj",
    )(xs, att, y, rest, o0, o1, rest, gg, gdn4, ones256, w_out, mod4, gn1)


def _ffn_kernel(x_ref, sh_ref, sc_ref, gt_ref, gn2_ref, gn3_ref, wi_ref, wo_ref, o_ref):
    x = x_ref[0]
    h = (_rms(x, gn2_ref[...]) * (1.0 + sc_ref[0, 0]) + sh_ref[0, 0]).astype(BF16)
    acc = jnp.zeros(x.shape, F32)
    for j in range(FFN_HIDDEN // FFN_CHUNK):
        lo = j * FFN_CHUNK
        gate = jnp.dot(h, wi_ref[:, lo:lo + FFN_CHUNK], preferred_element_type=F32)
        up = jnp.dot(h, wi_ref[:, FFN_HIDDEN + lo:FFN_HIDDEN + lo + FFN_CHUNK], preferred_element_type=F32)
        act = (_silu(gate) * up).astype(BF16)
        acc = acc + jnp.dot(act, wo_ref[lo:lo + FFN_CHUNK, :], preferred_element_type=F32)
    o_ref[0] = x + gt_ref[0, 0] * _rms(acc, gn3_ref[...])


def _ffn(x1, mod4, gn2, gn3, w_ffn_in, w_ffn_out, off):
    b, rows, d = x1.shape
    nb = rows // TM

    def mod(k):
        return pl.BlockSpec((1, 1, 1, d), lambda bi, i: (_mod_row(bi, i + off), k, 0, 0))

    def const(shape):
        return pl.BlockSpec(shape, lambda bi, i: (0,) * len(shape))

    return pl.pallas_call(
        _ffn_kernel,
        grid=(b, nb),
        in_specs=[pl.BlockSpec((1, TM, d), lambda bi, i: (bi, i, 0)),
                  mod(3), mod(4), mod(5), const((1, d)), const((1, d)),
                  const((d, 2 * FFN_HIDDEN)), const((FFN_HIDDEN, d))],
        out_specs=pl.BlockSpec((1, TM, d), lambda bi, i: (bi, i, 0)),
        out_shape=jax.ShapeDtypeStruct((b, rows, d), F32),
        compiler_params=_cparams("arbitrary", "arbitrary"),
        name="ffn",
    )(x1, mod4, mod4, mod4, gn2, gn3, w_ffn_in, w_ffn_out)


def _block_diag_ones(n):
    i = jnp.arange(n) // HEAD_DIM
    return (i[:, None] == i[None, :]).astype(BF16)


def _chunk_triangles(n):
    i = jnp.arange(n)
    same = (i[:, None] // DN_CHUNK) == (i[None, :] // DN_CHUNK)
    lower = jnp.logical_and(same, i[None, :] <= i[:, None])
    upper = jnp.logical_and(same, i[None, :] >= i[:, None])
    return jnp.stack([lower, upper, same]).astype(BF16)


def _head_spread(first_row):
    r = jnp.arange(LANES)[:, None]
    c = jnp.arange(N_DIR * DN_WIDTH)[None, :]
    return (r == first_row + c // DN_CHUNK).astype(BF16)


def _rope_tables(ctx, n):
    t = jnp.arange(n)
    row = (t // GRID_W).astype(F32)
    col = (t % GRID_W).astype(F32)
    n_freq = HEAD_DIM // 4
    inv_freq = ROPE_THETA ** (-jnp.arange(n_freq, dtype=F32) / n_freq)
    ang = jnp.concatenate([row[:, None] * inv_freq, col[:, None] * inv_freq], axis=-1)
    cos = jnp.concatenate([jnp.ones((ctx, HEAD_DIM // 2), F32), jnp.cos(ang)], axis=0)
    sin = jnp.concatenate([jnp.zeros((ctx, HEAD_DIM // 2), F32), jnp.sin(ang)], axis=0)
    return jnp.tile(jnp.concatenate([cos, cos], axis=1), (1, 2)), jnp.tile(jnp.concatenate([-sin, sin], axis=1), (1, 2))


def _lru_gate_weights(w_r, w_i):
    def dense(w):
        z = jnp.zeros((LRU_BLOCK_DIM, LRU_BLOCK_DIM), F32)
        return [jnp.block([[w[2 * hf], z], [z, w[2 * hf + 1]]]) for hf in range(2)]
    halves = [[], []]
    for d in range(N_DIR):
        for w in (w_r[d], w_i[d]):
            for hf, m in enumerate(dense(w)):
                halves[hf].append(m)
    return jnp.stack([jnp.concatenate(h, axis=1) for h in halves]).astype(BF16)


def _lru_vectors(lam, b_r, b_i):
    rows = [lam[0], lam[1], b_r[0], b_i[0], b_r[1], b_i[1], jnp.zeros_like(lam[0]), jnp.zeros_like(lam[0])]
    v = jnp.stack(rows)
    return jnp.stack([v[:, :LANES], v[:, LANES:]])


def kernel(x, c, ctx, c_ctx, w_ada, b_ada, g_norm, w_in, g_qk, lru_conv_w, lru_conv_b, lru_w_r, lru_b_r, lru_w_i, lru_b_i, lru_lambda, dn_conv_w, dn_a_log, dn_dt_bias, g_dn_out, g_group, w_out, w_ffn_in, w_ffn_out):
    depth = w_ada.shape[0]
    bsz, n, d = x.shape
    m = ctx.shape[1]
    assert m == TM and d == D_MODEL and bsz == 2 and n % (2 * ATT_TK) == 0

    cvec = jnp.concatenate([c, c_ctx[None, :], jnp.zeros((SUBLANES - bsz - 1, d), F32)], axis=0)
    mods = _modulation(cvec, w_ada, b_ada)
    cosf, sinf = _rope_tables(m, n)
    ones128 = _block_diag_ones(LANES)
    ones256 = _block_diag_ones(DN_WIDTH)
    tri = _chunk_triangles(TM)
    xg = _head_spread(0)
    xb = _head_spread(N_DIR * DN_HEADS)

    xs = jnp.concatenate([ctx, x], axis=1)
    for l in range(depth):
        last = l == depth - 1
        off = 1 if last else 0
        mod4 = mods[l].reshape(SUBLANES, 6, 1, d)
        wl = w_in[l]
        w_in_p = jnp.concatenate([wl[:, 0:768], wl[:, 1280:2048], wl[:, 1024:1280], wl[:, 768:1024],
                                  wl[:, 2048:2320], jnp.zeros((d, IN_COLS_PAD - 2320), F32)], axis=1).astype(BF16)
        qt, k, vt, rest = _in_proj(xs, mod4, g_norm[l, 0][None, :], w_in_p, jnp.tile(g_qk[l], (1, 2)),
                                   cosf, sinf, ones128)
        att = _attention(qt, k, vt, m, off)

        cw = jnp.concatenate([dn_conv_w[l], lru_conv_w[l]], axis=1)
        cb = jnp.concatenate([jnp.zeros((3 * DN_WIDTH,), F32), lru_conv_b[l]])[None, :]
        pad = jnp.zeros((LANES - N_DIR * DN_HEADS,), F32)
        dnv = jnp.stack([jnp.concatenate([dn_a_log[l].reshape(-1), pad]),
                         jnp.concatenate([dn_dt_bias[l].reshape(-1), pad])])
        xr, qk, pk, ge, rws = _prep(rest, cw, cb, dnv, ones256, tri, xg, xb)
        y = _lru(xr, _lru_gate_weights(lru_w_r[l], lru_w_i[l]),
                 _lru_vectors(lru_lambda[l], lru_b_r[l], lru_b_i[l]), m)
        o0, o1 = _deltanet(qk, pk, ge, rws, m)

        x1 = _out_proj(xs, att, y, rest, o0, o1, g_group[l][None, :], jnp.tile(g_dn_out[l], DN_HEADS)[None, :],
                       ones256, w_out[l].astype(BF16), mod4, g_norm[l, 1][None, :], off)
        xs = _ffn(x1, mod4, g_norm[l, 2][None, :], g_norm[l, 3][None, :],
                  w_ffn_in[l].astype(BF16), w_ffn_out[l].astype(BF16), off)
    return xs
```

```python
import functools

import jax
import jax.numpy as jnp
from jax import lax
from jax.experimental import pallas as pl
from jax.experimental.pallas import tpu as pltpu

F32 = jnp.float32
BF16 = jnp.bfloat16

D_MODEL = 1024
GRID_W = 64
EPS = 1e-6
ATT_GROUP = 4
ATT_KV_HEADS = 2
HEAD_DIM = 64
ATT_WIDTH = 512
ROPE_THETA = 10000.0
LRU_WIDTH = 256
LRU_BLOCK_DIM = 64
LRU_C = 8.0
DN_HEADS = 4
DN_WIDTH = 256
DN_CHUNK = 64
DN_BASE = 8
DN_BLOCK = 2 * DN_CHUNK
DN_PACK = 5 * DN_WIDTH
N_DIR = 2
FFN_HIDDEN = 2816
FFN_CHUNK = 1408

LANES = 128
SUBLANES = 8
TM = 256
ATT_TK = 1024
ATT_STEPS_PER_BODY = 4
ATT_VROWS = 80
LOG2E = 1.4426950408889634
ATT_FIXED_SHIFT_LIMIT = 48.0
QKV_COLS = 768
REST_COLS = 1664
IN_COLS_PAD = QKV_COLS + REST_COLS
VMEM_LIMIT = 56 * 1024 * 1024


def _cparams(*sem):
    return pltpu.CompilerParams(dimension_semantics=sem, vmem_limit_bytes=VMEM_LIMIT)


def _rms(x, gain):
    ms = jnp.mean(x * x, axis=-1, keepdims=True)
    return x * lax.rsqrt(ms + EPS) * gain


def _seg_sumsq(x, ones_bd):
    sq = x * x
    hi = sq.astype(BF16)
    lo = (sq - hi.astype(F32)).astype(BF16)
    return (jnp.dot(hi, ones_bd, preferred_element_type=F32)
            + jnp.dot(lo, ones_bd, preferred_element_type=F32))


def _silu(x):
    return x * jax.nn.sigmoid(x)


def _softplus(x):
    return jnp.maximum(x, 0.0) + jnp.log1p(jnp.exp(-jnp.abs(x)))


def _mod_row(b, i):
    return jnp.where(i == 0, 2, b)


def _mod_kernel(c_ref, w_ref, b_ref, o_ref):
    s = _silu(c_ref[...])
    o_ref[0] = jnp.dot(s.astype(BF16), w_ref[0].astype(BF16), preferred_element_type=F32) + b_ref[0]


def _modulation(cvec, w_ada, b_ada):
    depth = w_ada.shape[0]
    d = D_MODEL
    return pl.pallas_call(
        _mod_kernel,
        grid=(depth, 6),
        in_specs=[pl.BlockSpec((SUBLANES, d), lambda l, j: (0, 0)),
                  pl.BlockSpec((1, d, d), lambda l, j: (l, 0, j)),
                  pl.BlockSpec((1, 1, d), lambda l, j: (l, 0, j))],
        out_specs=pl.BlockSpec((1, SUBLANES, d), lambda l, j: (l, 0, j)),
        out_shape=jax.ShapeDtypeStruct((depth, SUBLANES, 6 * d), F32),
        compiler_params=_cparams("arbitrary", "arbitrary"),
        name="modulation",
    )(cvec, w_ada, b_ada.reshape(depth, 1, 6 * d))


def _in_kernel(x_ref, sh_ref, sc_ref, gn_ref, w_ref, gqk_ref, cos_ref, sin_ref, ones_ref,
               qt_ref, k_ref, vt_ref, rest_ref):
    x = x_ref[0]
    h = _rms(x, gn_ref[...]) * (1.0 + sc_ref[0, 0]) + sh_ref[0, 0]
    hb = h.astype(BF16)
    qkv = jnp.dot(hb, w_ref[:, 0:QKV_COLS], preferred_element_type=F32)
    cosf = cos_ref[...]
    sinf = sin_ref[...]
    lane = lax.broadcasted_iota(jnp.int32, (TM, LANES), 1)
    first_half = (lane & 32) == 0
    ones_bd = ones_ref[...]

    def norm_rope(slab, gain):
        ss = _seg_sumsq(slab, ones_bd)
        y = slab * lax.rsqrt(ss * (1.0 / HEAD_DIM) + EPS) * gain
        partner = jnp.where(first_half, pltpu.roll(y, 96, axis=1), pltpu.roll(y, 32, axis=1))
        return y * cosf + partner * sinf

    for s in range(ATT_WIDTH // LANES):
        qs = norm_rope(qkv[:, LANES * s:LANES * (s + 1)], gqk_ref[0:1, :]) * (LOG2E * HEAD_DIM ** -0.5)
        qst = qs.T.astype(BF16)
        for e in range(2):
            g, hh = divmod(2 * s + e, ATT_GROUP)
            qt_ref[0, g, 0, 0:HEAD_DIM, TM * hh:TM * (hh + 1)] = qst[HEAD_DIM * e:HEAD_DIM * (e + 1), :]
    for g in range(ATT_KV_HEADS):
        qt_ref[0, g, 0, HEAD_DIM:, :] = jnp.zeros((LANES - HEAD_DIM, ATT_GROUP * TM), BF16)
    kk = norm_rope(qkv[:, 512:640], gqk_ref[1:2, :])
    one_lane = jnp.where(lane == HEAD_DIM, 1.0, 0.0)
    vvt = qkv[:, 640:768].T.astype(BF16)
    row = lax.broadcasted_iota(jnp.int32, (ATT_VROWS - HEAD_DIM, TM), 0)
    ones_rows = jnp.where(row == 0, 1.0, 0.0).astype(BF16)
    for g in range(ATT_KV_HEADS):
        kg = kk if g == 0 else pltpu.roll(kk, HEAD_DIM, axis=1)
        k_ref[0, g] = jnp.where(lane < HEAD_DIM, kg, one_lane).astype(BF16)
        vt_ref[0, g, 0, 0:HEAD_DIM, :] = vvt[HEAD_DIM * g:HEAD_DIM * (g + 1), :]
        vt_ref[0, g, 0, HEAD_DIM:, :] = ones_rows
    rest_ref[0] = jnp.dot(hb, w_ref[:, QKV_COLS:], preferred_element_type=F32)


def _in_proj(xs, mod4, gn0, w_in_p, gqk2, cosf, sinf, ones128):
    b, t, d = xs.shape
    nb = t // TM
    return pl.pallas_call(
        _in_kernel,
        grid=(b, nb),
        in_specs=[pl.BlockSpec((1, TM, d), lambda bi, i: (bi, i, 0)),
                  pl.BlockSpec((1, 1, 1, d), lambda bi, i: (_mod_row(bi, i), 0, 0, 0)),
                  pl.BlockSpec((1, 1, 1, d), lambda bi, i: (_mod_row(bi, i), 1, 0, 0)),
                  pl.BlockSpec((1, d), lambda bi, i: (0, 0)),
                  pl.BlockSpec((d, IN_COLS_PAD), lambda bi, i: (0, 0)),
                  pl.BlockSpec((2, LANES), lambda bi, i: (0, 0)),
                  pl.BlockSpec((TM, LANES), lambda bi, i: (i, 0)),
                  pl.BlockSpec((TM, LANES), lambda bi, i: (i, 0)),
                  pl.BlockSpec((LANES, LANES), lambda bi, i: (0, 0))],
        out_specs=[pl.BlockSpec((1, ATT_KV_HEADS, 1, LANES, ATT_GROUP * TM), lambda bi, i: (bi, 0, i, 0, 0)),
                   pl.BlockSpec((1, ATT_KV_HEADS, TM, LANES), lambda bi, i: (bi, 0, i, 0)),
                   pl.BlockSpec((1, ATT_KV_HEADS, 1, ATT_VROWS, TM), lambda bi, i: (bi, 0, i, 0, 0)),
                   pl.BlockSpec((1, TM, REST_COLS), lambda bi, i: (bi, i, 0))],
        out_shape=[jax.ShapeDtypeStruct((b, ATT_KV_HEADS, nb, LANES, ATT_GROUP * TM), BF16),
                   jax.ShapeDtypeStruct((b, ATT_KV_HEADS, t, LANES), BF16),
                   jax.ShapeDtypeStruct((b, ATT_KV_HEADS, nb, ATT_VROWS, TM), BF16),
                   jax.ShapeDtypeStruct((b, t, REST_COLS), F32)],
        compiler_params=_cparams("arbitrary", "arbitrary"),
        name="in_proj",
    )(xs, mod4, mod4, gn0, w_in_p, gqk2, cosf, sinf, ones128)


def _att_kernel(qt_ref, k_ref, vt_ref, o_ref, m_ref, acc_ref, s_ref, cmax_ref, *, ctx, n_lat, q_off):
    i = pl.program_id(2) + q_off
    qt = qt_ref[0, 0, 0]
    nsub = ATT_TK // TM

    def scores(start, size):
        return jnp.dot(k_ref[0, 0, pl.ds(start, size), :], qt, preferred_element_type=F32)

    def pv(pt, blk0, nblk):
        out = None
        for j in range(nblk):
            t = jnp.dot(vt_ref[0, 0, blk0 + j], pt[j * TM:(j + 1) * TM, :], preferred_element_type=F32)
            out = t if out is None else out + t
        return out

    st = scores(0, ctx)
    m0 = jnp.max(st, axis=0, keepdims=True)
    m_ref[...] = m0
    acc_ref[...] = pv(jnp.exp2(st - m0).astype(BF16), 0, ctx // TM)

    n_chunks = n_lat // ATT_TK

    def step(j, cur, nxt):
        jn = jnp.minimum(j + 1, n_chunks - 1)
        sn = scores(pl.multiple_of(ctx + jn * ATT_TK, TM), ATT_TK)
        s_ref[nxt] = sn
        cmax_ref[nxt] = jnp.max(sn, axis=0, keepdims=True)
        st = s_ref[cur]
        m_prev = m_ref[...]
        m_new = jnp.maximum(m_prev, cmax_ref[cur])
        alpha = jnp.exp2(m_prev - m_new)
        pt = jnp.exp2(st - m_new).astype(BF16)
        acc_ref[...] = alpha * acc_ref[...] + pv(pt, ctx // TM + j * nsub, nsub)
        m_ref[...] = m_new

    def body(jj, carry):
        step(2 * jj, 0, 1)
        step(2 * jj + 1, 1, 0)
        return carry

    @pl.when(i > 0)
    def _():
        s0 = scores(ctx, ATT_TK)
        s_ref[0] = s0
        cmax_ref[0] = jnp.max(s0, axis=0, keepdims=True)
        lax.fori_loop(0, n_chunks // 2, body, 0)

    acc = acc_ref[...]
    out_t = acc[0:HEAD_DIM] / acc[HEAD_DIM:HEAD_DIM + 1]
    for h in range(ATT_GROUP):
        o_ref[0, :, HEAD_DIM * h:HEAD_DIM * (h + 1)] = out_t[:, TM * h:TM * (h + 1)].T


def _att_fixed_kernel(qt_ref, k_ref, vt_ref, o_ref, acc_ref, p_ref, qa_ref, *, ctx, n_lat, q_off):
    i = pl.program_id(2) + q_off
    nsub = ATT_TK // TM
    n_chunks = n_lat // ATT_TK
    qa_ref[...] = qt_ref[0, 0, 0]

    def scores(start, size):
        return jnp.dot(k_ref[0, 0, pl.ds(start, size), :], qa_ref[...], preferred_element_type=F32)

    def pv(pt, blk0, nblk):
        out = None
        for j in range(nblk):
            t = jnp.dot(vt_ref[0, 0, blk0 + j], pt[j * TM:(j + 1) * TM, :], preferred_element_type=F32)
            out = t if out is None else out + t
        return out

    def context_chunk(first_latent):
        s0 = scores(0, ctx)
        tile_row = lax.broadcasted_iota(jnp.int32, (2 * SUBLANES, ATT_GROUP * TM), 0)
        shift_tile = jnp.where(tile_row == 0, -jnp.max(s0, axis=0, keepdims=True), 0.0).astype(BF16)
        qa_ref[HEAD_DIM:HEAD_DIM + 2 * SUBLANES, :] = shift_tile
        if first_latent:
            p_ref[0] = jnp.exp2(scores(ctx, ATT_TK)).astype(BF16)
        r = -shift_tile.astype(F32)[0:1, :]
        acc_ref[...] = pv(jnp.exp2(s0 - r).astype(BF16), 0, ctx // TM)

    def step(j, cur, nxt):
        jn = jnp.minimum(j + 1, n_chunks - 1)
        p_ref[nxt] = jnp.exp2(scores(pl.multiple_of(ctx + jn * ATT_TK, TM), ATT_TK)).astype(BF16)
        acc_ref[...] += pv(p_ref[cur], ctx // TM + j * nsub, nsub)

    def body(jj, carry):
        for u in range(ATT_STEPS_PER_BODY):
            step(ATT_STEPS_PER_BODY * jj + u, u % 2, (u + 1) % 2)
        return carry

    @pl.when(i == 0)
    def _():
        context_chunk(False)

    @pl.when(i > 0)
    def _():
        context_chunk(True)
        lax.fori_loop(0, n_chunks // ATT_STEPS_PER_BODY, body, 0)

    acc = acc_ref[...]
    out_t = acc[0:HEAD_DIM] / acc[HEAD_DIM:HEAD_DIM + 1]
    for h in range(ATT_GROUP):
        o_ref[0, :, HEAD_DIM * h:HEAD_DIM * (h + 1)] = out_t[:, TM * h:TM * (h + 1)].T


def _attention(qt, k, vt, score_bound, ctx, q_off):
    b, _, t, _ = k.shape
    nb = t // TM
    nq = nb - q_off
    gw = ATT_GROUP * HEAD_DIM
    cols = ATT_GROUP * TM

    def call(body, scratch):
        return pl.pallas_call(
            functools.partial(body, ctx=ctx, n_lat=t - ctx, q_off=q_off),
            grid=(b, ATT_KV_HEADS, nq),
            in_specs=[pl.BlockSpec((1, 1, 1, LANES, cols), lambda bi, g, i: (bi, g, i + q_off, 0, 0)),
                      pl.BlockSpec((1, 1, t, LANES), lambda bi, g, i: (bi, g, 0, 0)),
                      pl.BlockSpec((1, 1, nb, ATT_VROWS, TM), lambda bi, g, i: (bi, g, 0, 0, 0))],
            out_specs=pl.BlockSpec((1, TM, gw), lambda bi, g, i: (bi, i, g)),
            out_shape=jax.ShapeDtypeStruct((b, nq * TM, ATT_WIDTH), F32),
            scratch_shapes=scratch,
            compiler_params=_cparams("arbitrary", "arbitrary", "arbitrary"),
            name="attention",
        )

    online = call(_att_kernel, [pltpu.VMEM((1, cols), F32), pltpu.VMEM((ATT_VROWS, cols), F32),
                                pltpu.VMEM((2, ATT_TK, cols), F32), pltpu.VMEM((2, 1, cols), F32)])
    fixed = call(_att_fixed_kernel, [pltpu.VMEM((ATT_VROWS, cols), F32), pltpu.VMEM((2, ATT_TK, cols), BF16),
                                     pltpu.VMEM((LANES, cols), BF16)])
    return lax.cond(score_bound <= ATT_FIXED_SHIFT_LIMIT, fixed, online, qt, k, vt)


def _split3(x):
    hi = x.astype(BF16)
    r = x - hi.astype(F32)
    mid = r.astype(BF16)
    lo = (r - mid.astype(F32)).astype(BF16)
    return hi, mid, lo


def _dot_parts(parts, mat, left=False):
    out = None
    for p in parts:
        t = (jnp.dot(mat, p, preferred_element_type=F32) if left else jnp.dot(p, mat, preferred_element_type=F32))
        out = t if out is None else out + t
    return out


def _prep_kernel(cur_ref, prv_ref, nxt_ref, cw_ref, cb_ref, ab_ref, dnv_ref, ones_ref, tri_ref, xg_ref, xb_ref,
                 xr_ref, qk_ref, pk_ref, ge_ref, rows_ref):
    i = pl.program_id(1)
    nb = pl.num_programs(1)
    c_, w_ = DN_CHUNK, DN_WIDTH
    has_prev = i >= 2
    has_next = jnp.logical_and(i >= 1, i < nb - 1)
    cur = cur_ref[0]
    prv = jnp.where(has_prev, prv_ref[0], 0.0)
    nxt = jnp.where(has_next, nxt_ref[0], 0.0)
    trow = lax.broadcasted_iota(jnp.int32, cur.shape, 0)
    x_m1 = jnp.where(trow == 0, prv[SUBLANES - 1:SUBLANES, :], pltpu.roll(cur, 1, axis=0))
    x_p1 = jnp.where(trow == TM - 1, nxt[0:1, :], pltpu.roll(cur, TM - 1, axis=0))
    x_p2 = jnp.where(trow == TM - 2, nxt[0:1, :],
                     jnp.where(trow == TM - 1, nxt[1:2, :], pltpu.roll(cur, TM - 2, axis=0)))
    conv = (cb_ref[...] + x_m1 * cw_ref[0:1, :] + cur * cw_ref[1:2, :]
            + x_p1 * cw_ref[2:3, :] + x_p2 * cw_ref[3:4, :])
    xr_ref[0] = conv[:, 3 * w_:]
    act = _silu(conv[:, 0:3 * w_])
    qq = act[:, 0:w_]
    kk = act[:, w_:2 * w_]
    vv = act[:, 2 * w_:3 * w_]
    ones_bd = ones_ref[...]
    qn = qq * lax.rsqrt(_seg_sumsq(qq, ones_bd) + EPS) * (c_ ** -0.5)
    kn = kk * lax.rsqrt(_seg_sumsq(kk, ones_bd) + EPS)
    qk_ref[0, :, 0:w_] = qn.astype(BF16)
    qk_ref[0, :, w_:2 * w_] = kn.astype(BF16)

    ab = ab_ref[0]
    lane = lax.broadcasted_iota(jnp.int32, ab.shape, 1)
    g = jnp.where(lane < N_DIR * DN_HEADS, -jnp.exp(dnv_ref[0:1, :]) * _softplus(ab + dnv_ref[1:2, :]), 0.0)
    beta = jax.nn.sigmoid(ab)
    g_parts = _split3(g)
    gc = jnp.where(lane < DN_HEADS, _dot_parts(g_parts, tri_ref[0], left=True),
                   _dot_parts(g_parts, tri_ref[1], left=True))
    tot = _dot_parts(g_parts, tri_ref[2], left=True)
    ge = _dot_parts(_split3(gc), xg_ref[...])
    te = _dot_parts(_split3(tot), xg_ref[...])
    be = _dot_parts(_split3(beta)[:2], xb_ref[...])
    ge_ref[0] = ge
    row = lax.broadcasted_iota(jnp.int32, (c_, w_), 0)
    diag = row == (lax.broadcasted_iota(jnp.int32, (c_, w_), 1) & (c_ - 1))
    for d in range(N_DIR):
        ge_d = ge[:, w_ * d:w_ * (d + 1)]
        te_d = te[:, w_ * d:w_ * (d + 1)]
        be_d = be[:, w_ * d:w_ * (d + 1)]
        eg = jnp.exp(ge_d)
        kb = kn * be_d
        pk_ref[0, d, :, 0:w_] = kb.astype(BF16)
        pk_ref[0, d, :, w_:2 * w_] = (vv * be_d).astype(BF16)
        pk_ref[0, d, :, 2 * w_:3 * w_] = (kb * eg).astype(BF16)
        pk_ref[0, d, :, 3 * w_:4 * w_] = (qn * eg).astype(BF16)
        pk_ref[0, d, :, 4 * w_:5 * w_] = (kn * jnp.exp(te_d - ge_d)).astype(BF16)
        for pr in range(TM // DN_BLOCK):
            for c in range(DN_BLOCK // c_):
                r0 = DN_BLOCK * pr + c_ * c
                rows_ref[0, d, pr, c:c + 1, :] = jnp.sum(jnp.where(diag, ge_d[r0:r0 + c_, :], 0.0), axis=0,
                                                         keepdims=True)
                rows_ref[0, d, pr, 2 + c:3 + c, :] = te_d[r0:r0 + 1, :]


def _prep(rest, cw, cb, dnv, ones256, tri, xg, xb):
    b, t, _ = rest.shape
    nb = t // TM
    cc = 4 * DN_WIDTH
    r8 = TM // SUBLANES
    ppb = TM // DN_BLOCK

    def const(shape):
        return pl.BlockSpec(shape, lambda bi, i: (0,) * len(shape))

    return pl.pallas_call(
        _prep_kernel,
        grid=(b, nb),
        in_specs=[pl.BlockSpec((1, TM, cc), lambda bi, i: (bi, i, 0)),
                  pl.BlockSpec((1, SUBLANES, cc), lambda bi, i: (bi, jnp.maximum(i * r8 - 1, 0), 0)),
                  pl.BlockSpec((1, SUBLANES, cc), lambda bi, i: (bi, jnp.minimum((i + 1) * r8, nb * r8 - 1), 0)),
                  const((4, cc)), const((1, cc)),
                  pl.BlockSpec((1, TM, LANES), lambda bi, i: (bi, i, 12)),
                  const((2, LANES)), const((DN_WIDTH, DN_WIDTH)), const((3, TM, TM)),
                  const((LANES, N_DIR * DN_WIDTH)), const((LANES, N_DIR * DN_WIDTH))],
        out_specs=[pl.BlockSpec((1, TM, LRU_WIDTH), lambda bi, i: (bi, i, 0)),
                   pl.BlockSpec((1, TM, 2 * DN_WIDTH), lambda bi, i: (bi, i, 0)),
                   pl.BlockSpec((1, N_DIR, TM, DN_PACK), lambda bi, i: (bi, 0, i, 0)),
                   pl.BlockSpec((1, TM, N_DIR * DN_WIDTH), lambda bi, i: (bi, i, 0)),
                   pl.BlockSpec((1, N_DIR, ppb, 4, DN_WIDTH), lambda bi, i: (bi, 0, i, 0, 0))],
        out_shape=[jax.ShapeDtypeStruct((b, t, LRU_WIDTH), F32),
                   jax.ShapeDtypeStruct((b, t, 2 * DN_WIDTH), BF16),
                   jax.ShapeDtypeStruct((b, N_DIR, t, DN_PACK), BF16),
                   jax.ShapeDtypeStruct((b, t, N_DIR * DN_WIDTH), F32),
                   jax.ShapeDtypeStruct((b, N_DIR, t // DN_BLOCK, 4, DN_WIDTH), F32)],
        compiler_params=_cparams("arbitrary", "arbitrary"),
        name="conv_prep",
    )(rest, rest, rest, cw, cb, rest, dnv, ones256, tri, xg, xb)


def _lru_kernel(xr_ref, wg_ref, vec_ref, y_ref, a0, u0, a1, u1, *, ctx, n):
    nseg = SUBLANES
    vec = vec_ref[0]
    a_refs = (a0, a1)
    u_refs = (u0, u1)

    def gates(x, d):
        g = jnp.dot(x.astype(BF16), wg_ref[0, :, 2 * LANES * d:2 * LANES * (d + 1)],
                    preferred_element_type=F32)
        r = jax.nn.sigmoid(g[:, :LANES] + vec[2 + 2 * d:3 + 2 * d])
        ig = jax.nn.sigmoid(g[:, LANES:] + vec[3 + 2 * d:4 + 2 * d])
        log_a = -LRU_C * r * _softplus(-vec[d:d + 1])
        a = jnp.exp(log_a)
        mult = jnp.sqrt(-jnp.tanh(log_a) * (a * a + 1.0))
        return a, mult * (ig * x)

    def run(row0, seg, base, init):
        pitch = seg + SUBLANES
        for j in range(nseg):
            x = xr_ref[0, row0 + j * seg:row0 + (j + 1) * seg, :]
            for d in range(N_DIR):
                a, u = gates(x, d)
                a_refs[d][base + j * pitch:base + j * pitch + seg, :] = a
                u_refs[d][base + j * pitch:base + j * pitch + seg, :] = u

        def step(t, carry):
            out = []
            for d in range(N_DIR):
                h, p = carry[2 * d], carry[2 * d + 1]
                r = base + (t if d == 0 else seg - 1 - t)
                idx = pl.ds(r, nseg, stride=pitch)
                a = a_refs[d][idx, :]
                h = a * h + u_refs[d][idx, :]
                p = a * p
                u_refs[d][idx, :] = h
                a_refs[d][idx, :] = p
                out += [h, p]
            return tuple(out)

        z = jnp.zeros((nseg, LANES), F32)
        o = jnp.ones((nseg, LANES), F32)
        fin = lax.fori_loop(0, seg, step, (z, o, z, o), unroll=8)
        carries, finals = [], []
        for d in range(N_DIR):
            hl, pl_ = fin[2 * d], fin[2 * d + 1]
            order = list(range(nseg)) if d == 0 else list(range(nseg - 1, -1, -1))
            c = init[d]
            cs = {}
            for j in order:
                cs[j] = c
                c = hl[j:j + 1] + pl_[j:j + 1] * c
            carries.append(cs)
            finals.append(c)
        for j in range(nseg):
            sl = slice(base + j * pitch, base + j * pitch + seg)
            y = (u0[sl, :] + a0[sl, :] * carries[0][j]) + (u1[sl, :] + a1[sl, :] * carries[1][j])
            y_ref[0, row0 + j * seg:row0 + (j + 1) * seg, :] = y
        return finals

    zero = jnp.zeros((1, LANES), F32)
    seg_c = ctx // nseg
    fin_c = run(0, seg_c, 0, [zero, zero])
    run(ctx, n // nseg, nseg * (seg_c + SUBLANES), fin_c)


def _lru(xr, wg, vec, ctx):
    b, t, _ = xr.shape
    n = t - ctx
    rows = SUBLANES * (ctx // SUBLANES + SUBLANES) + SUBLANES * (n // SUBLANES + SUBLANES)
    kern = functools.partial(_lru_kernel, ctx=ctx, n=n)
    return pl.pallas_call(
        kern,
        grid=(b, LRU_WIDTH // LANES),
        in_specs=[pl.BlockSpec((1, t, LANES), lambda bi, hf: (bi, 0, hf)),
                  pl.BlockSpec((1, LANES, 4 * LANES), lambda bi, hf: (hf, 0, 0)),
                  pl.BlockSpec((1, SUBLANES, LANES), lambda bi, hf: (hf, 0, 0))],
        out_specs=pl.BlockSpec((1, t, LANES), lambda bi, hf: (bi, 0, hf)),
        out_shape=jax.ShapeDtypeStruct((b, t, LRU_WIDTH), F32),
        scratch_shapes=[pltpu.VMEM((rows, LANES), F32)] * 4,
        compiler_params=_cparams("arbitrary", "arbitrary"),
        name="rglru",
    )(xr, wg, vec)


def _dn_kernel(qkf_ref, pkf_ref, gef_ref, rwf_ref, qkb_ref, pkb_ref, geb_ref, rwb_ref, o0_ref, o1_ref, s_ref):
    c_, w_ = DN_CHUNK, DN_WIDTH
    n_batch = o0_ref.shape[0]

    @pl.when(pl.program_id(0) == 0)
    def _():
        s_ref[...] = jnp.zeros_like(s_ref)

    row = lax.broadcasted_iota(jnp.int32, (c_, w_), 0)
    lane = lax.broadcasted_iota(jnp.int32, (c_, w_), 1)
    lj = lane & (c_ - 1)
    head_masks = [(lane >> 6) == h for h in range(DN_HEADS)]
    eye_f = jnp.where(row == lj, 1.0, 0.0)
    shift = DN_BASE.bit_length() - 1
    same_base = (row >> shift) == (lj >> shift)
    off_blk = ({}, {})
    size = DN_BASE
    while size < c_:
        shift = size.bit_length() - 1
        bi, bj = row >> shift, lj >> shift
        off_blk[0][size] = jnp.logical_and(bi == bj + 1, (bi & 1) == 1)
        off_blk[1][size] = jnp.logical_and(bj == bi + 1, (bj & 1) == 1)
        size *= 2
    bd_mask = ((lax.broadcasted_iota(jnp.int32, (w_, w_), 0) >> 6)
               == (lax.broadcasted_iota(jnp.int32, (w_, w_), 1) >> 6))
    nt = (((1,), (1,)), ((), ()))
    tn = (((0,), (0,)), ((), ()))

    def bd(x):
        xb = x.astype(BF16)
        zero = jnp.zeros_like(xb)
        return jnp.concatenate([jnp.where(hm, xb, zero) for hm in head_masks], axis=0)

    def bdot(a, b, dims=None):
        if dims is None:
            return jnp.dot(a.astype(BF16), b, preferred_element_type=F32)
        return lax.dot_general(a.astype(BF16), b, dims, preferred_element_type=F32)

    per_block = DN_BLOCK // c_
    in_refs = ((qkf_ref, pkf_ref, gef_ref, rwf_ref), (qkb_ref, pkb_ref, geb_ref, rwb_ref))
    o_refs = (o0_ref, o1_ref)
    probs = []
    for d in range(N_DIR):
        incl = (lj <= row) if d == 0 else (lj >= row)
        strict = (lj < row) if d == 0 else (lj > row)
        qk_ref, pk_ref, ge_ref, rw_ref = in_refs[d]
        for bi, step in [(bi, step) for bi in range(n_batch) for step in range(per_block)]:
            c = step if d == 0 else per_block - 1 - step
            rows = slice(c_ * c, c_ * (c + 1))
            decay = jnp.where(incl, jnp.exp(jnp.where(incl, ge_ref[bi, rows, :] - rw_ref[bi, 0, 0, c:c + 1, :], 0.0)),
                              0.0)
            probs.append(dict(d=d, bi=bi, step=step, rows=rows, incl=incl, strict=strict, decay=decay,
                              st=N_DIR * bi + d,
                              kbq=jnp.concatenate([pk_ref[bi, 0, rows, 0:w_], qk_ref[bi, rows, 0:w_]], axis=0),
                              k=qk_ref[bi, rows, w_:2 * w_], vb=pk_ref[bi, 0, rows, w_:2 * w_],
                              kbe=pk_ref[bi, 0, rows, 2 * w_:3 * w_], qd=pk_ref[bi, 0, rows, 3 * w_:4 * w_],
                              kt=pk_ref[bi, 0, rows, 4 * w_:5 * w_],
                              gt=jnp.exp(rw_ref[bi, 0, 0, per_block + c:per_block + c + 1, :])))
    for p in probs:
        a1 = bdot(p["kbq"], bd(p["k"]), nt)
        p["lower"] = jnp.where(p["strict"], a1[:c_] * p["decay"], 0.0)
        p["attn"] = jnp.where(p["incl"], a1[c_:] * p["decay"], 0.0)
        p["pw"] = jnp.where(same_base, p["lower"], 0.0)
        p["tm"] = eye_f - p["pw"]
    for p in probs:
        p["pw"] = bdot(p["pw"], bd(p["pw"]))
    for p in probs:
        t = bdot(jnp.concatenate([p["tm"], p["pw"]], axis=0), bd(p["pw"]))
        p["tm"] = p["tm"] + t[:c_]
        p["pw"] = t[c_:]
    for p in probs:
        p["tm"] = p["tm"] + bdot(p["tm"], bd(p["pw"]))
    size = DN_BASE
    while size < c_:
        for p in probs:
            p["tmp"] = bdot(jnp.where(off_blk[p["d"]][size], p["lower"], 0.0), bd(p["tm"]))
        for p in probs:
            p["tm"] = p["tm"] - bdot(p["tm"], bd(p["tmp"]))
        size *= 2
    for p in probs:
        sol = bdot(p["tm"], jnp.concatenate([bd(p["vb"]), bd(p["kbe"])], axis=1))
        p["u"] = sol[:, :w_]
        p["wqd"] = jnp.concatenate([sol[:, w_:].astype(BF16), p["qd"]], axis=0)
    for step in range(per_block):
        cur = [p for p in probs if p["step"] == step]
        for p in cur:
            p["s_old"] = s_ref[p["st"]]
            p["a2"] = bdot(p["wqd"], p["s_old"].astype(BF16))
        for p in cur:
            p["v_new"] = p["u"] - p["a2"][:c_]
        for p in cur:
            o_refs[p["d"]][p["bi"], p["rows"], :] = p["a2"][c_:] + bdot(p["attn"], bd(p["v_new"]))
        for p in cur:
            upd = bdot(p["kt"], p["v_new"].astype(BF16), tn)
            s_ref[p["st"]] = p["s_old"] * p["gt"] + jnp.where(bd_mask, upd, 0.0)


def _deltanet(qk, pk, ge, rws, ctx):
    b, t, _ = qk.shape
    np_ = t // DN_BLOCK
    npc = ctx // DN_BLOCK

    def bwd(p):
        return jnp.where(p < npc, npc - 1 - p, np_ - 1 - (p - npc))

    def specs(d, pos):
        return [pl.BlockSpec((b, DN_BLOCK, 2 * DN_WIDTH), lambda p: (0, pos(p), 0)),
                pl.BlockSpec((b, 1, DN_BLOCK, DN_PACK), lambda p: (0, d, pos(p), 0)),
                pl.BlockSpec((b, DN_BLOCK, DN_WIDTH), lambda p: (0, pos(p), d)),
                pl.BlockSpec((b, 1, 1, 4, DN_WIDTH), lambda p: (0, d, pos(p), 0, 0))]

    o_shape = jax.ShapeDtypeStruct((b, t, DN_WIDTH), F32)
    return pl.pallas_call(
        _dn_kernel,
        grid=(np_,),
        in_specs=specs(0, lambda p: p) + specs(1, bwd),
        out_specs=[pl.BlockSpec((b, DN_BLOCK, DN_WIDTH), lambda p: (0, p, 0)),
                   pl.BlockSpec((b, DN_BLOCK, DN_WIDTH), lambda p: (0, bwd(p), 0))],
        out_shape=[o_shape, o_shape],
        scratch_shapes=[pltpu.VMEM((b * N_DIR, DN_WIDTH, DN_WIDTH), F32)],
        compiler_params=_cparams("arbitrary"),
        name="deltanet",
    )(qk, pk, ge, rws, qk, pk, ge, rws)


def _out_kernel(x_ref, att_ref, y_ref, lg_ref, o0_ref, o1_ref, dz_ref, gg_ref, gdn_ref, ones_ref, w_ref,
                gt_ref, gn_ref, xo_ref):
    gg = gg_ref[...]
    a_n = _rms(att_ref[0], gg[:, 0:ATT_WIDTH]).astype(BF16)
    lg = lg_ref[0]
    gelu = lg * (0.5 * (1.0 + jnp.tanh(0.7978845608028654 * (lg + 0.044715 * (lg * lg * lg)))))
    l_n = _rms(gelu * y_ref[0], gg[:, ATT_WIDTH:]).astype(BF16)
    o = o0_ref[0] + o1_ref[0]
    ss = _seg_sumsq(o, ones_ref[...])
    dn = ((o * lax.rsqrt(ss * (1.0 / DN_CHUNK) + EPS) * gdn_ref[...]) * _silu(dz_ref[0])).astype(BF16)
    mix = (jnp.dot(a_n, w_ref[0:ATT_WIDTH, :], preferred_element_type=F32)
           + jnp.dot(l_n, w_ref[ATT_WIDTH:ATT_WIDTH + LRU_WIDTH, :], preferred_element_type=F32)
           + jnp.dot(dn, w_ref[ATT_WIDTH + LRU_WIDTH:, :], preferred_element_type=F32))
    xo_ref[0] = x_ref[0] + gt_ref[0, 0] * _rms(mix, gn_ref[...])


def _out_proj(xs, att, y, rest, o0, o1, gg, gdn4, ones256, w_out, mod4, gn1, off):
    b, t, d = xs.shape
    nb = t // TM - off
    att_off = off - (t - att.shape[1]) // TM

    def tok(width, col=0):
        return pl.BlockSpec((1, TM, width), lambda bi, i: (bi, i + off, col))

    def const(shape):
        return pl.BlockSpec(shape, lambda bi, i: (0,) * len(shape))

    return pl.pallas_call(
        _out_kernel,
        grid=(b, nb),
        in_specs=[tok(d),
                  pl.BlockSpec((1, TM, ATT_WIDTH), lambda bi, i: (bi, i + att_off, 0)),
                  tok(LRU_WIDTH), tok(LRU_WIDTH, 4), tok(DN_WIDTH), tok(DN_WIDTH), tok(DN_WIDTH, 5),
                  const((1, ATT_WIDTH + LRU_WIDTH)), const((1, DN_WIDTH)), const((DN_WIDTH, DN_WIDTH)),
                  const((d, d)),
                  pl.BlockSpec((1, 1, 1, d), lambda bi, i: (_mod_row(bi, i + off), 2, 0, 0)),
                  const((1, d))],
        out_specs=pl.BlockSpec((1, TM, d), lambda bi, i: (bi, i, 0)),
        out_shape=jax.ShapeDtypeStruct((b, nb * TM, d), F32),
        compiler_params=_cparams("arbitrary", "arbitrary"),
        name="out_proj",
    )(xs, att, y, rest, o0, o1, rest, gg, gdn4, ones256, w_out, mod4, gn1)


def _ffn_kernel(x_ref, sh_ref, sc_ref, gt_ref, gn2_ref, gn3_ref, wi_ref, wo_ref, o_ref):
    x = x_ref[0]
    h = (_rms(x, gn2_ref[...]) * (1.0 + sc_ref[0, 0]) + sh_ref[0, 0]).astype(BF16)
    acc = jnp.zeros(x.shape, F32)
    for j in range(FFN_HIDDEN // FFN_CHUNK):
        lo = j * FFN_CHUNK
        gate = jnp.dot(h, wi_ref[:, lo:lo + FFN_CHUNK], preferred_element_type=F32)
        up = jnp.dot(h, wi_ref[:, FFN_HIDDEN + lo:FFN_HIDDEN + lo + FFN_CHUNK], preferred_element_type=F32)
        act = (_silu(gate) * up).astype(BF16)
        acc = acc + jnp.dot(act, wo_ref[lo:lo + FFN_CHUNK, :], preferred_element_type=F32)
    o_ref[0] = x + gt_ref[0, 0] * _rms(acc, gn3_ref[...])


def _ffn(x1, mod4, gn2, gn3, w_ffn_in, w_ffn_out, off):
    b, rows, d = x1.shape
    nb = rows // TM

    def mod(k):
        return pl.BlockSpec((1, 1, 1, d), lambda bi, i: (_mod_row(bi, i + off), k, 0, 0))

    def const(shape):
        return pl.BlockSpec(shape, lambda bi, i: (0,) * len(shape))

    return pl.pallas_call(
        _ffn_kernel,
        grid=(b, nb),
        in_specs=[pl.BlockSpec((1, TM, d), lambda bi, i: (bi, i, 0)),
                  mod(3), mod(4), mod(5), const((1, d)), const((1, d)),
                  const((d, 2 * FFN_HIDDEN)), const((FFN_HIDDEN, d))],
        out_specs=pl.BlockSpec((1, TM, d), lambda bi, i: (bi, i, 0)),
        out_shape=jax.ShapeDtypeStruct((b, rows, d), F32),
        compiler_params=_cparams("arbitrary", "arbitrary"),
        name="ffn",
    )(x1, mod4, mod4, mod4, gn2, gn3, w_ffn_in, w_ffn_out)


def _block_diag_ones(n):
    i = jnp.arange(n) // HEAD_DIM
    return (i[:, None] == i[None, :]).astype(BF16)


def _chunk_triangles(n):
    i = jnp.arange(n)
    same = (i[:, None] // DN_CHUNK) == (i[None, :] // DN_CHUNK)
    lower = jnp.logical_and(same, i[None, :] <= i[:, None])
    upper = jnp.logical_and(same, i[None, :] >= i[:, None])
    return jnp.stack([lower, upper, same]).astype(BF16)


def _head_spread(first_row):
    r = jnp.arange(LANES)[:, None]
    c = jnp.arange(N_DIR * DN_WIDTH)[None, :]
    return (r == first_row + c // DN_CHUNK).astype(BF16)


def _rope_tables(ctx, n):
    t = jnp.arange(n)
    row = (t // GRID_W).astype(F32)
    col = (t % GRID_W).astype(F32)
    n_freq = HEAD_DIM // 4
    inv_freq = ROPE_THETA ** (-jnp.arange(n_freq, dtype=F32) / n_freq)
    ang = jnp.concatenate([row[:, None] * inv_freq, col[:, None] * inv_freq], axis=-1)
    cos = jnp.concatenate([jnp.ones((ctx, HEAD_DIM // 2), F32), jnp.cos(ang)], axis=0)
    sin = jnp.concatenate([jnp.zeros((ctx, HEAD_DIM // 2), F32), jnp.sin(ang)], axis=0)
    return jnp.tile(jnp.concatenate([cos, cos], axis=1), (1, 2)), jnp.tile(jnp.concatenate([-sin, sin], axis=1), (1, 2))


def _lru_gate_weights(w_r, w_i):
    def dense(w):
        z = jnp.zeros((LRU_BLOCK_DIM, LRU_BLOCK_DIM), F32)
        return [jnp.block([[w[2 * hf], z], [z, w[2 * hf + 1]]]) for hf in range(2)]
    halves = [[], []]
    for d in range(N_DIR):
        for w in (w_r[d], w_i[d]):
            for hf, m in enumerate(dense(w)):
                halves[hf].append(m)
    return jnp.stack([jnp.concatenate(h, axis=1) for h in halves]).astype(BF16)


def _lru_vectors(lam, b_r, b_i):
    rows = [lam[0], lam[1], b_r[0], b_i[0], b_r[1], b_i[1], jnp.zeros_like(lam[0]), jnp.zeros_like(lam[0])]
    v = jnp.stack(rows)
    return jnp.stack([v[:, :LANES], v[:, LANES:]])


def kernel(x, c, ctx, c_ctx, w_ada, b_ada, g_norm, w_in, g_qk, lru_conv_w, lru_conv_b, lru_w_r, lru_b_r, lru_w_i, lru_b_i, lru_lambda, dn_conv_w, dn_a_log, dn_dt_bias, g_dn_out, g_group, w_out, w_ffn_in, w_ffn_out):
    depth = w_ada.shape[0]
    bsz, n, d = x.shape
    m = ctx.shape[1]
    assert m == TM and d == D_MODEL and bsz == 2 and n % (ATT_STEPS_PER_BODY * ATT_TK) == 0

    cvec = jnp.concatenate([c, c_ctx[None, :], jnp.zeros((SUBLANES - bsz - 1, d), F32)], axis=0)
    mods = _modulation(cvec, w_ada, b_ada)
    cosf, sinf = _rope_tables(m, n)
    ones128 = _block_diag_ones(LANES)
    ones256 = _block_diag_ones(DN_WIDTH)
    tri = _chunk_triangles(TM)
    xg = _head_spread(0)
    xb = _head_spread(N_DIR * DN_HEADS)

    xs = jnp.concatenate([ctx, x], axis=1)
    for l in range(depth):
        last = l == depth - 1
        off = 1 if last else 0
        mod4 = mods[l].reshape(SUBLANES, 6, 1, d)
        wl = w_in[l]
        w_in_p = jnp.concatenate([wl[:, 0:768], wl[:, 1280:2048], wl[:, 1024:1280], wl[:, 768:1024],
                                  wl[:, 2048:2320], jnp.zeros((d, IN_COLS_PAD - 2320), F32)], axis=1).astype(BF16)
        qt, k, vt, rest = _in_proj(xs, mod4, g_norm[l, 0][None, :], w_in_p, jnp.tile(g_qk[l], (1, 2)),
                                   cosf, sinf, ones128)
        score_bound = (1.01 * LOG2E * HEAD_DIM ** 0.5) * jnp.max(jnp.abs(g_qk[l, 0])) * jnp.max(jnp.abs(g_qk[l, 1]))
        att = _attention(qt, k, vt, score_bound, m, off)

        cw = jnp.concatenate([dn_conv_w[l], lru_conv_w[l]], axis=1)
        cb = jnp.concatenate([jnp.zeros((3 * DN_WIDTH,), F32), lru_conv_b[l]])[None, :]
        pad = jnp.zeros((LANES - N_DIR * DN_HEADS,), F32)
        dnv = jnp.stack([jnp.concatenate([dn_a_log[l].reshape(-1), pad]),
                         jnp.concatenate([dn_dt_bias[l].reshape(-1), pad])])
        xr, qk, pk, ge, rws = _prep(rest, cw, cb, dnv, ones256, tri, xg, xb)
        y = _lru(xr, _lru_gate_weights(lru_w_r[l], lru_w_i[l]),
                 _lru_vectors(lru_lambda[l], lru_b_r[l], lru_b_i[l]), m)
        o0, o1 = _deltanet(qk, pk, ge, rws, m)

        x1 = _out_proj(xs, att, y, rest, o0, o1, g_group[l][None, :], jnp.tile(g_dn_out[l], DN_HEADS)[None, :],
                       ones256, w_out[l].astype(BF16), mod4, g_norm[l, 1][None, :], off)
        xs = _ffn(x1, mod4, g_norm[l, 2][None, :], g_norm[l, 3][None, :],
                  w_ffn_in[l].astype(BF16), w_ffn_out[l].astype(BF16), off)
    return xs
```

```python
import functools

import jax
import jax.numpy as jnp
from jax import lax
from jax.experimental import pallas as pl
from jax.experimental.pallas import tpu as pltpu

F32 = jnp.float32
BF16 = jnp.bfloat16

D_MODEL = 1024
GRID_W = 64
EPS = 1e-6
ATT_GROUP = 4
ATT_KV_HEADS = 2
HEAD_DIM = 64
ATT_WIDTH = 512
ROPE_THETA = 10000.0
LRU_WIDTH = 256
LRU_BLOCK_DIM = 64
LRU_C = 8.0
DN_HEADS = 4
DN_WIDTH = 256
DN_CHUNK = 64
DN_BASE = 8
DN_BLOCK = 4 * DN_CHUNK
DN_ROWS = 2 * (DN_BLOCK // DN_CHUNK)
DN_PACK = 5 * DN_WIDTH
N_DIR = 2
FFN_HIDDEN = 2816
FFN_CHUNK = 2816

LANES = 128
SUBLANES = 8
TM = 256
ATT_TK = 1024
ATT_STEPS_PER_BODY = 4
ATT_VROWS = 80
LOG2E = 1.4426950408889634
ATT_FIXED_SHIFT_LIMIT = 48.0
QKV_COLS = 768
REST_COLS = 1664
IN_COLS_PAD = QKV_COLS + REST_COLS
VMEM_LIMIT = 56 * 1024 * 1024


def _cparams(*sem):
    return pltpu.CompilerParams(dimension_semantics=sem, vmem_limit_bytes=VMEM_LIMIT)


def _rms(x, gain):
    ms = jnp.mean(x * x, axis=-1, keepdims=True)
    return x * lax.rsqrt(ms + EPS) * gain


def _seg_sumsq(x, ones_bd):
    sq = x * x
    hi = sq.astype(BF16)
    lo = (sq - hi.astype(F32)).astype(BF16)
    return (jnp.dot(hi, ones_bd, preferred_element_type=F32)
            + jnp.dot(lo, ones_bd, preferred_element_type=F32))


def _silu(x):
    return x * jax.nn.sigmoid(x)


def _softplus(x):
    return jnp.maximum(x, 0.0) + jnp.log1p(jnp.exp(-jnp.abs(x)))


def _mod_row(b, i):
    return jnp.where(i == 0, 2, b)


def _mod_kernel(c_ref, w_ref, b_ref, o_ref):
    s = _silu(c_ref[...])
    o_ref[0] = jnp.dot(s.astype(BF16), w_ref[0].astype(BF16), preferred_element_type=F32) + b_ref[0]


def _modulation(cvec, w_ada, b_ada):
    depth = w_ada.shape[0]
    d = D_MODEL
    return pl.pallas_call(
        _mod_kernel,
        grid=(depth, 6),
        in_specs=[pl.BlockSpec((SUBLANES, d), lambda l, j: (0, 0)),
                  pl.BlockSpec((1, d, d), lambda l, j: (l, 0, j)),
                  pl.BlockSpec((1, 1, d), lambda l, j: (l, 0, j))],
        out_specs=pl.BlockSpec((1, SUBLANES, d), lambda l, j: (l, 0, j)),
        out_shape=jax.ShapeDtypeStruct((depth, SUBLANES, 6 * d), F32),
        compiler_params=_cparams("arbitrary", "arbitrary"),
        name="modulation",
    )(cvec, w_ada, b_ada.reshape(depth, 1, 6 * d))


def _in_kernel(x_ref, sh_ref, sc_ref, gn_ref, w_ref, gqk_ref, cos_ref, sin_ref, ones_ref,
               qt_ref, k_ref, vt_ref, rest_ref):
    x = x_ref[0]
    h = _rms(x, gn_ref[...]) * (1.0 + sc_ref[0, 0]) + sh_ref[0, 0]
    hb = h.astype(BF16)
    qkv = jnp.dot(hb, w_ref[:, 0:QKV_COLS], preferred_element_type=F32)
    cosf = cos_ref[...]
    sinf = sin_ref[...]
    lane = lax.broadcasted_iota(jnp.int32, (TM, LANES), 1)
    first_half = (lane & 32) == 0
    ones_bd = ones_ref[...]

    def norm_rope(slab, gain):
        ss = _seg_sumsq(slab, ones_bd)
        y = slab * lax.rsqrt(ss * (1.0 / HEAD_DIM) + EPS) * gain
        partner = jnp.where(first_half, pltpu.roll(y, 96, axis=1), pltpu.roll(y, 32, axis=1))
        return y * cosf + partner * sinf

    for s in range(ATT_WIDTH // LANES):
        qs = norm_rope(qkv[:, LANES * s:LANES * (s + 1)], gqk_ref[0:1, :]) * (LOG2E * HEAD_DIM ** -0.5)
        qst = qs.T.astype(BF16)
        for e in range(2):
            g, hh = divmod(2 * s + e, ATT_GROUP)
            qt_ref[0, g, 0, 0:HEAD_DIM, TM * hh:TM * (hh + 1)] = qst[HEAD_DIM * e:HEAD_DIM * (e + 1), :]
    for g in range(ATT_KV_HEADS):
        qt_ref[0, g, 0, HEAD_DIM:, :] = jnp.zeros((LANES - HEAD_DIM, ATT_GROUP * TM), BF16)
    kk = norm_rope(qkv[:, 512:640], gqk_ref[1:2, :])
    one_lane = jnp.where(lane == HEAD_DIM, 1.0, 0.0)
    vvt = qkv[:, 640:768].T.astype(BF16)
    row = lax.broadcasted_iota(jnp.int32, (ATT_VROWS - HEAD_DIM, TM), 0)
    ones_rows = jnp.where(row == 0, 1.0, 0.0).astype(BF16)
    for g in range(ATT_KV_HEADS):
        kg = kk if g == 0 else pltpu.roll(kk, HEAD_DIM, axis=1)
        k_ref[0, g] = jnp.where(lane < HEAD_DIM, kg, one_lane).astype(BF16)
        vt_ref[0, g, 0, 0:HEAD_DIM, :] = vvt[HEAD_DIM * g:HEAD_DIM * (g + 1), :]
        vt_ref[0, g, 0, HEAD_DIM:, :] = ones_rows
    rest_ref[0] = jnp.dot(hb, w_ref[:, QKV_COLS:], preferred_element_type=F32)


def _in_proj(xs, mod4, gn0, w_in_p, gqk2, cosf, sinf, ones128):
    b, t, d = xs.shape
    nb = t // TM
    return pl.pallas_call(
        _in_kernel,
        grid=(b, nb),
        in_specs=[pl.BlockSpec((1, TM, d), lambda bi, i: (bi, i, 0)),
                  pl.BlockSpec((1, 1, 1, d), lambda bi, i: (_mod_row(bi, i), 0, 0, 0)),
                  pl.BlockSpec((1, 1, 1, d), lambda bi, i: (_mod_row(bi, i), 1, 0, 0)),
                  pl.BlockSpec((1, d), lambda bi, i: (0, 0)),
                  pl.BlockSpec((d, IN_COLS_PAD), lambda bi, i: (0, 0)),
                  pl.BlockSpec((2, LANES), lambda bi, i: (0, 0)),
                  pl.BlockSpec((TM, LANES), lambda bi, i: (i, 0)),
                  pl.BlockSpec((TM, LANES), lambda bi, i: (i, 0)),
                  pl.BlockSpec((LANES, LANES), lambda bi, i: (0, 0))],
        out_specs=[pl.BlockSpec((1, ATT_KV_HEADS, 1, LANES, ATT_GROUP * TM), lambda bi, i: (bi, 0, i, 0, 0)),
                   pl.BlockSpec((1, ATT_KV_HEADS, TM, LANES), lambda bi, i: (bi, 0, i, 0)),
                   pl.BlockSpec((1, ATT_KV_HEADS, 1, ATT_VROWS, TM), lambda bi, i: (bi, 0, i, 0, 0)),
                   pl.BlockSpec((1, TM, REST_COLS), lambda bi, i: (bi, i, 0))],
        out_shape=[jax.ShapeDtypeStruct((b, ATT_KV_HEADS, nb, LANES, ATT_GROUP * TM), BF16),
                   jax.ShapeDtypeStruct((b, ATT_KV_HEADS, t, LANES), BF16),
                   jax.ShapeDtypeStruct((b, ATT_KV_HEADS, nb, ATT_VROWS, TM), BF16),
                   jax.ShapeDtypeStruct((b, t, REST_COLS), F32)],
        compiler_params=_cparams("arbitrary", "arbitrary"),
        name="in_proj",
    )(xs, mod4, mod4, gn0, w_in_p, gqk2, cosf, sinf, ones128)


def _att_kernel(qt_ref, k_ref, vt_ref, o_ref, m_ref, acc_ref, s_ref, cmax_ref, *, ctx, n_lat, q_off):
    i = pl.program_id(2) + q_off
    qt = qt_ref[0, 0, 0]
    nsub = ATT_TK // TM

    def scores(start, size):
        return jnp.dot(k_ref[0, 0, pl.ds(start, size), :], qt, preferred_element_type=F32)

    def pv(pt, blk0, nblk):
        out = None
        for j in range(nblk):
            t = jnp.dot(vt_ref[0, 0, blk0 + j], pt[j * TM:(j + 1) * TM, :], preferred_element_type=F32)
            out = t if out is None else out + t
        return out

    st = scores(0, ctx)
    m0 = jnp.max(st, axis=0, keepdims=True)
    m_ref[...] = m0
    acc_ref[...] = pv(jnp.exp2(st - m0).astype(BF16), 0, ctx // TM)

    n_chunks = n_lat // ATT_TK

    def step(j, cur, nxt):
        jn = jnp.minimum(j + 1, n_chunks - 1)
        sn = scores(pl.multiple_of(ctx + jn * ATT_TK, TM), ATT_TK)
        s_ref[nxt] = sn
        cmax_ref[nxt] = jnp.max(sn, axis=0, keepdims=True)
        st = s_ref[cur]
        m_prev = m_ref[...]
        m_new = jnp.maximum(m_prev, cmax_ref[cur])
        alpha = jnp.exp2(m_prev - m_new)
        pt = jnp.exp2(st - m_new).astype(BF16)
        acc_ref[...] = alpha * acc_ref[...] + pv(pt, ctx // TM + j * nsub, nsub)
        m_ref[...] = m_new

    def body(jj, carry):
        step(2 * jj, 0, 1)
        step(2 * jj + 1, 1, 0)
        return carry

    @pl.when(i > 0)
    def _():
        s0 = scores(ctx, ATT_TK)
        s_ref[0] = s0
        cmax_ref[0] = jnp.max(s0, axis=0, keepdims=True)
        lax.fori_loop(0, n_chunks // 2, body, 0)

    acc = acc_ref[...]
    out_t = acc[0:HEAD_DIM] / acc[HEAD_DIM:HEAD_DIM + 1]
    for h in range(ATT_GROUP):
        o_ref[0, :, HEAD_DIM * h:HEAD_DIM * (h + 1)] = out_t[:, TM * h:TM * (h + 1)].T


def _att_fixed_kernel(qt_ref, k_ref, vt_ref, o_ref, acc_ref, p_ref, qa_ref, *, ctx, n_lat, q_off):
    i = pl.program_id(2) + q_off
    nsub = ATT_TK // TM
    n_chunks = n_lat // ATT_TK
    qa_ref[...] = qt_ref[0, 0, 0]

    def scores(start, size):
        return jnp.dot(k_ref[0, 0, pl.ds(start, size), :], qa_ref[...], preferred_element_type=F32)

    def pv(pt, blk0, nblk):
        out = None
        for j in range(nblk):
            t = jnp.dot(vt_ref[0, 0, blk0 + j], pt[j * TM:(j + 1) * TM, :], preferred_element_type=F32)
            out = t if out is None else out + t
        return out

    def context_chunk(first_latent):
        s0 = scores(0, ctx)
        tile_row = lax.broadcasted_iota(jnp.int32, (2 * SUBLANES, ATT_GROUP * TM), 0)
        shift_tile = jnp.where(tile_row == 0, -jnp.max(s0, axis=0, keepdims=True), 0.0).astype(BF16)
        qa_ref[HEAD_DIM:HEAD_DIM + 2 * SUBLANES, :] = shift_tile
        if first_latent:
            p_ref[0] = jnp.exp2(scores(ctx, ATT_TK)).astype(BF16)
        r = -shift_tile.astype(F32)[0:1, :]
        acc_ref[...] = pv(jnp.exp2(s0 - r).astype(BF16), 0, ctx // TM)

    def step(j, cur, nxt):
        jn = jnp.minimum(j + 1, n_chunks - 1)
        p_ref[nxt] = jnp.exp2(scores(pl.multiple_of(ctx + jn * ATT_TK, TM), ATT_TK)).astype(BF16)
        acc_ref[...] += pv(p_ref[cur], ctx // TM + j * nsub, nsub)

    def body(jj, carry):
        for u in range(ATT_STEPS_PER_BODY):
            step(ATT_STEPS_PER_BODY * jj + u, u % 2, (u + 1) % 2)
        return carry

    @pl.when(i == 0)
    def _():
        context_chunk(False)

    @pl.when(i > 0)
    def _():
        context_chunk(True)
        lax.fori_loop(0, n_chunks // ATT_STEPS_PER_BODY, body, 0)

    acc = acc_ref[...]
    out_t = acc[0:HEAD_DIM] / acc[HEAD_DIM:HEAD_DIM + 1]
    for h in range(ATT_GROUP):
        o_ref[0, :, HEAD_DIM * h:HEAD_DIM * (h + 1)] = out_t[:, TM * h:TM * (h + 1)].T


def _attention(qt, k, vt, score_bound, ctx, q_off):
    b, _, t, _ = k.shape
    nb = t // TM
    nq = nb - q_off
    gw = ATT_GROUP * HEAD_DIM
    cols = ATT_GROUP * TM

    def call(body, scratch):
        return pl.pallas_call(
            functools.partial(body, ctx=ctx, n_lat=t - ctx, q_off=q_off),
            grid=(b, ATT_KV_HEADS, nq),
            in_specs=[pl.BlockSpec((1, 1, 1, LANES, cols), lambda bi, g, i: (bi, g, i + q_off, 0, 0)),
                      pl.BlockSpec((1, 1, t, LANES), lambda bi, g, i: (bi, g, 0, 0)),
                      pl.BlockSpec((1, 1, nb, ATT_VROWS, TM), lambda bi, g, i: (bi, g, 0, 0, 0))],
            out_specs=pl.BlockSpec((1, TM, gw), lambda bi, g, i: (bi, i, g)),
            out_shape=jax.ShapeDtypeStruct((b, nq * TM, ATT_WIDTH), F32),
            scratch_shapes=scratch,
            compiler_params=_cparams("arbitrary", "arbitrary", "arbitrary"),
            name="attention",
        )

    online = call(_att_kernel, [pltpu.VMEM((1, cols), F32), pltpu.VMEM((ATT_VROWS, cols), F32),
                                pltpu.VMEM((2, ATT_TK, cols), F32), pltpu.VMEM((2, 1, cols), F32)])
    fixed = call(_att_fixed_kernel, [pltpu.VMEM((ATT_VROWS, cols), F32), pltpu.VMEM((2, ATT_TK, cols), BF16),
                                     pltpu.VMEM((LANES, cols), BF16)])
    return lax.cond(score_bound <= ATT_FIXED_SHIFT_LIMIT, fixed, online, qt, k, vt)


def _split3(x):
    hi = x.astype(BF16)
    r = x - hi.astype(F32)
    mid = r.astype(BF16)
    lo = (r - mid.astype(F32)).astype(BF16)
    return hi, mid, lo


def _dot_parts(parts, mat, left=False):
    out = None
    for p in parts:
        t = (jnp.dot(mat, p, preferred_element_type=F32) if left else jnp.dot(p, mat, preferred_element_type=F32))
        out = t if out is None else out + t
    return out


def _prep_kernel(cur_ref, prv_ref, nxt_ref, cw_ref, cb_ref, ab_ref, dnv_ref, ones_ref, tri_ref, xg_ref, xb_ref,
                 xr_ref, qk_ref, pk_ref, ge_ref, rows_ref):
    i = pl.program_id(1)
    nb = pl.num_programs(1)
    c_, w_ = DN_CHUNK, DN_WIDTH
    has_prev = i >= 2
    has_next = jnp.logical_and(i >= 1, i < nb - 1)
    cur = cur_ref[0]
    prv = jnp.where(has_prev, prv_ref[0], 0.0)
    nxt = jnp.where(has_next, nxt_ref[0], 0.0)
    trow = lax.broadcasted_iota(jnp.int32, cur.shape, 0)
    x_m1 = jnp.where(trow == 0, prv[SUBLANES - 1:SUBLANES, :], pltpu.roll(cur, 1, axis=0))
    x_p1 = jnp.where(trow == TM - 1, nxt[0:1, :], pltpu.roll(cur, TM - 1, axis=0))
    x_p2 = jnp.where(trow == TM - 2, nxt[0:1, :],
                     jnp.where(trow == TM - 1, nxt[1:2, :], pltpu.roll(cur, TM - 2, axis=0)))
    conv = (cb_ref[...] + x_m1 * cw_ref[0:1, :] + cur * cw_ref[1:2, :]
            + x_p1 * cw_ref[2:3, :] + x_p2 * cw_ref[3:4, :])
    xr_ref[0] = conv[:, 3 * w_:]
    act = _silu(conv[:, 0:3 * w_])
    qq = act[:, 0:w_]
    kk = act[:, w_:2 * w_]
    vv = act[:, 2 * w_:3 * w_]
    ones_bd = ones_ref[...]
    qn = qq * lax.rsqrt(_seg_sumsq(qq, ones_bd) + EPS) * (c_ ** -0.5)
    kn = kk * lax.rsqrt(_seg_sumsq(kk, ones_bd) + EPS)
    qk_ref[0, :, 0:w_] = qn.astype(BF16)
    qk_ref[0, :, w_:2 * w_] = kn.astype(BF16)

    ab = ab_ref[0]
    lane = lax.broadcasted_iota(jnp.int32, ab.shape, 1)
    g = jnp.where(lane < N_DIR * DN_HEADS, -jnp.exp(dnv_ref[0:1, :]) * _softplus(ab + dnv_ref[1:2, :]), 0.0)
    beta = jax.nn.sigmoid(ab)
    g_parts = _split3(g)
    gc = jnp.where(lane < DN_HEADS, _dot_parts(g_parts, tri_ref[0], left=True),
                   _dot_parts(g_parts, tri_ref[1], left=True))
    tot = _dot_parts(g_parts, tri_ref[2], left=True)
    ge = _dot_parts(_split3(gc), xg_ref[...])
    te = _dot_parts(_split3(tot), xg_ref[...])
    be = _dot_parts(_split3(beta)[:2], xb_ref[...])
    ge_ref[0] = ge
    row = lax.broadcasted_iota(jnp.int32, (c_, w_), 0)
    diag = row == (lax.broadcasted_iota(jnp.int32, (c_, w_), 1) & (c_ - 1))
    for d in range(N_DIR):
        ge_d = ge[:, w_ * d:w_ * (d + 1)]
        te_d = te[:, w_ * d:w_ * (d + 1)]
        be_d = be[:, w_ * d:w_ * (d + 1)]
        eg = jnp.exp(ge_d)
        kb = kn * be_d
        pk_ref[0, d, :, 0:w_] = kb.astype(BF16)
        pk_ref[0, d, :, w_:2 * w_] = (vv * be_d).astype(BF16)
        pk_ref[0, d, :, 2 * w_:3 * w_] = (kb * eg).astype(BF16)
        pk_ref[0, d, :, 3 * w_:4 * w_] = (qn * eg).astype(BF16)
        pk_ref[0, d, :, 4 * w_:5 * w_] = (kn * jnp.exp(te_d - ge_d)).astype(BF16)
        for pr in range(TM // DN_BLOCK):
            for c in range(DN_BLOCK // c_):
                r0 = DN_BLOCK * pr + c_ * c
                rows_ref[0, d, pr, c:c + 1, :] = jnp.sum(jnp.where(diag, ge_d[r0:r0 + c_, :], 0.0), axis=0,
                                                         keepdims=True)
                rows_ref[0, d, pr, DN_ROWS // 2 + c:DN_ROWS // 2 + c + 1, :] = te_d[r0:r0 + 1, :]


def _prep(rest, cw, cb, dnv, ones256, tri, xg, xb):
    b, t, _ = rest.shape
    nb = t // TM
    cc = 4 * DN_WIDTH
    r8 = TM // SUBLANES
    ppb = TM // DN_BLOCK

    def const(shape):
        return pl.BlockSpec(shape, lambda bi, i: (0,) * len(shape))

    return pl.pallas_call(
        _prep_kernel,
        grid=(b, nb),
        in_specs=[pl.BlockSpec((1, TM, cc), lambda bi, i: (bi, i, 0)),
                  pl.BlockSpec((1, SUBLANES, cc), lambda bi, i: (bi, jnp.maximum(i * r8 - 1, 0), 0)),
                  pl.BlockSpec((1, SUBLANES, cc), lambda bi, i: (bi, jnp.minimum((i + 1) * r8, nb * r8 - 1), 0)),
                  const((4, cc)), const((1, cc)),
                  pl.BlockSpec((1, TM, LANES), lambda bi, i: (bi, i, 12)),
                  const((2, LANES)), const((DN_WIDTH, DN_WIDTH)), const((3, TM, TM)),
                  const((LANES, N_DIR * DN_WIDTH)), const((LANES, N_DIR * DN_WIDTH))],
        out_specs=[pl.BlockSpec((1, TM, LRU_WIDTH), lambda bi, i: (bi, i, 0)),
                   pl.BlockSpec((1, TM, 2 * DN_WIDTH), lambda bi, i: (bi, i, 0)),
                   pl.BlockSpec((1, N_DIR, TM, DN_PACK), lambda bi, i: (bi, 0, i, 0)),
                   pl.BlockSpec((1, TM, N_DIR * DN_WIDTH), lambda bi, i: (bi, i, 0)),
                   pl.BlockSpec((1, N_DIR, ppb, DN_ROWS, DN_WIDTH), lambda bi, i: (bi, 0, i, 0, 0))],
        out_shape=[jax.ShapeDtypeStruct((b, t, LRU_WIDTH), F32),
                   jax.ShapeDtypeStruct((b, t, 2 * DN_WIDTH), BF16),
                   jax.ShapeDtypeStruct((b, N_DIR, t, DN_PACK), BF16),
                   jax.ShapeDtypeStruct((b, t, N_DIR * DN_WIDTH), F32),
                   jax.ShapeDtypeStruct((b, N_DIR, t // DN_BLOCK, DN_ROWS, DN_WIDTH), F32)],
        compiler_params=_cparams("arbitrary", "arbitrary"),
        name="conv_prep",
    )(rest, rest, rest, cw, cb, rest, dnv, ones256, tri, xg, xb)


def _lru_kernel(xr_ref, wg_ref, vec_ref, y_ref, a0, u0, a1, u1, *, ctx, n):
    nseg = SUBLANES
    vec = vec_ref[0]
    a_refs = (a0, a1)
    u_refs = (u0, u1)

    def gates(x, d):
        g = jnp.dot(x.astype(BF16), wg_ref[0, :, 2 * LANES * d:2 * LANES * (d + 1)],
                    preferred_element_type=F32)
        r = jax.nn.sigmoid(g[:, :LANES] + vec[2 + 2 * d:3 + 2 * d])
        ig = jax.nn.sigmoid(g[:, LANES:] + vec[3 + 2 * d:4 + 2 * d])
        log_a = -LRU_C * r * _softplus(-vec[d:d + 1])
        a = jnp.exp(log_a)
        mult = jnp.sqrt(-jnp.tanh(log_a) * (a * a + 1.0))
        return a, mult * (ig * x)

    def run(row0, seg, base, init):
        pitch = seg + SUBLANES
        for j in range(nseg):
            x = xr_ref[0, row0 + j * seg:row0 + (j + 1) * seg, :]
            for d in range(N_DIR):
                a, u = gates(x, d)
                a_refs[d][base + j * pitch:base + j * pitch + seg, :] = a
                u_refs[d][base + j * pitch:base + j * pitch + seg, :] = u

        def step(t, carry):
            out = []
            for d in range(N_DIR):
                h, p = carry[2 * d], carry[2 * d + 1]
                r = base + (t if d == 0 else seg - 1 - t)
                idx = pl.ds(r, nseg, stride=pitch)
                a = a_refs[d][idx, :]
                h = a * h + u_refs[d][idx, :]
                p = a * p
                u_refs[d][idx, :] = h
                a_refs[d][idx, :] = p
                out += [h, p]
            return tuple(out)

        z = jnp.zeros((nseg, LANES), F32)
        o = jnp.ones((nseg, LANES), F32)
        fin = lax.fori_loop(0, seg, step, (z, o, z, o), unroll=8)
        carries, finals = [], []
        for d in range(N_DIR):
            hl, pl_ = fin[2 * d], fin[2 * d + 1]
            order = list(range(nseg)) if d == 0 else list(range(nseg - 1, -1, -1))
            c = init[d]
            cs = {}
            for j in order:
                cs[j] = c
                c = hl[j:j + 1] + pl_[j:j + 1] * c
            carries.append(cs)
            finals.append(c)
        for j in range(nseg):
            sl = slice(base + j * pitch, base + j * pitch + seg)
            y = (u0[sl, :] + a0[sl, :] * carries[0][j]) + (u1[sl, :] + a1[sl, :] * carries[1][j])
            y_ref[0, row0 + j * seg:row0 + (j + 1) * seg, :] = y
        return finals

    zero = jnp.zeros((1, LANES), F32)
    seg_c = ctx // nseg
    fin_c = run(0, seg_c, 0, [zero, zero])
    run(ctx, n // nseg, nseg * (seg_c + SUBLANES), fin_c)


def _lru(xr, wg, vec, ctx):
    b, t, _ = xr.shape
    n = t - ctx
    rows = SUBLANES * (ctx // SUBLANES + SUBLANES) + SUBLANES * (n // SUBLANES + SUBLANES)
    kern = functools.partial(_lru_kernel, ctx=ctx, n=n)
    return pl.pallas_call(
        kern,
        grid=(b, LRU_WIDTH // LANES),
        in_specs=[pl.BlockSpec((1, t, LANES), lambda bi, hf: (bi, 0, hf)),
                  pl.BlockSpec((1, LANES, 4 * LANES), lambda bi, hf: (hf, 0, 0)),
                  pl.BlockSpec((1, SUBLANES, LANES), lambda bi, hf: (hf, 0, 0))],
        out_specs=pl.BlockSpec((1, t, LANES), lambda bi, hf: (bi, 0, hf)),
        out_shape=jax.ShapeDtypeStruct((b, t, LRU_WIDTH), F32),
        scratch_shapes=[pltpu.VMEM((rows, LANES), F32)] * 4,
        compiler_params=_cparams("arbitrary", "arbitrary"),
        name="rglru",
    )(xr, wg, vec)


def _dn_kernel(qkf_ref, pkf_ref, gef_ref, rwf_ref, qkb_ref, pkb_ref, geb_ref, rwb_ref, o0_ref, o1_ref, s_ref):
    c_, w_ = DN_CHUNK, DN_WIDTH
    n_batch = o0_ref.shape[0]

    @pl.when(pl.program_id(0) == 0)
    def _():
        s_ref[...] = jnp.zeros_like(s_ref)

    row = lax.broadcasted_iota(jnp.int32, (c_, w_), 0)
    lane = lax.broadcasted_iota(jnp.int32, (c_, w_), 1)
    lj = lane & (c_ - 1)
    head_masks = [(lane >> 6) == h for h in range(DN_HEADS)]
    eye_f = jnp.where(row == lj, 1.0, 0.0)
    shift = DN_BASE.bit_length() - 1
    same_base = (row >> shift) == (lj >> shift)
    off_blk = ({}, {})
    size = DN_BASE
    while size < c_:
        shift = size.bit_length() - 1
        bi, bj = row >> shift, lj >> shift
        off_blk[0][size] = jnp.logical_and(bi == bj + 1, (bi & 1) == 1)
        off_blk[1][size] = jnp.logical_and(bj == bi + 1, (bj & 1) == 1)
        size *= 2
    bd_mask = ((lax.broadcasted_iota(jnp.int32, (w_, w_), 0) >> 6)
               == (lax.broadcasted_iota(jnp.int32, (w_, w_), 1) >> 6))
    nt = (((1,), (1,)), ((), ()))
    tn = (((0,), (0,)), ((), ()))

    def bd(x):
        xb = x.astype(BF16)
        zero = jnp.zeros_like(xb)
        return jnp.concatenate([jnp.where(hm, xb, zero) for hm in head_masks], axis=0)

    def bdot(a, b, dims=None):
        if dims is None:
            return jnp.dot(a.astype(BF16), b, preferred_element_type=F32)
        return lax.dot_general(a.astype(BF16), b, dims, preferred_element_type=F32)

    per_block = DN_BLOCK // c_
    in_refs = ((qkf_ref, pkf_ref, gef_ref, rwf_ref), (qkb_ref, pkb_ref, geb_ref, rwb_ref))
    o_refs = (o0_ref, o1_ref)
    probs = []
    for d in range(N_DIR):
        incl = (lj <= row) if d == 0 else (lj >= row)
        strict = (lj < row) if d == 0 else (lj > row)
        qk_ref, pk_ref, ge_ref, rw_ref = in_refs[d]
        for bi, step in [(bi, step) for bi in range(n_batch) for step in range(per_block)]:
            c = step if d == 0 else per_block - 1 - step
            rows = slice(c_ * c, c_ * (c + 1))
            decay = jnp.where(incl, jnp.exp(jnp.where(incl, ge_ref[bi, rows, :] - rw_ref[bi, 0, 0, c:c + 1, :], 0.0)),
                              0.0)
            probs.append(dict(d=d, bi=bi, step=step, rows=rows, incl=incl, strict=strict, decay=decay,
                              st=N_DIR * bi + d,
                              kbq=jnp.concatenate([pk_ref[bi, 0, rows, 0:w_], qk_ref[bi, rows, 0:w_]], axis=0),
                              k=qk_ref[bi, rows, w_:2 * w_], vb=pk_ref[bi, 0, rows, w_:2 * w_],
                              kbe=pk_ref[bi, 0, rows, 2 * w_:3 * w_], qd=pk_ref[bi, 0, rows, 3 * w_:4 * w_],
                              kt=pk_ref[bi, 0, rows, 4 * w_:5 * w_],
                              gt=jnp.exp(rw_ref[bi, 0, 0, per_block + c:per_block + c + 1, :])))
    for p in probs:
        a1 = bdot(p["kbq"], bd(p["k"]), nt)
        p["lower"] = jnp.where(p["strict"], a1[:c_] * p["decay"], 0.0)
        p["attn"] = jnp.where(p["incl"], a1[c_:] * p["decay"], 0.0)
        p["pw"] = jnp.where(same_base, p["lower"], 0.0)
        p["tm"] = eye_f - p["pw"]
    for p in probs:
        p["pw"] = bdot(p["pw"], bd(p["pw"]))
    for p in probs:
        t = bdot(jnp.concatenate([p["tm"], p["pw"]], axis=0), bd(p["pw"]))
        p["tm"] = p["tm"] + t[:c_]
        p["pw"] = t[c_:]
    for p in probs:
        p["tm"] = p["tm"] + bdot(p["tm"], bd(p["pw"]))
    size = DN_BASE
    while size < c_:
        for p in probs:
            p["tmp"] = bdot(jnp.where(off_blk[p["d"]][size], p["lower"], 0.0), bd(p["tm"]))
        for p in probs:
            p["tm"] = p["tm"] - bdot(p["tm"], bd(p["tmp"]))
        size *= 2
    for p in probs:
        sol = bdot(p["tm"], jnp.concatenate([bd(p["vb"]), bd(p["kbe"])], axis=1))
        p["u"] = sol[:, :w_]
        p["wqd"] = jnp.concatenate([sol[:, w_:].astype(BF16), p["qd"]], axis=0)
    for step in range(per_block):
        cur = [p for p in probs if p["step"] == step]
        for p in cur:
            p["s_old"] = s_ref[p["st"]]
            p["a2"] = bdot(p["wqd"], p["s_old"].astype(BF16))
        for p in cur:
            p["v_new"] = p["u"] - p["a2"][:c_]
        for p in cur:
            o_refs[p["d"]][p["bi"], p["rows"], :] = p["a2"][c_:] + bdot(p["attn"], bd(p["v_new"]))
        for p in cur:
            upd = bdot(p["kt"], p["v_new"].astype(BF16), tn)
            s_ref[p["st"]] = p["s_old"] * p["gt"] + jnp.where(bd_mask, upd, 0.0)


def _deltanet(qk, pk, ge, rws, ctx):
    b, t, _ = qk.shape
    np_ = t // DN_BLOCK
    npc = ctx // DN_BLOCK

    def bwd(p):
        return jnp.where(p < npc, npc - 1 - p, np_ - 1 - (p - npc))

    def specs(d, pos):
        return [pl.BlockSpec((b, DN_BLOCK, 2 * DN_WIDTH), lambda p: (0, pos(p), 0)),
                pl.BlockSpec((b, 1, DN_BLOCK, DN_PACK), lambda p: (0, d, pos(p), 0)),
                pl.BlockSpec((b, DN_BLOCK, DN_WIDTH), lambda p: (0, pos(p), d)),
                pl.BlockSpec((b, 1, 1, DN_ROWS, DN_WIDTH), lambda p: (0, d, pos(p), 0, 0))]

    o_shape = jax.ShapeDtypeStruct((b, t, DN_WIDTH), F32)
    return pl.pallas_call(
        _dn_kernel,
        grid=(np_,),
        in_specs=specs(0, lambda p: p) + specs(1, bwd),
        out_specs=[pl.BlockSpec((b, DN_BLOCK, DN_WIDTH), lambda p: (0, p, 0)),
                   pl.BlockSpec((b, DN_BLOCK, DN_WIDTH), lambda p: (0, bwd(p), 0))],
        out_shape=[o_shape, o_shape],
        scratch_shapes=[pltpu.VMEM((b * N_DIR, DN_WIDTH, DN_WIDTH), F32)],
        compiler_params=_cparams("arbitrary"),
        name="deltanet",
    )(qk, pk, ge, rws, qk, pk, ge, rws)


def _out_kernel(x_ref, att_ref, y_ref, lg_ref, o0_ref, o1_ref, dz_ref, gg_ref, gdn_ref, ones_ref, w_ref,
                gt_ref, gn_ref, xo_ref):
    gg = gg_ref[...]
    a_n = _rms(att_ref[0], gg[:, 0:ATT_WIDTH]).astype(BF16)
    lg = lg_ref[0]
    gelu = lg * (0.5 * (1.0 + jnp.tanh(0.7978845608028654 * (lg + 0.044715 * (lg * lg * lg)))))
    l_n = _rms(gelu * y_ref[0], gg[:, ATT_WIDTH:]).astype(BF16)
    o = o0_ref[0] + o1_ref[0]
    ss = _seg_sumsq(o, ones_ref[...])
    dn = ((o * lax.rsqrt(ss * (1.0 / DN_CHUNK) + EPS) * gdn_ref[...]) * _silu(dz_ref[0])).astype(BF16)
    mix = (jnp.dot(a_n, w_ref[0:ATT_WIDTH, :], preferred_element_type=F32)
           + jnp.dot(l_n, w_ref[ATT_WIDTH:ATT_WIDTH + LRU_WIDTH, :], preferred_element_type=F32)
           + jnp.dot(dn, w_ref[ATT_WIDTH + LRU_WIDTH:, :], preferred_element_type=F32))
    xo_ref[0] = x_ref[0] + gt_ref[0, 0] * _rms(mix, gn_ref[...])


def _out_proj(xs, att, y, rest, o0, o1, gg, gdn4, ones256, w_out, mod4, gn1, off):
    b, t, d = xs.shape
    nb = t // TM - off
    att_off = off - (t - att.shape[1]) // TM

    def tok(width, col=0):
        return pl.BlockSpec((1, TM, width), lambda bi, i: (bi, i + off, col))

    def const(shape):
        return pl.BlockSpec(shape, lambda bi, i: (0,) * len(shape))

    return pl.pallas_call(
        _out_kernel,
        grid=(b, nb),
        in_specs=[tok(d),
                  pl.BlockSpec((1, TM, ATT_WIDTH), lambda bi, i: (bi, i + att_off, 0)),
                  tok(LRU_WIDTH), tok(LRU_WIDTH, 4), tok(DN_WIDTH), tok(DN_WIDTH), tok(DN_WIDTH, 5),
                  const((1, ATT_WIDTH + LRU_WIDTH)), const((1, DN_WIDTH)), const((DN_WIDTH, DN_WIDTH)),
                  const((d, d)),
                  pl.BlockSpec((1, 1, 1, d), lambda bi, i: (_mod_row(bi, i + off), 2, 0, 0)),
                  const((1, d))],
        out_specs=pl.BlockSpec((1, TM, d), lambda bi, i: (bi, i, 0)),
        out_shape=jax.ShapeDtypeStruct((b, nb * TM, d), F32),
        compiler_params=_cparams("arbitrary", "arbitrary"),
        name="out_proj",
    )(xs, att, y, rest, o0, o1, rest, gg, gdn4, ones256, w_out, mod4, gn1)


def _ffn_kernel(x_ref, sh_ref, sc_ref, gt_ref, gn2_ref, gn3_ref, wi_ref, wo_ref, o_ref):
    x = x_ref[0]
    h = (_rms(x, gn2_ref[...]) * (1.0 + sc_ref[0, 0]) + sh_ref[0, 0]).astype(BF16)
    acc = jnp.zeros(x.shape, F32)
    for j in range(FFN_HIDDEN // FFN_CHUNK):
        lo = j * FFN_CHUNK
        gate = jnp.dot(h, wi_ref[:, lo:lo + FFN_CHUNK], preferred_element_type=F32)
        up = jnp.dot(h, wi_ref[:, FFN_HIDDEN + lo:FFN_HIDDEN + lo + FFN_CHUNK], preferred_element_type=F32)
        act = (_silu(gate) * up).astype(BF16)
        acc = acc + jnp.dot(act, wo_ref[lo:lo + FFN_CHUNK, :], preferred_element_type=F32)
    o_ref[0] = x + gt_ref[0, 0] * _rms(acc, gn3_ref[...])


def _ffn(x1, mod4, gn2, gn3, w_ffn_in, w_ffn_out, off):
    b, rows, d = x1.shape
    nb = rows // TM

    def mod(k):
        return pl.BlockSpec((1, 1, 1, d), lambda bi, i: (_mod_row(bi, i + off), k, 0, 0))

    def const(shape):
        return pl.BlockSpec(shape, lambda bi, i: (0,) * len(shape))

    return pl.pallas_call(
        _ffn_kernel,
        grid=(b, nb),
        in_specs=[pl.BlockSpec((1, TM, d), lambda bi, i: (bi, i, 0)),
                  mod(3), mod(4), mod(5), const((1, d)), const((1, d)),
                  const((d, 2 * FFN_HIDDEN)), const((FFN_HIDDEN, d))],
        out_specs=pl.BlockSpec((1, TM, d), lambda bi, i: (bi, i, 0)),
        out_shape=jax.ShapeDtypeStruct((b, rows, d), F32),
        compiler_params=_cparams("arbitrary", "arbitrary"),
        name="ffn",
    )(x1, mod4, mod4, mod4, gn2, gn3, w_ffn_in, w_ffn_out)


def _block_diag_ones(n):
    i = jnp.arange(n) // HEAD_DIM
    return (i[:, None] == i[None, :]).astype(BF16)


def _chunk_triangles(n):
    i = jnp.arange(n)
    same = (i[:, None] // DN_CHUNK) == (i[None, :] // DN_CHUNK)
    lower = jnp.logical_and(same, i[None, :] <= i[:, None])
    upper = jnp.logical_and(same, i[None, :] >= i[:, None])
    return jnp.stack([lower, upper, same]).astype(BF16)


def _head_spread(first_row):
    r = jnp.arange(LANES)[:, None]
    c = jnp.arange(N_DIR * DN_WIDTH)[None, :]
    return (r == first_row + c // DN_CHUNK).astype(BF16)


def _rope_tables(ctx, n):
    t = jnp.arange(n)
    row = (t // GRID_W).astype(F32)
    col = (t % GRID_W).astype(F32)
    n_freq = HEAD_DIM // 4
    inv_freq = ROPE_THETA ** (-jnp.arange(n_freq, dtype=F32) / n_freq)
    ang = jnp.concatenate([row[:, None] * inv_freq, col[:, None] * inv_freq], axis=-1)
    cos = jnp.concatenate([jnp.ones((ctx, HEAD_DIM // 2), F32), jnp.cos(ang)], axis=0)
    sin = jnp.concatenate([jnp.zeros((ctx, HEAD_DIM // 2), F32), jnp.sin(ang)], axis=0)
    return jnp.tile(jnp.concatenate([cos, cos], axis=1), (1, 2)), jnp.tile(jnp.concatenate([-sin, sin], axis=1), (1, 2))


def _lru_gate_weights(w_r, w_i):
    def dense(w):
        z = jnp.zeros((LRU_BLOCK_DIM, LRU_BLOCK_DIM), F32)
        return [jnp.block([[w[2 * hf], z], [z, w[2 * hf + 1]]]) for hf in range(2)]
    halves = [[], []]
    for d in range(N_DIR):
        for w in (w_r[d], w_i[d]):
            for hf, m in enumerate(dense(w)):
                halves[hf].append(m)
    return jnp.stack([jnp.concatenate(h, axis=1) for h in halves]).astype(BF16)


def _lru_vectors(lam, b_r, b_i):
    rows = [lam[0], lam[1], b_r[0], b_i[0], b_r[1], b_i[1], jnp.zeros_like(lam[0]), jnp.zeros_like(lam[0])]
    v = jnp.stack(rows)
    return jnp.stack([v[:, :LANES], v[:, LANES:]])


def kernel(x, c, ctx, c_ctx, w_ada, b_ada, g_norm, w_in, g_qk, lru_conv_w, lru_conv_b, lru_w_r, lru_b_r, lru_w_i, lru_b_i, lru_lambda, dn_conv_w, dn_a_log, dn_dt_bias, g_dn_out, g_group, w_out, w_ffn_in, w_ffn_out):
    depth = w_ada.shape[0]
    bsz, n, d = x.shape
    m = ctx.shape[1]
    assert m == TM and d == D_MODEL and bsz == 2 and n % (ATT_STEPS_PER_BODY * ATT_TK) == 0

    cvec = jnp.concatenate([c, c_ctx[None, :], jnp.zeros((SUBLANES - bsz - 1, d), F32)], axis=0)
    mods = _modulation(cvec, w_ada, b_ada)
    cosf, sinf = _rope_tables(m, n)
    ones128 = _block_diag_ones(LANES)
    ones256 = _block_diag_ones(DN_WIDTH)
    tri = _chunk_triangles(TM)
    xg = _head_spread(0)
    xb = _head_spread(N_DIR * DN_HEADS)

    xs = jnp.concatenate([ctx, x], axis=1)
    for l in range(depth):
        last = l == depth - 1
        off = 1 if last else 0
        mod4 = mods[l].reshape(SUBLANES, 6, 1, d)
        wl = w_in[l]
        w_in_p = jnp.concatenate([wl[:, 0:768], wl[:, 1280:2048], wl[:, 1024:1280], wl[:, 768:1024],
                                  wl[:, 2048:2320], jnp.zeros((d, IN_COLS_PAD - 2320), F32)], axis=1).astype(BF16)
        qt, k, vt, rest = _in_proj(xs, mod4, g_norm[l, 0][None, :], w_in_p, jnp.tile(g_qk[l], (1, 2)),
                                   cosf, sinf, ones128)
        score_bound = (1.01 * LOG2E * HEAD_DIM ** 0.5) * jnp.max(jnp.abs(g_qk[l, 0])) * jnp.max(jnp.abs(g_qk[l, 1]))
        att = _attention(qt, k, vt, score_bound, m, off)

        cw = jnp.concatenate([dn_conv_w[l], lru_conv_w[l]], axis=1)
        cb = jnp.concatenate([jnp.zeros((3 * DN_WIDTH,), F32), lru_conv_b[l]])[None, :]
        pad = jnp.zeros((LANES - N_DIR * DN_HEADS,), F32)
        dnv = jnp.stack([jnp.concatenate([dn_a_log[l].reshape(-1), pad]),
                         jnp.concatenate([dn_dt_bias[l].reshape(-1), pad])])
        xr, qk, pk, ge, rws = _prep(rest, cw, cb, dnv, ones256, tri, xg, xb)
        y = _lru(xr, _lru_gate_weights(lru_w_r[l], lru_w_i[l]),
                 _lru_vectors(lru_lambda[l], lru_b_r[l], lru_b_i[l]), m)
        o0, o1 = _deltanet(qk, pk, ge, rws, m)

        x1 = _out_proj(xs, att, y, rest, o0, o1, g_group[l][None, :], jnp.tile(g_dn_out[l], DN_HEADS)[None, :],
                       ones256, w_out[l].astype(BF16), mod4, g_norm[l, 1][None, :], off)
        xs = _ffn(x1, mod4, g_norm[l, 2][None, :], g_norm[l, 3][None, :],
                  w_ffn_in[l].astype(BF16), w_ffn_out[l].astype(BF16), off)
    return xs
```

```python
import functools

import jax
import jax.numpy as jnp
from jax import lax
from jax.experimental import pallas as pl
from jax.experimental.pallas import tpu as pltpu

F32 = jnp.float32
BF16 = jnp.bfloat16

D_MODEL = 1024
GRID_W = 64
EPS = 1e-6
ATT_GROUP = 4
ATT_KV_HEADS = 2
HEAD_DIM = 64
ATT_WIDTH = 512
ROPE_THETA = 10000.0
LRU_WIDTH = 256
LRU_BLOCK_DIM = 64
LRU_C = 8.0
DN_HEADS = 4
DN_WIDTH = 256
DN_CHUNK = 64
DN_BASE = 8
DN_BLOCK = 4 * DN_CHUNK
DN_ROWS = 2 * (DN_BLOCK // DN_CHUNK)
DN_PACK = 5 * DN_WIDTH
N_DIR = 2
FFN_HIDDEN = 2816
FFN_CHUNK = 2816

LANES = 128
SUBLANES = 8
TM = 256
ATT_TK = 512
ATT_STEPS_PER_BODY = 4
ATT_VROWS = 80
LOG2E = 1.4426950408889634
ATT_FIXED_SHIFT_LIMIT = 48.0
QKV_COLS = 768
REST_COLS = 1664
IN_COLS_PAD = QKV_COLS + REST_COLS
VMEM_LIMIT = 56 * 1024 * 1024


def _cparams(*sem):
    return pltpu.CompilerParams(dimension_semantics=sem, vmem_limit_bytes=VMEM_LIMIT)


def _rms(x, gain):
    ms = jnp.mean(x * x, axis=-1, keepdims=True)
    return x * lax.rsqrt(ms + EPS) * gain


def _seg_sumsq(x, ones_bd):
    sq = x * x
    hi = sq.astype(BF16)
    lo = (sq - hi.astype(F32)).astype(BF16)
    return (jnp.dot(hi, ones_bd, preferred_element_type=F32)
            + jnp.dot(lo, ones_bd, preferred_element_type=F32))


def _silu(x):
    return x * jax.nn.sigmoid(x)


def _softplus(x):
    return jnp.maximum(x, 0.0) + jnp.log1p(jnp.exp(-jnp.abs(x)))


def _mod_row(b, i):
    return jnp.where(i == 0, 2, b)


def _mod_kernel(c_ref, w_ref, b_ref, o_ref):
    s = _silu(c_ref[...])
    o_ref[0] = jnp.dot(s.astype(BF16), w_ref[0].astype(BF16), preferred_element_type=F32) + b_ref[0]


def _modulation(cvec, w_ada, b_ada):
    depth = w_ada.shape[0]
    d = D_MODEL
    return pl.pallas_call(
        _mod_kernel,
        grid=(depth, 6),
        in_specs=[pl.BlockSpec((SUBLANES, d), lambda l, j: (0, 0)),
                  pl.BlockSpec((1, d, d), lambda l, j: (l, 0, j)),
                  pl.BlockSpec((1, 1, d), lambda l, j: (l, 0, j))],
        out_specs=pl.BlockSpec((1, SUBLANES, d), lambda l, j: (l, 0, j)),
        out_shape=jax.ShapeDtypeStruct((depth, SUBLANES, 6 * d), F32),
        compiler_params=_cparams("arbitrary", "arbitrary"),
        name="modulation",
    )(cvec, w_ada, b_ada.reshape(depth, 1, 6 * d))


def _in_kernel(x_ref, sh_ref, sc_ref, gn_ref, w_ref, gqk_ref, cos_ref, sin_ref, ones_ref,
               qt_ref, k_ref, vt_ref, rest_ref):
    x = x_ref[0]
    h = _rms(x, gn_ref[...]) * (1.0 + sc_ref[0, 0]) + sh_ref[0, 0]
    hb = h.astype(BF16)
    qkv = jnp.dot(hb, w_ref[:, 0:QKV_COLS], preferred_element_type=F32)
    cosf = cos_ref[...]
    sinf = sin_ref[...]
    lane = lax.broadcasted_iota(jnp.int32, (TM, LANES), 1)
    first_half = (lane & 32) == 0
    ones_bd = ones_ref[...]

    def norm_rope(slab, gain):
        ss = _seg_sumsq(slab, ones_bd)
        y = slab * lax.rsqrt(ss * (1.0 / HEAD_DIM) + EPS) * gain
        partner = jnp.where(first_half, pltpu.roll(y, 96, axis=1), pltpu.roll(y, 32, axis=1))
        return y * cosf + partner * sinf

    for s in range(ATT_WIDTH // LANES):
        qs = norm_rope(qkv[:, LANES * s:LANES * (s + 1)], gqk_ref[0:1, :]) * (LOG2E * HEAD_DIM ** -0.5)
        qst = qs.T.astype(BF16)
        for e in range(2):
            g, hh = divmod(2 * s + e, ATT_GROUP)
            qt_ref[0, g, 0, 0:HEAD_DIM, TM * hh:TM * (hh + 1)] = qst[HEAD_DIM * e:HEAD_DIM * (e + 1), :]
    for g in range(ATT_KV_HEADS):
        qt_ref[0, g, 0, HEAD_DIM:, :] = jnp.zeros((LANES - HEAD_DIM, ATT_GROUP * TM), BF16)
    kk = norm_rope(qkv[:, 512:640], gqk_ref[1:2, :])
    one_lane = jnp.where(lane == HEAD_DIM, 1.0, 0.0)
    vvt = qkv[:, 640:768].T.astype(BF16)
    row = lax.broadcasted_iota(jnp.int32, (ATT_VROWS - HEAD_DIM, TM), 0)
    ones_rows = jnp.where(row == 0, 1.0, 0.0).astype(BF16)
    for g in range(ATT_KV_HEADS):
        kg = kk if g == 0 else pltpu.roll(kk, HEAD_DIM, axis=1)
        k_ref[0, g] = jnp.where(lane < HEAD_DIM, kg, one_lane).astype(BF16)
        vt_ref[0, g, 0, 0:HEAD_DIM, :] = vvt[HEAD_DIM * g:HEAD_DIM * (g + 1), :]
        vt_ref[0, g, 0, HEAD_DIM:, :] = ones_rows
    rest_ref[0] = jnp.dot(hb, w_ref[:, QKV_COLS:], preferred_element_type=F32)


def _in_proj(xs, mod4, gn0, w_in_p, gqk2, cosf, sinf, ones128):
    b, t, d = xs.shape
    nb = t // TM
    return pl.pallas_call(
        _in_kernel,
        grid=(b, nb),
        in_specs=[pl.BlockSpec((1, TM, d), lambda bi, i: (bi, i, 0)),
                  pl.BlockSpec((1, 1, 1, d), lambda bi, i: (_mod_row(bi, i), 0, 0, 0)),
                  pl.BlockSpec((1, 1, 1, d), lambda bi, i: (_mod_row(bi, i), 1, 0, 0)),
                  pl.BlockSpec((1, d), lambda bi, i: (0, 0)),
                  pl.BlockSpec((d, IN_COLS_PAD), lambda bi, i: (0, 0)),
                  pl.BlockSpec((2, LANES), lambda bi, i: (0, 0)),
                  pl.BlockSpec((TM, LANES), lambda bi, i: (i, 0)),
                  pl.BlockSpec((TM, LANES), lambda bi, i: (i, 0)),
                  pl.BlockSpec((LANES, LANES), lambda bi, i: (0, 0))],
        out_specs=[pl.BlockSpec((1, ATT_KV_HEADS, 1, LANES, ATT_GROUP * TM), lambda bi, i: (bi, 0, i, 0, 0)),
                   pl.BlockSpec((1, ATT_KV_HEADS, TM, LANES), lambda bi, i: (bi, 0, i, 0)),
                   pl.BlockSpec((1, ATT_KV_HEADS, 1, ATT_VROWS, TM), lambda bi, i: (bi, 0, i, 0, 0)),
                   pl.BlockSpec((1, TM, REST_COLS), lambda bi, i: (bi, i, 0))],
        out_shape=[jax.ShapeDtypeStruct((b, ATT_KV_HEADS, nb, LANES, ATT_GROUP * TM), BF16),
                   jax.ShapeDtypeStruct((b, ATT_KV_HEADS, t, LANES), BF16),
                   jax.ShapeDtypeStruct((b, ATT_KV_HEADS, nb, ATT_VROWS, TM), BF16),
                   jax.ShapeDtypeStruct((b, t, REST_COLS), F32)],
        compiler_params=_cparams("arbitrary", "arbitrary"),
        name="in_proj",
    )(xs, mod4, mod4, gn0, w_in_p, gqk2, cosf, sinf, ones128)


def _att_kernel(qt_ref, k_ref, vt_ref, kn_ref, o_ref, m_ref, acc_ref, s_ref, cmax_ref, *, ctx, n_lat, q_off):
    del kn_ref
    i = pl.program_id(2) + q_off
    qt = qt_ref[0, 0, 0]
    nsub = ATT_TK // TM

    def scores(start, size):
        return jnp.dot(k_ref[0, 0, pl.ds(start, size), :], qt, preferred_element_type=F32)

    def pv(pt, blk0, nblk):
        out = None
        for j in range(nblk):
            t = jnp.dot(vt_ref[0, 0, blk0 + j], pt[j * TM:(j + 1) * TM, :], preferred_element_type=F32)
            out = t if out is None else out + t
        return out

    st = scores(0, ctx)
    m0 = jnp.max(st, axis=0, keepdims=True)
    m_ref[...] = m0
    acc_ref[...] = pv(jnp.exp2(st - m0).astype(BF16), 0, ctx // TM)

    n_chunks = n_lat // ATT_TK

    def step(j, cur, nxt):
        jn = jnp.minimum(j + 1, n_chunks - 1)
        sn = scores(pl.multiple_of(ctx + jn * ATT_TK, TM), ATT_TK)
        s_ref[nxt] = sn
        cmax_ref[nxt] = jnp.max(sn, axis=0, keepdims=True)
        st = s_ref[cur]
        m_prev = m_ref[...]
        m_new = jnp.maximum(m_prev, cmax_ref[cur])
        alpha = jnp.exp2(m_prev - m_new)
        pt = jnp.exp2(st - m_new).astype(BF16)
        acc_ref[...] = alpha * acc_ref[...] + pv(pt, ctx // TM + j * nsub, nsub)
        m_ref[...] = m_new

    def body(jj, carry):
        step(2 * jj, 0, 1)
        step(2 * jj + 1, 1, 0)
        return carry

    @pl.when(i > 0)
    def _():
        s0 = scores(ctx, ATT_TK)
        s_ref[0] = s0
        cmax_ref[0] = jnp.max(s0, axis=0, keepdims=True)
        lax.fori_loop(0, n_chunks // 2, body, 0)

    acc = acc_ref[...]
    out_t = acc[0:HEAD_DIM] / acc[HEAD_DIM:HEAD_DIM + 1]
    for h in range(ATT_GROUP):
        o_ref[0, :, HEAD_DIM * h:HEAD_DIM * (h + 1)] = out_t[:, TM * h:TM * (h + 1)].T


def _att_fixed_kernel(qt_ref, k_ref, vt_ref, kn_ref, o_ref, acc_ref, p_ref, qa_ref, *, ctx, n_lat, q_off):
    i = pl.program_id(2) + q_off
    nsub = ATT_TK // TM
    n_chunks = n_lat // ATT_TK
    qt = qt_ref[0, 0, 0]
    q32 = qt[0:HEAD_DIM, :].astype(F32)
    r = jnp.sqrt(jnp.sum(q32 * q32, axis=0, keepdims=True)) * kn_ref[...]
    tile_row = lax.broadcasted_iota(jnp.int32, (2 * SUBLANES, ATT_GROUP * TM), 0)
    qa_ref[...] = qt
    qa_ref[HEAD_DIM:HEAD_DIM + 2 * SUBLANES, :] = jnp.where(tile_row == 0, -r, 0.0).astype(BF16)

    def probs(start, size):
        s = jnp.dot(k_ref[0, 0, pl.ds(start, size), :], qa_ref[...], preferred_element_type=F32)
        return jnp.exp2(s).astype(BF16)

    def pv(pt, blk0, nblk):
        out = None
        for j in range(nblk):
            t = jnp.dot(vt_ref[0, 0, blk0 + j], pt[j * TM:(j + 1) * TM, :], preferred_element_type=F32)
            out = t if out is None else out + t
        return out

    def step(j, cur, nxt, prefetch=True):
        if prefetch:
            p_ref[nxt] = probs(pl.multiple_of(ctx + (j + 1) * ATT_TK, TM), ATT_TK)
        acc_ref[...] += pv(p_ref[cur], ctx // TM + j * nsub, nsub)

    def body(jj, carry, last=False):
        for u in range(ATT_STEPS_PER_BODY):
            step(ATT_STEPS_PER_BODY * jj + u, (u + 1) % 2, u % 2,
                 prefetch=not (last and u == ATT_STEPS_PER_BODY - 1))
        return carry

    @pl.when(i == 0)
    def _():
        acc_ref[...] = pv(probs(0, ctx), 0, ctx // TM)

    @pl.when(i > 0)
    def _():
        p_ctx = probs(0, ctx)
        p_ref[1] = probs(ctx, ATT_TK)
        acc_ref[...] = pv(p_ctx, 0, ctx // TM)
        n_bodies = n_chunks // ATT_STEPS_PER_BODY
        lax.fori_loop(0, n_bodies - 1, body, 0)
        body(n_bodies - 1, 0, last=True)

    acc = acc_ref[...]
    out_t = acc[0:HEAD_DIM] / acc[HEAD_DIM:HEAD_DIM + 1]
    for h in range(ATT_GROUP):
        o_ref[0, :, HEAD_DIM * h:HEAD_DIM * (h + 1)] = out_t[:, TM * h:TM * (h + 1)].T


def _attention(qt, k, vt, key_norm_bound, score_bound, ctx, q_off):
    b, _, t, _ = k.shape
    nb = t // TM
    nq = nb - q_off
    gw = ATT_GROUP * HEAD_DIM
    cols = ATT_GROUP * TM

    def call(body, scratch):
        return pl.pallas_call(
            functools.partial(body, ctx=ctx, n_lat=t - ctx, q_off=q_off),
            grid=(b, ATT_KV_HEADS, nq),
            in_specs=[pl.BlockSpec((1, 1, 1, LANES, cols), lambda bi, g, i: (bi, g, i + q_off, 0, 0)),
                      pl.BlockSpec((1, 1, t, LANES), lambda bi, g, i: (bi, g, 0, 0)),
                      pl.BlockSpec((1, 1, nb, ATT_VROWS, TM), lambda bi, g, i: (bi, g, 0, 0, 0)),
                      pl.BlockSpec((1, 1), lambda bi, g, i: (0, 0))],
            out_specs=pl.BlockSpec((1, TM, gw), lambda bi, g, i: (bi, i, g)),
            out_shape=jax.ShapeDtypeStruct((b, nq * TM, ATT_WIDTH), F32),
            scratch_shapes=scratch,
            compiler_params=_cparams("arbitrary", "arbitrary", "arbitrary"),
            name="attention",
        )

    online = call(_att_kernel, [pltpu.VMEM((1, cols), F32), pltpu.VMEM((ATT_VROWS, cols), F32),
                                pltpu.VMEM((2, ATT_TK, cols), F32), pltpu.VMEM((2, 1, cols), F32)])
    fixed = call(_att_fixed_kernel, [pltpu.VMEM((ATT_VROWS, cols), F32), pltpu.VMEM((2, ATT_TK, cols), BF16),
                                     pltpu.VMEM((LANES, cols), BF16)])
    return lax.cond(score_bound <= ATT_FIXED_SHIFT_LIMIT, fixed, online, qt, k, vt,
                    jnp.reshape(key_norm_bound, (1, 1)))


def _split3(x):
    hi = x.astype(BF16)
    r = x - hi.astype(F32)
    mid = r.astype(BF16)
    lo = (r - mid.astype(F32)).astype(BF16)
    return hi, mid, lo


def _dot_parts(parts, mat, left=False):
    out = None
    for p in parts:
        t = (jnp.dot(mat, p, preferred_element_type=F32) if left else jnp.dot(p, mat, preferred_element_type=F32))
        out = t if out is None else out + t
    return out


def _prep_kernel(cur_ref, prv_ref, nxt_ref, cw_ref, cb_ref, ab_ref, dnv_ref, ones_ref, tri_ref, xg_ref, xb_ref,
                 xr_ref, qk_ref, pk_ref, ge_ref, rows_ref):
    i = pl.program_id(1)
    nb = pl.num_programs(1)
    c_, w_ = DN_CHUNK, DN_WIDTH
    has_prev = i >= 2
    has_next = jnp.logical_and(i >= 1, i < nb - 1)
    cur = cur_ref[0]
    prv = jnp.where(has_prev, prv_ref[0], 0.0)
    nxt = jnp.where(has_next, nxt_ref[0], 0.0)
    trow = lax.broadcasted_iota(jnp.int32, cur.shape, 0)
    x_m1 = jnp.where(trow == 0, prv[SUBLANES - 1:SUBLANES, :], pltpu.roll(cur, 1, axis=0))
    x_p1 = jnp.where(trow == TM - 1, nxt[0:1, :], pltpu.roll(cur, TM - 1, axis=0))
    x_p2 = jnp.where(trow == TM - 2, nxt[0:1, :],
                     jnp.where(trow == TM - 1, nxt[1:2, :], pltpu.roll(cur, TM - 2, axis=0)))
    conv = (cb_ref[...] + x_m1 * cw_ref[0:1, :] + cur * cw_ref[1:2, :]
            + x_p1 * cw_ref[2:3, :] + x_p2 * cw_ref[3:4, :])
    xr_ref[0] = conv[:, 3 * w_:]
    act = _silu(conv[:, 0:3 * w_])
    qq = act[:, 0:w_]
    kk = act[:, w_:2 * w_]
    vv = act[:, 2 * w_:3 * w_]
    ones_bd = ones_ref[...]
    qn = qq * lax.rsqrt(_seg_sumsq(qq, ones_bd) + EPS) * (c_ ** -0.5)
    kn = kk * lax.rsqrt(_seg_sumsq(kk, ones_bd) + EPS)
    qk_ref[0, :, 0:w_] = qn.astype(BF16)
    qk_ref[0, :, w_:2 * w_] = kn.astype(BF16)

    ab = ab_ref[0]
    lane = lax.broadcasted_iota(jnp.int32, ab.shape, 1)
    g = jnp.where(lane < N_DIR * DN_HEADS, -jnp.exp(dnv_ref[0:1, :]) * _softplus(ab + dnv_ref[1:2, :]), 0.0)
    beta = jax.nn.sigmoid(ab)
    g_parts = _split3(g)
    gc = jnp.where(lane < DN_HEADS, _dot_parts(g_parts, tri_ref[0], left=True),
                   _dot_parts(g_parts, tri_ref[1], left=True))
    tot = _dot_parts(g_parts, tri_ref[2], left=True)
    ge = _dot_parts(_split3(gc), xg_ref[...])
    te = _dot_parts(_split3(tot), xg_ref[...])
    be = _dot_parts(_split3(beta)[:2], xb_ref[...])
    ge_ref[0] = ge
    row = lax.broadcasted_iota(jnp.int32, (c_, w_), 0)
    diag = row == (lax.broadcasted_iota(jnp.int32, (c_, w_), 1) & (c_ - 1))
    for d in range(N_DIR):
        ge_d = ge[:, w_ * d:w_ * (d + 1)]
        te_d = te[:, w_ * d:w_ * (d + 1)]
        be_d = be[:, w_ * d:w_ * (d + 1)]
        eg = jnp.exp(ge_d)
        kb = kn * be_d
        pk_ref[0, d, :, 0:w_] = kb.astype(BF16)
        pk_ref[0, d, :, w_:2 * w_] = (vv * be_d).astype(BF16)
        pk_ref[0, d, :, 2 * w_:3 * w_] = (kb * eg).astype(BF16)
        pk_ref[0, d, :, 3 * w_:4 * w_] = (qn * eg).astype(BF16)
        pk_ref[0, d, :, 4 * w_:5 * w_] = (kn * jnp.exp(te_d - ge_d)).astype(BF16)
        for pr in range(TM // DN_BLOCK):
            for c in range(DN_BLOCK // c_):
                r0 = DN_BLOCK * pr + c_ * c
                rows_ref[0, d, pr, c:c + 1, :] = jnp.sum(jnp.where(diag, ge_d[r0:r0 + c_, :], 0.0), axis=0,
                                                         keepdims=True)
                rows_ref[0, d, pr, DN_ROWS // 2 + c:DN_ROWS // 2 + c + 1, :] = te_d[r0:r0 + 1, :]


def _prep(rest, cw, cb, dnv, ones256, tri, xg, xb):
    b, t, _ = rest.shape
    nb = t // TM
    cc = 4 * DN_WIDTH
    r8 = TM // SUBLANES
    ppb = TM // DN_BLOCK

    def const(shape):
        return pl.BlockSpec(shape, lambda bi, i: (0,) * len(shape))

    return pl.pallas_call(
        _prep_kernel,
        grid=(b, nb),
        in_specs=[pl.BlockSpec((1, TM, cc), lambda bi, i: (bi, i, 0)),
                  pl.BlockSpec((1, SUBLANES, cc), lambda bi, i: (bi, jnp.maximum(i * r8 - 1, 0), 0)),
                  pl.BlockSpec((1, SUBLANES, cc), lambda bi, i: (bi, jnp.minimum((i + 1) * r8, nb * r8 - 1), 0)),
                  const((4, cc)), const((1, cc)),
                  pl.BlockSpec((1, TM, LANES), lambda bi, i: (bi, i, 12)),
                  const((2, LANES)), const((DN_WIDTH, DN_WIDTH)), const((3, TM, TM)),
                  const((LANES, N_DIR * DN_WIDTH)), const((LANES, N_DIR * DN_WIDTH))],
        out_specs=[pl.BlockSpec((1, TM, LRU_WIDTH), lambda bi, i: (bi, i, 0)),
                   pl.BlockSpec((1, TM, 2 * DN_WIDTH), lambda bi, i: (bi, i, 0)),
                   pl.BlockSpec((1, N_DIR, TM, DN_PACK), lambda bi, i: (bi, 0, i, 0)),
                   pl.BlockSpec((1, TM, N_DIR * DN_WIDTH), lambda bi, i: (bi, i, 0)),
                   pl.BlockSpec((1, N_DIR, ppb, DN_ROWS, DN_WIDTH), lambda bi, i: (bi, 0, i, 0, 0))],
        out_shape=[jax.ShapeDtypeStruct((b, t, LRU_WIDTH), F32),
                   jax.ShapeDtypeStruct((b, t, 2 * DN_WIDTH), BF16),
                   jax.ShapeDtypeStruct((b, N_DIR, t, DN_PACK), BF16),
                   jax.ShapeDtypeStruct((b, t, N_DIR * DN_WIDTH), F32),
                   jax.ShapeDtypeStruct((b, N_DIR, t // DN_BLOCK, DN_ROWS, DN_WIDTH), F32)],
        compiler_params=_cparams("arbitrary", "arbitrary"),
        name="conv_prep",
    )(rest, rest, rest, cw, cb, rest, dnv, ones256, tri, xg, xb)


def _lru_kernel(xr_ref, wg_ref, vec_ref, y_ref, a0, u0, a1, u1, *, ctx, n):
    nseg = SUBLANES
    vec = vec_ref[0]
    a_refs = (a0, a1)
    u_refs = (u0, u1)

    def gates(x, d):
        g = jnp.dot(x.astype(BF16), wg_ref[0, :, 2 * LANES * d:2 * LANES * (d + 1)],
                    preferred_element_type=F32)
        r = jax.nn.sigmoid(g[:, :LANES] + vec[2 + 2 * d:3 + 2 * d])
        ig = jax.nn.sigmoid(g[:, LANES:] + vec[3 + 2 * d:4 + 2 * d])
        log_a = -LRU_C * r * _softplus(-vec[d:d + 1])
        a = jnp.exp(log_a)
        mult = jnp.sqrt(-jnp.tanh(log_a) * (a * a + 1.0))
        return a, mult * (ig * x)

    def run(row0, seg, base, init):
        pitch = seg + SUBLANES
        for j in range(nseg):
            x = xr_ref[0, row0 + j * seg:row0 + (j + 1) * seg, :]
            for d in range(N_DIR):
                a, u = gates(x, d)
                a_refs[d][base + j * pitch:base + j * pitch + seg, :] = a
                u_refs[d][base + j * pitch:base + j * pitch + seg, :] = u

        def step(t, carry):
            out = []
            for d in range(N_DIR):
                h, p = carry[2 * d], carry[2 * d + 1]
                r = base + (t if d == 0 else seg - 1 - t)
                idx = pl.ds(r, nseg, stride=pitch)
                a = a_refs[d][idx, :]
                h = a * h + u_refs[d][idx, :]
                p = a * p
                u_refs[d][idx, :] = h
                a_refs[d][idx, :] = p
                out += [h, p]
            return tuple(out)

        z = jnp.zeros((nseg, LANES), F32)
        o = jnp.ones((nseg, LANES), F32)
        fin = lax.fori_loop(0, seg, step, (z, o, z, o), unroll=8)
        carries, finals = [], []
        for d in range(N_DIR):
            hl, pl_ = fin[2 * d], fin[2 * d + 1]
            order = list(range(nseg)) if d == 0 else list(range(nseg - 1, -1, -1))
            c = init[d]
            cs = {}
            for j in order:
                cs[j] = c
                c = hl[j:j + 1] + pl_[j:j + 1] * c
            carries.append(cs)
            finals.append(c)
        for j in range(nseg):
            sl = slice(base + j * pitch, base + j * pitch + seg)
            y = (u0[sl, :] + a0[sl, :] * carries[0][j]) + (u1[sl, :] + a1[sl, :] * carries[1][j])
            y_ref[0, row0 + j * seg:row0 + (j + 1) * seg, :] = y
        return finals

    zero = jnp.zeros((1, LANES), F32)
    seg_c = ctx // nseg
    fin_c = run(0, seg_c, 0, [zero, zero])
    run(ctx, n // nseg, nseg * (seg_c + SUBLANES), fin_c)


def _lru(xr, wg, vec, ctx):
    b, t, _ = xr.shape
    n = t - ctx
    rows = SUBLANES * (ctx // SUBLANES + SUBLANES) + SUBLANES * (n // SUBLANES + SUBLANES)
    kern = functools.partial(_lru_kernel, ctx=ctx, n=n)
    return pl.pallas_call(
        kern,
        grid=(b, LRU_WIDTH // LANES),
        in_specs=[pl.BlockSpec((1, t, LANES), lambda bi, hf: (bi, 0, hf)),
                  pl.BlockSpec((1, LANES, 4 * LANES), lambda bi, hf: (hf, 0, 0)),
                  pl.BlockSpec((1, SUBLANES, LANES), lambda bi, hf: (hf, 0, 0))],
        out_specs=pl.BlockSpec((1, t, LANES), lambda bi, hf: (bi, 0, hf)),
        out_shape=jax.ShapeDtypeStruct((b, t, LRU_WIDTH), F32),
        scratch_shapes=[pltpu.VMEM((rows, LANES), F32)] * 4,
        compiler_params=_cparams("arbitrary", "arbitrary"),
        name="rglru",
    )(xr, wg, vec)


def _dn_kernel(qkf_ref, pkf_ref, gef_ref, rwf_ref, qkb_ref, pkb_ref, geb_ref, rwb_ref, o0_ref, o1_ref, s_ref):
    c_, w_ = DN_CHUNK, DN_WIDTH
    n_batch = o0_ref.shape[0]

    @pl.when(pl.program_id(0) == 0)
    def _():
        s_ref[...] = jnp.zeros_like(s_ref)

    row = lax.broadcasted_iota(jnp.int32, (c_, w_), 0)
    lane = lax.broadcasted_iota(jnp.int32, (c_, w_), 1)
    lj = lane & (c_ - 1)
    head_masks = [(lane >> 6) == h for h in range(DN_HEADS)]
    eye_f = jnp.where(row == lj, 1.0, 0.0)
    shift = DN_BASE.bit_length() - 1
    same_base = (row >> shift) == (lj >> shift)
    off_blk = ({}, {})
    size = DN_BASE
    while size < c_:
        shift = size.bit_length() - 1
        bi, bj = row >> shift, lj >> shift
        off_blk[0][size] = jnp.logical_and(bi == bj + 1, (bi & 1) == 1)
        off_blk[1][size] = jnp.logical_and(bj == bi + 1, (bj & 1) == 1)
        size *= 2
    bd_mask = ((lax.broadcasted_iota(jnp.int32, (w_, w_), 0) >> 6)
               == (lax.broadcasted_iota(jnp.int32, (w_, w_), 1) >> 6))
    nt = (((1,), (1,)), ((), ()))
    tn = (((0,), (0,)), ((), ()))

    def bd(x):
        xb = x.astype(BF16)
        zero = jnp.zeros_like(xb)
        return jnp.concatenate([jnp.where(hm, xb, zero) for hm in head_masks], axis=0)

    def bdot(a, b, dims=None):
        if dims is None:
            return jnp.dot(a.astype(BF16), b, preferred_element_type=F32)
        return lax.dot_general(a.astype(BF16), b, dims, preferred_element_type=F32)

    per_block = DN_BLOCK // c_
    in_refs = ((qkf_ref, pkf_ref, gef_ref, rwf_ref), (qkb_ref, pkb_ref, geb_ref, rwb_ref))
    o_refs = (o0_ref, o1_ref)
    probs = []
    for d in range(N_DIR):
        incl = (lj <= row) if d == 0 else (lj >= row)
        strict = (lj < row) if d == 0 else (lj > row)
        qk_ref, pk_ref, ge_ref, rw_ref = in_refs[d]
        for bi, step in [(bi, step) for bi in range(n_batch) for step in range(per_block)]:
            c = step if d == 0 else per_block - 1 - step
            rows = slice(c_ * c, c_ * (c + 1))
            decay = jnp.where(incl, jnp.exp(jnp.where(incl, ge_ref[bi, rows, :] - rw_ref[bi, 0, 0, c:c + 1, :], 0.0)),
                              0.0)
            probs.append(dict(d=d, bi=bi, step=step, rows=rows, incl=incl, strict=strict, decay=decay,
                              st=N_DIR * bi + d,
                              kbq=jnp.concatenate([pk_ref[bi, 0, rows, 0:w_], qk_ref[bi, rows, 0:w_]], axis=0),
                              k=qk_ref[bi, rows, w_:2 * w_], vb=pk_ref[bi, 0, rows, w_:2 * w_],
                              kbe=pk_ref[bi, 0, rows, 2 * w_:3 * w_], qd=pk_ref[bi, 0, rows, 3 * w_:4 * w_],
                              kt=pk_ref[bi, 0, rows, 4 * w_:5 * w_],
                              gt=jnp.exp(rw_ref[bi, 0, 0, per_block + c:per_block + c + 1, :])))
    for p in probs:
        a1 = bdot(p["kbq"], bd(p["k"]), nt)
        p["lower"] = jnp.where(p["strict"], a1[:c_] * p["decay"], 0.0)
        p["attn"] = jnp.where(p["incl"], a1[c_:] * p["decay"], 0.0)
        p["pw"] = jnp.where(same_base, p["lower"], 0.0)
        p["tm"] = eye_f - p["pw"]
    for p in probs:
        p["pw"] = bdot(p["pw"], bd(p["pw"]))
    for p in probs:
        t = bdot(jnp.concatenate([p["tm"], p["pw"]], axis=0), bd(p["pw"]))
        p["tm"] = p["tm"] + t[:c_]
        p["pw"] = t[c_:]
    for p in probs:
        p["tm"] = p["tm"] + bdot(p["tm"], bd(p["pw"]))
    size = DN_BASE
    while size < c_:
        for p in probs:
            p["tmp"] = bdot(jnp.where(off_blk[p["d"]][size], p["lower"], 0.0), bd(p["tm"]))
        for p in probs:
            p["tm"] = p["tm"] - bdot(p["tm"], bd(p["tmp"]))
        size *= 2
    for p in probs:
        sol = bdot(p["tm"], jnp.concatenate([bd(p["vb"]), bd(p["kbe"])], axis=1))
        p["u"] = sol[:, :w_]
        p["wqd"] = jnp.concatenate([sol[:, w_:].astype(BF16), p["qd"]], axis=0)
    for step in range(per_block):
        cur = [p for p in probs if p["step"] == step]
        for p in cur:
            p["s_old"] = s_ref[p["st"]]
            p["a2"] = bdot(p["wqd"], p["s_old"].astype(BF16))
        for p in cur:
            p["v_new"] = p["u"] - p["a2"][:c_]
        for p in cur:
            o_refs[p["d"]][p["bi"], p["rows"], :] = p["a2"][c_:] + bdot(p["attn"], bd(p["v_new"]))
        for p in cur:
            upd = bdot(p["kt"], p["v_new"].astype(BF16), tn)
            s_ref[p["st"]] = p["s_old"] * p["gt"] + jnp.where(bd_mask, upd, 0.0)


def _deltanet(qk, pk, ge, rws, ctx):
    b, t, _ = qk.shape
    np_ = t // DN_BLOCK
    npc = ctx // DN_BLOCK

    def bwd(p):
        return jnp.where(p < npc, npc - 1 - p, np_ - 1 - (p - npc))

    def specs(d, pos):
        return [pl.BlockSpec((b, DN_BLOCK, 2 * DN_WIDTH), lambda p: (0, pos(p), 0)),
                pl.BlockSpec((b, 1, DN_BLOCK, DN_PACK), lambda p: (0, d, pos(p), 0)),
                pl.BlockSpec((b, DN_BLOCK, DN_WIDTH), lambda p: (0, pos(p), d)),
                pl.BlockSpec((b, 1, 1, DN_ROWS, DN_WIDTH), lambda p: (0, d, pos(p), 0, 0))]

    o_shape = jax.ShapeDtypeStruct((b, t, DN_WIDTH), F32)
    return pl.pallas_call(
        _dn_kernel,
        grid=(np_,),
        in_specs=specs(0, lambda p: p) + specs(1, bwd),
        out_specs=[pl.BlockSpec((b, DN_BLOCK, DN_WIDTH), lambda p: (0, p, 0)),
                   pl.BlockSpec((b, DN_BLOCK, DN_WIDTH), lambda p: (0, bwd(p), 0))],
        out_shape=[o_shape, o_shape],
        scratch_shapes=[pltpu.VMEM((b * N_DIR, DN_WIDTH, DN_WIDTH), F32)],
        compiler_params=_cparams("arbitrary"),
        name="deltanet",
    )(qk, pk, ge, rws, qk, pk, ge, rws)


def _out_kernel(x_ref, att_ref, y_ref, lg_ref, o0_ref, o1_ref, dz_ref, gg_ref, gdn_ref, ones_ref, w_ref,
                gt_ref, gn_ref, xo_ref):
    gg = gg_ref[...]
    a_n = _rms(att_ref[0], gg[:, 0:ATT_WIDTH]).astype(BF16)
    lg = lg_ref[0]
    gelu = lg * (0.5 * (1.0 + jnp.tanh(0.7978845608028654 * (lg + 0.044715 * (lg * lg * lg)))))
    l_n = _rms(gelu * y_ref[0], gg[:, ATT_WIDTH:]).astype(BF16)
    o = o0_ref[0] + o1_ref[0]
    ss = _seg_sumsq(o, ones_ref[...])
    dn = ((o * lax.rsqrt(ss * (1.0 / DN_CHUNK) + EPS) * gdn_ref[...]) * _silu(dz_ref[0])).astype(BF16)
    mix = (jnp.dot(a_n, w_ref[0:ATT_WIDTH, :], preferred_element_type=F32)
           + jnp.dot(l_n, w_ref[ATT_WIDTH:ATT_WIDTH + LRU_WIDTH, :], preferred_element_type=F32)
           + jnp.dot(dn, w_ref[ATT_WIDTH + LRU_WIDTH:, :], preferred_element_type=F32))
    xo_ref[0] = x_ref[0] + gt_ref[0, 0] * _rms(mix, gn_ref[...])


def _out_proj(xs, att, y, rest, o0, o1, gg, gdn4, ones256, w_out, mod4, gn1, off):
    b, t, d = xs.shape
    nb = t // TM - off
    att_off = off - (t - att.shape[1]) // TM

    def tok(width, col=0):
        return pl.BlockSpec((1, TM, width), lambda bi, i: (bi, i + off, col))

    def const(shape):
        return pl.BlockSpec(shape, lambda bi, i: (0,) * len(shape))

    return pl.pallas_call(
        _out_kernel,
        grid=(b, nb),
        in_specs=[tok(d),
                  pl.BlockSpec((1, TM, ATT_WIDTH), lambda bi, i: (bi, i + att_off, 0)),
                  tok(LRU_WIDTH), tok(LRU_WIDTH, 4), tok(DN_WIDTH), tok(DN_WIDTH), tok(DN_WIDTH, 5),
                  const((1, ATT_WIDTH + LRU_WIDTH)), const((1, DN_WIDTH)), const((DN_WIDTH, DN_WIDTH)),
                  const((d, d)),
                  pl.BlockSpec((1, 1, 1, d), lambda bi, i: (_mod_row(bi, i + off), 2, 0, 0)),
                  const((1, d))],
        out_specs=pl.BlockSpec((1, TM, d), lambda bi, i: (bi, i, 0)),
        out_shape=jax.ShapeDtypeStruct((b, nb * TM, d), F32),
        compiler_params=_cparams("arbitrary", "arbitrary"),
        name="out_proj",
    )(xs, att, y, rest, o0, o1, rest, gg, gdn4, ones256, w_out, mod4, gn1)


def _ffn_kernel(x_ref, sh_ref, sc_ref, gt_ref, gn2_ref, gn3_ref, wi_ref, wo_ref, o_ref):
    x = x_ref[0]
    h = (_rms(x, gn2_ref[...]) * (1.0 + sc_ref[0, 0]) + sh_ref[0, 0]).astype(BF16)
    acc = jnp.zeros(x.shape, F32)
    for j in range(FFN_HIDDEN // FFN_CHUNK):
        lo = j * FFN_CHUNK
        gate = jnp.dot(h, wi_ref[:, lo:lo + FFN_CHUNK], preferred_element_type=F32)
        up = jnp.dot(h, wi_ref[:, FFN_HIDDEN + lo:FFN_HIDDEN + lo + FFN_CHUNK], preferred_element_type=F32)
        act = (_silu(gate) * up).astype(BF16)
        acc = acc + jnp.dot(act, wo_ref[lo:lo + FFN_CHUNK, :], preferred_element_type=F32)
    o_ref[0] = x + gt_ref[0, 0] * _rms(acc, gn3_ref[...])


def _ffn(x1, mod4, gn2, gn3, w_ffn_in, w_ffn_out, off):
    b, rows, d = x1.shape
    nb = rows // TM

    def mod(k):
        return pl.BlockSpec((1, 1, 1, d), lambda bi, i: (_mod_row(bi, i + off), k, 0, 0))

    def const(shape):
        return pl.BlockSpec(shape, lambda bi, i: (0,) * len(shape))

    return pl.pallas_call(
        _ffn_kernel,
        grid=(b, nb),
        in_specs=[pl.BlockSpec((1, TM, d), lambda bi, i: (bi, i, 0)),
                  mod(3), mod(4), mod(5), const((1, d)), const((1, d)),
                  const((d, 2 * FFN_HIDDEN)), const((FFN_HIDDEN, d))],
        out_specs=pl.BlockSpec((1, TM, d), lambda bi, i: (bi, i, 0)),
        out_shape=jax.ShapeDtypeStruct((b, rows, d), F32),
        compiler_params=_cparams("arbitrary", "arbitrary"),
        name="ffn",
    )(x1, mod4, mod4, mod4, gn2, gn3, w_ffn_in, w_ffn_out)


def _block_diag_ones(n):
    i = jnp.arange(n) // HEAD_DIM
    return (i[:, None] == i[None, :]).astype(BF16)


def _chunk_triangles(n):
    i = jnp.arange(n)
    same = (i[:, None] // DN_CHUNK) == (i[None, :] // DN_CHUNK)
    lower = jnp.logical_and(same, i[None, :] <= i[:, None])
    upper = jnp.logical_and(same, i[None, :] >= i[:, None])
    return jnp.stack([lower, upper, same]).astype(BF16)


def _head_spread(first_row):
    r = jnp.arange(LANES)[:, None]
    c = jnp.arange(N_DIR * DN_WIDTH)[None, :]
    return (r == first_row + c // DN_CHUNK).astype(BF16)


def _rope_tables(ctx, n):
    t = jnp.arange(n)
    row = (t // GRID_W).astype(F32)
    col = (t % GRID_W).astype(F32)
    n_freq = HEAD_DIM // 4
    inv_freq = ROPE_THETA ** (-jnp.arange(n_freq, dtype=F32) / n_freq)
    ang = jnp.concatenate([row[:, None] * inv_freq, col[:, None] * inv_freq], axis=-1)
    cos = jnp.concatenate([jnp.ones((ctx, HEAD_DIM // 2), F32), jnp.cos(ang)], axis=0)
    sin = jnp.concatenate([jnp.zeros((ctx, HEAD_DIM // 2), F32), jnp.sin(ang)], axis=0)
    return jnp.tile(jnp.concatenate([cos, cos], axis=1), (1, 2)), jnp.tile(jnp.concatenate([-sin, sin], axis=1), (1, 2))


def _lru_gate_weights(w_r, w_i):
    def dense(w):
        z = jnp.zeros((LRU_BLOCK_DIM, LRU_BLOCK_DIM), F32)
        return [jnp.block([[w[2 * hf], z], [z, w[2 * hf + 1]]]) for hf in range(2)]
    halves = [[], []]
    for d in range(N_DIR):
        for w in (w_r[d], w_i[d]):
            for hf, m in enumerate(dense(w)):
                halves[hf].append(m)
    return jnp.stack([jnp.concatenate(h, axis=1) for h in halves]).astype(BF16)


def _lru_vectors(lam, b_r, b_i):
    rows = [lam[0], lam[1], b_r[0], b_i[0], b_r[1], b_i[1], jnp.zeros_like(lam[0]), jnp.zeros_like(lam[0])]
    v = jnp.stack(rows)
    return jnp.stack([v[:, :LANES], v[:, LANES:]])


def kernel(x, c, ctx, c_ctx, w_ada, b_ada, g_norm, w_in, g_qk, lru_conv_w, lru_conv_b, lru_w_r, lru_b_r, lru_w_i, lru_b_i, lru_lambda, dn_conv_w, dn_a_log, dn_dt_bias, g_dn_out, g_group, w_out, w_ffn_in, w_ffn_out):
    depth = w_ada.shape[0]
    bsz, n, d = x.shape
    m = ctx.shape[1]
    assert m == TM and d == D_MODEL and bsz == 2 and n % (ATT_STEPS_PER_BODY * ATT_TK) == 0

    cvec = jnp.concatenate([c, c_ctx[None, :], jnp.zeros((SUBLANES - bsz - 1, d), F32)], axis=0)
    mods = _modulation(cvec, w_ada, b_ada)
    cosf, sinf = _rope_tables(m, n)
    ones128 = _block_diag_ones(LANES)
    ones256 = _block_diag_ones(DN_WIDTH)
    tri = _chunk_triangles(TM)
    xg = _head_spread(0)
    xb = _head_spread(N_DIR * DN_HEADS)

    xs = jnp.concatenate([ctx, x], axis=1)
    for l in range(depth):
        last = l == depth - 1
        off = 1 if last else 0
        mod4 = mods[l].reshape(SUBLANES, 6, 1, d)
        wl = w_in[l]
        w_in_p = jnp.concatenate([wl[:, 0:768], wl[:, 1280:2048], wl[:, 1024:1280], wl[:, 768:1024],
                                  wl[:, 2048:2320], jnp.zeros((d, IN_COLS_PAD - 2320), F32)], axis=1).astype(BF16)
        qt, k, vt, rest = _in_proj(xs, mod4, g_norm[l, 0][None, :], w_in_p, jnp.tile(g_qk[l], (1, 2)),
                                   cosf, sinf, ones128)
        key_norm_bound = 1.01 * HEAD_DIM ** 0.5 * jnp.max(jnp.abs(g_qk[l, 1]))
        score_bound = key_norm_bound * (LOG2E * jnp.max(jnp.abs(g_qk[l, 0])))
        att = _attention(qt, k, vt, key_norm_bound, score_bound, m, off)

        cw = jnp.concatenate([dn_conv_w[l], lru_conv_w[l]], axis=1)
        cb = jnp.concatenate([jnp.zeros((3 * DN_WIDTH,), F32), lru_conv_b[l]])[None, :]
        pad = jnp.zeros((LANES - N_DIR * DN_HEADS,), F32)
        dnv = jnp.stack([jnp.concatenate([dn_a_log[l].reshape(-1), pad]),
                         jnp.concatenate([dn_dt_bias[l].reshape(-1), pad])])
        xr, qk, pk, ge, rws = _prep(rest, cw, cb, dnv, ones256, tri, xg, xb)
        y = _lru(xr, _lru_gate_weights(lru_w_r[l], lru_w_i[l]),
                 _lru_vectors(lru_lambda[l], lru_b_r[l], lru_b_i[l]), m)
        o0, o1 = _deltanet(qk, pk, ge, rws, m)

        x1 = _out_proj(xs, att, y, rest, o0, o1, g_group[l][None, :], jnp.tile(g_dn_out[l], DN_HEADS)[None, :],
                       ones256, w_out[l].astype(BF16), mod4, g_norm[l, 1][None, :], off)
        xs = _ffn(x1, mod4, g_norm[l, 2][None, :], g_norm[l, 3][None, :],
                  w_ffn_in[l].astype(BF16), w_ffn_out[l].astype(BF16), off)
    return xs
```

```python
import functools

import jax
import jax.numpy as jnp
from jax import lax
from jax.experimental import pallas as pl
from jax.experimental.pallas import tpu as pltpu

F32 = jnp.float32
BF16 = jnp.bfloat16

D_MODEL = 1024
GRID_W = 64
EPS = 1e-6
ATT_GROUP = 4
ATT_KV_HEADS = 2
HEAD_DIM = 64
ATT_WIDTH = 512
ROPE_THETA = 10000.0
LRU_WIDTH = 256
LRU_BLOCK_DIM = 64
LRU_C = 8.0
DN_HEADS = 4
DN_WIDTH = 256
DN_CHUNK = 64
DN_BASE = 8
DN_BLOCK = 4 * DN_CHUNK
DN_ROWS = 2 * (DN_BLOCK // DN_CHUNK)
DN_PACK = 5 * DN_WIDTH
N_DIR = 2
FFN_HIDDEN = 2816

LANES = 128
SUBLANES = 8
TM = 256
ATT_TK = 512
ATT_STEPS_PER_BODY = 4
ATT_VROWS = 80
LOG2E = 1.4426950408889634
ATT_FIXED_SHIFT_LIMIT = 48.0
QKV_COLS = 768
CONV_COLS = 1024
REST_COLS = 1664
IN_COLS_PAD = QKV_COLS + REST_COLS
VMEM_LIMIT = 56 * 1024 * 1024


def _cparams(*sem):
    return pltpu.CompilerParams(dimension_semantics=sem, vmem_limit_bytes=VMEM_LIMIT)


def _rms(x, gain):
    ms = jnp.mean(x * x, axis=-1, keepdims=True)
    return x * lax.rsqrt(ms + EPS) * gain


def _seg_sumsq(x, ones_bd):
    sq = x * x
    hi = sq.astype(BF16)
    lo = (sq - hi.astype(F32)).astype(BF16)
    return (jnp.dot(hi, ones_bd, preferred_element_type=F32)
            + jnp.dot(lo, ones_bd, preferred_element_type=F32))


def _silu(x):
    return x * jax.nn.sigmoid(x)


def _softplus(x):
    return jnp.maximum(x, 0.0) + jnp.log1p(jnp.exp(-jnp.abs(x)))


def _mod_row(b, i):
    return jnp.where(i == 0, 2, b)


def _mod_kernel(c_ref, w_ref, b_ref, o_ref):
    s = _silu(c_ref[...])
    o_ref[0] = jnp.dot(s.astype(BF16), w_ref[0].astype(BF16), preferred_element_type=F32) + b_ref[0]


def _modulation(cvec, w_ada, b_ada):
    depth = w_ada.shape[0]
    d = D_MODEL
    return pl.pallas_call(
        _mod_kernel,
        grid=(depth, 6),
        in_specs=[pl.BlockSpec((SUBLANES, d), lambda l, j: (0, 0)),
                  pl.BlockSpec((1, d, d), lambda l, j: (l, 0, j)),
                  pl.BlockSpec((1, 1, d), lambda l, j: (l, 0, j))],
        out_specs=pl.BlockSpec((1, SUBLANES, d), lambda l, j: (l, 0, j)),
        out_shape=jax.ShapeDtypeStruct((depth, SUBLANES, 6 * d), F32),
        compiler_params=_cparams("arbitrary", "arbitrary"),
        name="modulation",
    )(cvec, w_ada, b_ada.reshape(depth, 1, 6 * d))


def _split3(x):
    hi = x.astype(BF16)
    r = x - hi.astype(F32)
    mid = r.astype(BF16)
    lo = (r - mid.astype(F32)).astype(BF16)
    return hi, mid, lo


def _dot_parts(parts, mat, left=False):
    out = None
    for p in parts:
        t = (jnp.dot(mat, p, preferred_element_type=F32) if left else jnp.dot(p, mat, preferred_element_type=F32))
        out = t if out is None else out + t
    return out


def _in_kernel(x_ref, c_ref, xp_ref, xn_ref, sh_ref, sc_ref, gn_ref, w_ref, gqk_ref, cos_ref, sin_ref, ones_ref,
               cw_ref, cb_ref, dnv_ref, ones2_ref, tri_ref, xg_ref, xb_ref,
               qt_ref, k_ref, vt_ref, gz_ref, xr_ref, qk_ref, pk_ref, ge_ref, rows_ref, *, split):
    i = pl.program_id(1)
    nb = pl.num_programs(1)
    c_, w_ = DN_CHUNK, DN_WIDTH
    x = jnp.where(i == 0, c_ref[0], x_ref[0]) if split else x_ref[0]
    xe = jnp.concatenate([x, xp_ref[0], xn_ref[0]], axis=0)
    hb = (_rms(xe, gn_ref[...]) * (1.0 + sc_ref[0, 0]) + sh_ref[0, 0]).astype(BF16)
    hb_cur = hb[0:TM]
    qkv = jnp.dot(hb_cur, w_ref[:, 0:QKV_COLS], preferred_element_type=F32)
    cosf = cos_ref[...]
    sinf = sin_ref[...]
    lane = lax.broadcasted_iota(jnp.int32, (TM, LANES), 1)
    first_half = (lane & 32) == 0
    ones_bd = ones_ref[...]

    def norm_rope(slab, gain):
        ss = _seg_sumsq(slab, ones_bd)
        y = slab * lax.rsqrt(ss * (1.0 / HEAD_DIM) + EPS) * gain
        partner = jnp.where(first_half, pltpu.roll(y, 96, axis=1), pltpu.roll(y, 32, axis=1))
        return y * cosf + partner * sinf

    for s in range(ATT_WIDTH // LANES):
        qs = norm_rope(qkv[:, LANES * s:LANES * (s + 1)], gqk_ref[0:1, :]) * (LOG2E * HEAD_DIM ** -0.5)
        qst = qs.T.astype(BF16)
        for e in range(2):
            g, hh = divmod(2 * s + e, ATT_GROUP)
            qt_ref[0, g, 0, 0:HEAD_DIM, TM * hh:TM * (hh + 1)] = qst[HEAD_DIM * e:HEAD_DIM * (e + 1), :]
    for g in range(ATT_KV_HEADS):
        qt_ref[0, g, 0, HEAD_DIM:, :] = jnp.zeros((LANES - HEAD_DIM, ATT_GROUP * TM), BF16)
    kk = norm_rope(qkv[:, 512:640], gqk_ref[1:2, :])
    one_lane = jnp.where(lane == HEAD_DIM, 1.0, 0.0)
    vvt = qkv[:, 640:768].T.astype(BF16)
    row = lax.broadcasted_iota(jnp.int32, (ATT_VROWS - HEAD_DIM, TM), 0)
    ones_rows = jnp.where(row == 0, 1.0, 0.0).astype(BF16)
    for g in range(ATT_KV_HEADS):
        kg = kk if g == 0 else pltpu.roll(kk, HEAD_DIM, axis=1)
        k_ref[0, g] = jnp.where(lane < HEAD_DIM, kg, one_lane).astype(BF16)
        vt_ref[0, g, 0, 0:HEAD_DIM, :] = vvt[HEAD_DIM * g:HEAD_DIM * (g + 1), :]
        vt_ref[0, g, 0, HEAD_DIM:, :] = ones_rows

    tail = jnp.dot(hb_cur, w_ref[:, QKV_COLS + CONV_COLS:], preferred_element_type=F32)
    gz_ref[0] = tail[:, 0:LRU_WIDTH + DN_WIDTH]
    ab = tail[:, LRU_WIDTH + DN_WIDTH:]

    cin = jnp.dot(hb, w_ref[:, QKV_COLS:QKV_COLS + CONV_COLS], preferred_element_type=F32)
    cur = cin[0:TM]
    prv = jnp.where(i >= 2, cin[TM:TM + SUBLANES], 0.0)
    nxt = jnp.where(jnp.logical_and(i >= 1, i < nb - 1), cin[TM + SUBLANES:], 0.0)
    trow = lax.broadcasted_iota(jnp.int32, cur.shape, 0)
    x_m1 = jnp.where(trow == 0, prv[SUBLANES - 1:SUBLANES, :], pltpu.roll(cur, 1, axis=0))
    x_p1 = jnp.where(trow == TM - 1, nxt[0:1, :], pltpu.roll(cur, TM - 1, axis=0))
    x_p2 = jnp.where(trow == TM - 2, nxt[0:1, :],
                     jnp.where(trow == TM - 1, nxt[1:2, :], pltpu.roll(cur, TM - 2, axis=0)))
    conv = (cb_ref[...] + x_m1 * cw_ref[0:1, :] + cur * cw_ref[1:2, :]
            + x_p1 * cw_ref[2:3, :] + x_p2 * cw_ref[3:4, :])
    xr_ref[0] = conv[:, 3 * w_:]
    act = _silu(conv[:, 0:3 * w_])
    qq = act[:, 0:w_]
    kq = act[:, w_:2 * w_]
    vv = act[:, 2 * w_:3 * w_]
    ones2 = ones2_ref[...]
    qn = qq * lax.rsqrt(_seg_sumsq(qq, ones2) + EPS) * (c_ ** -0.5)
    kn = kq * lax.rsqrt(_seg_sumsq(kq, ones2) + EPS)
    qk_ref[0, :, 0:w_] = qn.astype(BF16)
    qk_ref[0, :, w_:2 * w_] = kn.astype(BF16)

    g = jnp.where(lane < N_DIR * DN_HEADS, -jnp.exp(dnv_ref[0:1, :]) * _softplus(ab + dnv_ref[1:2, :]), 0.0)
    beta = jax.nn.sigmoid(ab)
    g_parts = _split3(g)
    gc = jnp.where(lane < DN_HEADS, _dot_parts(g_parts, tri_ref[0], left=True),
                   _dot_parts(g_parts, tri_ref[1], left=True))
    tot = _dot_parts(g_parts, tri_ref[2], left=True)
    ge = _dot_parts(_split3(gc), xg_ref[...])
    te = _dot_parts(_split3(tot), xg_ref[...])
    be = _dot_parts(_split3(beta)[:2], xb_ref[...])
    ge_ref[0] = ge
    crow = lax.broadcasted_iota(jnp.int32, (c_, w_), 0)
    diag = crow == (lax.broadcasted_iota(jnp.int32, (c_, w_), 1) & (c_ - 1))
    for d in range(N_DIR):
        ge_d = ge[:, w_ * d:w_ * (d + 1)]
        te_d = te[:, w_ * d:w_ * (d + 1)]
        be_d = be[:, w_ * d:w_ * (d + 1)]
        eg = jnp.exp(ge_d)
        kb = kn * be_d
        pk_ref[0, d, :, 0:w_] = kb.astype(BF16)
        pk_ref[0, d, :, w_:2 * w_] = (vv * be_d).astype(BF16)
        pk_ref[0, d, :, 2 * w_:3 * w_] = (kb * eg).astype(BF16)
        pk_ref[0, d, :, 3 * w_:4 * w_] = (qn * eg).astype(BF16)
        pk_ref[0, d, :, 4 * w_:5 * w_] = (kn * jnp.exp(te_d - ge_d)).astype(BF16)
        for pr in range(TM // DN_BLOCK):
            for c in range(DN_BLOCK // c_):
                r0 = DN_BLOCK * pr + c_ * c
                rows_ref[0, d, pr, c:c + 1, :] = jnp.sum(jnp.where(diag, ge_d[r0:r0 + c_, :], 0.0), axis=0,
                                                         keepdims=True)
                rows_ref[0, d, pr, DN_ROWS // 2 + c:DN_ROWS // 2 + c + 1, :] = te_d[r0:r0 + 1, :]


def _in_proj(stream, mod4, gn0, w_in_p, gqk2, cosf, sinf, ones128, cw, cb, dnv, ones256, tri, xg, xb):
    c_src, x_src, lat_off, t = stream
    b, rows_src, d = x_src.shape
    nb = t // TM
    r8 = TM // SUBLANES
    ppb = TM // DN_BLOCK

    def const(shape):
        return pl.BlockSpec(shape, lambda bi, i: (0,) * len(shape))

    def halo(step):
        return pl.BlockSpec((1, SUBLANES, d), lambda bi, i: (
            bi, jnp.clip((i - lat_off) * r8 + step, 0, rows_src // SUBLANES - 1), 0))

    return pl.pallas_call(
        functools.partial(_in_kernel, split=lat_off > 0),
        grid=(b, nb),
        in_specs=[pl.BlockSpec((1, TM, d), lambda bi, i: (bi, jnp.maximum(i - lat_off, 0), 0)),
                  pl.BlockSpec((1, TM, d), lambda bi, i: (bi, 0, 0)),
                  halo(-1), halo(r8),
                  pl.BlockSpec((1, 1, 1, d), lambda bi, i: (_mod_row(bi, i), 0, 0, 0)),
                  pl.BlockSpec((1, 1, 1, d), lambda bi, i: (_mod_row(bi, i), 1, 0, 0)),
                  const((1, d)), const((d, IN_COLS_PAD)), const((2, LANES)),
                  pl.BlockSpec((TM, LANES), lambda bi, i: (i, 0)),
                  pl.BlockSpec((TM, LANES), lambda bi, i: (i, 0)),
                  const((LANES, LANES)),
                  const((4, CONV_COLS)), const((1, CONV_COLS)), const((2, LANES)), const((DN_WIDTH, DN_WIDTH)),
                  const((3, TM, TM)), const((LANES, N_DIR * DN_WIDTH)), const((LANES, N_DIR * DN_WIDTH))],
        out_specs=[pl.BlockSpec((1, ATT_KV_HEADS, 1, LANES, ATT_GROUP * TM), lambda bi, i: (bi, 0, i, 0, 0)),
                   pl.BlockSpec((1, ATT_KV_HEADS, TM, LANES), lambda bi, i: (bi, 0, i, 0)),
                   pl.BlockSpec((1, ATT_KV_HEADS, 1, ATT_VROWS, TM), lambda bi, i: (bi, 0, i, 0, 0)),
                   pl.BlockSpec((1, TM, LRU_WIDTH + DN_WIDTH), lambda bi, i: (bi, i, 0)),
                   pl.BlockSpec((1, TM, LRU_WIDTH), lambda bi, i: (bi, i, 0)),
                   pl.BlockSpec((1, TM, 2 * DN_WIDTH), lambda bi, i: (bi, i, 0)),
                   pl.BlockSpec((1, N_DIR, TM, DN_PACK), lambda bi, i: (bi, 0, i, 0)),
                   pl.BlockSpec((1, TM, N_DIR * DN_WIDTH), lambda bi, i: (bi, i, 0)),
                   pl.BlockSpec((1, N_DIR, ppb, DN_ROWS, DN_WIDTH), lambda bi, i: (bi, 0, i, 0, 0))],
        out_shape=[jax.ShapeDtypeStruct((b, ATT_KV_HEADS, nb, LANES, ATT_GROUP * TM), BF16),
                   jax.ShapeDtypeStruct((b, ATT_KV_HEADS, t, LANES), BF16),
                   jax.ShapeDtypeStruct((b, ATT_KV_HEADS, nb, ATT_VROWS, TM), BF16),
                   jax.ShapeDtypeStruct((b, t, LRU_WIDTH + DN_WIDTH), F32),
                   jax.ShapeDtypeStruct((b, t, LRU_WIDTH), F32),
                   jax.ShapeDtypeStruct((b, t, 2 * DN_WIDTH), BF16),
                   jax.ShapeDtypeStruct((b, N_DIR, t, DN_PACK), BF16),
                   jax.ShapeDtypeStruct((b, t, N_DIR * DN_WIDTH), F32),
                   jax.ShapeDtypeStruct((b, N_DIR, t // DN_BLOCK, DN_ROWS, DN_WIDTH), F32)],
        compiler_params=_cparams("arbitrary", "arbitrary"),
        name="in_proj",
    )(x_src, c_src, x_src, x_src, mod4, mod4, gn0, w_in_p, gqk2, cosf, sinf, ones128,
      cw, cb, dnv, ones256, tri, xg, xb)


def _att_kernel(qt_ref, k_ref, vt_ref, kn_ref, o_ref, m_ref, acc_ref, s_ref, cmax_ref, *, ctx, n_lat, q_off):
    del kn_ref
    i = pl.program_id(2) + q_off
    qt = qt_ref[0, 0, 0]
    nsub = ATT_TK // TM

    def scores(start, size):
        return jnp.dot(k_ref[0, 0, pl.ds(start, size), :], qt, preferred_element_type=F32)

    def pv(pt, blk0, nblk):
        out = None
        for j in range(nblk):
            t = jnp.dot(vt_ref[0, 0, blk0 + j], pt[j * TM:(j + 1) * TM, :], preferred_element_type=F32)
            out = t if out is None else out + t
        return out

    st = scores(0, ctx)
    m0 = jnp.max(st, axis=0, keepdims=True)
    m_ref[...] = m0
    acc_ref[...] = pv(jnp.exp2(st - m0).astype(BF16), 0, ctx // TM)

    n_chunks = n_lat // ATT_TK

    def step(j, cur, nxt):
        jn = jnp.minimum(j + 1, n_chunks - 1)
        sn = scores(pl.multiple_of(ctx + jn * ATT_TK, TM), ATT_TK)
        s_ref[nxt] = sn
        cmax_ref[nxt] = jnp.max(sn, axis=0, keepdims=True)
        st = s_ref[cur]
        m_prev = m_ref[...]
        m_new = jnp.maximum(m_prev, cmax_ref[cur])
        alpha = jnp.exp2(m_prev - m_new)
        pt = jnp.exp2(st - m_new).astype(BF16)
        acc_ref[...] = alpha * acc_ref[...] + pv(pt, ctx // TM + j * nsub, nsub)
        m_ref[...] = m_new

    def body(jj, carry):
        step(2 * jj, 0, 1)
        step(2 * jj + 1, 1, 0)
        return carry

    @pl.when(i > 0)
    def _():
        s0 = scores(ctx, ATT_TK)
        s_ref[0] = s0
        cmax_ref[0] = jnp.max(s0, axis=0, keepdims=True)
        lax.fori_loop(0, n_chunks // 2, body, 0)

    acc = acc_ref[...]
    out_t = acc[0:HEAD_DIM] / acc[HEAD_DIM:HEAD_DIM + 1]
    for h in range(ATT_GROUP):
        o_ref[0, :, HEAD_DIM * h:HEAD_DIM * (h + 1)] = out_t[:, TM * h:TM * (h + 1)].T


def _att_fixed_kernel(qt_ref, k_ref, vt_ref, kn_ref, o_ref, acc_ref, p_ref, qa_ref, *, ctx, n_lat, q_off):
    i = pl.program_id(2) + q_off
    nsub = ATT_TK // TM
    n_chunks = n_lat // ATT_TK
    qt = qt_ref[0, 0, 0]
    q32 = qt[0:HEAD_DIM, :].astype(F32)
    r = jnp.sqrt(jnp.sum(q32 * q32, axis=0, keepdims=True)) * kn_ref[...]
    tile_row = lax.broadcasted_iota(jnp.int32, (2 * SUBLANES, ATT_GROUP * TM), 0)
    qa_ref[...] = qt
    qa_ref[HEAD_DIM:HEAD_DIM + 2 * SUBLANES, :] = jnp.where(tile_row == 0, -r, 0.0).astype(BF16)

    def probs(start, size):
        s = jnp.dot(k_ref[0, 0, pl.ds(start, size), :], qa_ref[...], preferred_element_type=F32)
        return jnp.exp2(s).astype(BF16)

    def pv(pt, blk0, nblk):
        out = None
        for j in range(nblk):
            t = jnp.dot(vt_ref[0, 0, blk0 + j], pt[j * TM:(j + 1) * TM, :], preferred_element_type=F32)
            out = t if out is None else out + t
        return out

    def step(j, cur, nxt, prefetch=True):
        if prefetch:
            p_ref[nxt] = probs(pl.multiple_of(ctx + (j + 1) * ATT_TK, TM), ATT_TK)
        acc_ref[...] += pv(p_ref[cur], ctx // TM + j * nsub, nsub)

    def body(jj, carry, last=False):
        for u in range(ATT_STEPS_PER_BODY):
            step(ATT_STEPS_PER_BODY * jj + u, (u + 1) % 2, u % 2,
                 prefetch=not (last and u == ATT_STEPS_PER_BODY - 1))
        return carry

    @pl.when(i == 0)
    def _():
        acc_ref[...] = pv(probs(0, ctx), 0, ctx // TM)

    @pl.when(i > 0)
    def _():
        p_ctx = probs(0, ctx)
        p_ref[1] = probs(ctx, ATT_TK)
        acc_ref[...] = pv(p_ctx, 0, ctx // TM)
        n_bodies = n_chunks // ATT_STEPS_PER_BODY
        lax.fori_loop(0, n_bodies - 1, body, 0)
        body(n_bodies - 1, 0, last=True)

    acc = acc_ref[...]
    out_t = acc[0:HEAD_DIM] / acc[HEAD_DIM:HEAD_DIM + 1]
    for h in range(ATT_GROUP):
        o_ref[0, :, HEAD_DIM * h:HEAD_DIM * (h + 1)] = out_t[:, TM * h:TM * (h + 1)].T


def _attention(qt, k, vt, key_norm_bound, score_bound, ctx, q_off):
    b, _, t, _ = k.shape
    nb = t // TM
    nq = nb - q_off
    gw = ATT_GROUP * HEAD_DIM
    cols = ATT_GROUP * TM

    def call(body, scratch):
        return pl.pallas_call(
            functools.partial(body, ctx=ctx, n_lat=t - ctx, q_off=q_off),
            grid=(b, ATT_KV_HEADS, nq),
            in_specs=[pl.BlockSpec((1, 1, 1, LANES, cols), lambda bi, g, i: (bi, g, i + q_off, 0, 0)),
                      pl.BlockSpec((1, 1, t, LANES), lambda bi, g, i: (bi, g, 0, 0)),
                      pl.BlockSpec((1, 1, nb, ATT_VROWS, TM), lambda bi, g, i: (bi, g, 0, 0, 0)),
                      pl.BlockSpec((1, 1), lambda bi, g, i: (0, 0))],
            out_specs=pl.BlockSpec((1, TM, gw), lambda bi, g, i: (bi, i, g)),
            out_shape=jax.ShapeDtypeStruct((b, nq * TM, ATT_WIDTH), F32),
            scratch_shapes=scratch,
            compiler_params=_cparams("arbitrary", "arbitrary", "arbitrary"),
            name="attention",
        )

    online = call(_att_kernel, [pltpu.VMEM((1, cols), F32), pltpu.VMEM((ATT_VROWS, cols), F32),
                                pltpu.VMEM((2, ATT_TK, cols), F32), pltpu.VMEM((2, 1, cols), F32)])
    fixed = call(_att_fixed_kernel, [pltpu.VMEM((ATT_VROWS, cols), F32), pltpu.VMEM((2, ATT_TK, cols), BF16),
                                     pltpu.VMEM((LANES, cols), BF16)])
    return lax.cond(score_bound <= ATT_FIXED_SHIFT_LIMIT, fixed, online, qt, k, vt,
                    jnp.reshape(key_norm_bound, (1, 1)))


def _lru_kernel(xr_ref, wg_ref, vec_ref, y_ref, a0, u0, a1, u1, *, ctx, n):
    nseg = SUBLANES
    vec = vec_ref[0]
    a_refs = (a0, a1)
    u_refs = (u0, u1)

    def gates(x, d):
        g = jnp.dot(x.astype(BF16), wg_ref[0, :, 2 * LANES * d:2 * LANES * (d + 1)],
                    preferred_element_type=F32)
        r = 0.5 * jnp.tanh(0.5 * (g[:, :LANES] + vec[2 + 2 * d:3 + 2 * d])) + 0.5
        ig = 0.5 * jnp.tanh(0.5 * (g[:, LANES:] + vec[3 + 2 * d:4 + 2 * d])) + 0.5
        log_a = -LRU_C * r * _softplus(-vec[d:d + 1])
        a = jnp.exp(log_a)
        mult = jnp.sqrt(-jnp.tanh(log_a) * (a * a + 1.0))
        return a, mult * (ig * x)

    def run(row0, seg, base, init):
        pitch = seg + SUBLANES
        for j in range(nseg):
            x = xr_ref[0, row0 + j * seg:row0 + (j + 1) * seg, :]
            for d in range(N_DIR):
                a, u = gates(x, d)
                a_refs[d][base + j * pitch:base + j * pitch + seg, :] = a
                u_refs[d][base + j * pitch:base + j * pitch + seg, :] = u

        def step(t, carry):
            out = []
            for d in range(N_DIR):
                h, p = carry[2 * d], carry[2 * d + 1]
                r = base + (t if d == 0 else seg - 1 - t)
                idx = pl.ds(r, nseg, stride=pitch)
                a = a_refs[d][idx, :]
                h = a * h + u_refs[d][idx, :]
                p = a * p
                u_refs[d][idx, :] = h
                a_refs[d][idx, :] = p
                out += [h, p]
            return tuple(out)

        z = jnp.zeros((nseg, LANES), F32)
        o = jnp.ones((nseg, LANES), F32)
        fin = lax.fori_loop(0, seg, step, (z, o, z, o), unroll=8)
        carries, finals = [], []
        for d in range(N_DIR):
            hl, pl_ = fin[2 * d], fin[2 * d + 1]
            order = list(range(nseg)) if d == 0 else list(range(nseg - 1, -1, -1))
            c = init[d]
            cs = {}
            for j in order:
                cs[j] = c
                c = hl[j:j + 1] + pl_[j:j + 1] * c
            carries.append(cs)
            finals.append(c)
        for j in range(nseg):
            sl = slice(base + j * pitch, base + j * pitch + seg)
            y = (u0[sl, :] + a0[sl, :] * carries[0][j]) + (u1[sl, :] + a1[sl, :] * carries[1][j])
            y_ref[0, row0 + j * seg:row0 + (j + 1) * seg, :] = y
        return finals

    zero = jnp.zeros((1, LANES), F32)
    seg_c = ctx // nseg
    fin_c = run(0, seg_c, 0, [zero, zero])
    run(ctx, n // nseg, nseg * (seg_c + SUBLANES), fin_c)


def _lru(xr, wg, vec, ctx):
    b, t, _ = xr.shape
    n = t - ctx
    rows = SUBLANES * (ctx // SUBLANES + SUBLANES) + SUBLANES * (n // SUBLANES + SUBLANES)
    kern = functools.partial(_lru_kernel, ctx=ctx, n=n)
    return pl.pallas_call(
        kern,
        grid=(b, LRU_WIDTH // LANES),
        in_specs=[pl.BlockSpec((1, t, LANES), lambda bi, hf: (bi, 0, hf)),
                  pl.BlockSpec((1, LANES, 4 * LANES), lambda bi, hf: (hf, 0, 0)),
                  pl.BlockSpec((1, SUBLANES, LANES), lambda bi, hf: (hf, 0, 0))],
        out_specs=pl.BlockSpec((1, t, LANES), lambda bi, hf: (bi, 0, hf)),
        out_shape=jax.ShapeDtypeStruct((b, t, LRU_WIDTH), F32),
        scratch_shapes=[pltpu.VMEM((rows, LANES), F32)] * 4,
        compiler_params=_cparams("arbitrary", "arbitrary"),
        name="rglru",
    )(xr, wg, vec)


def _dn_kernel(qkf_ref, pkf_ref, gef_ref, rwf_ref, qkb_ref, pkb_ref, geb_ref, rwb_ref, o0_ref, o1_ref, s_ref):
    c_, w_ = DN_CHUNK, DN_WIDTH
    n_batch = o0_ref.shape[0]

    @pl.when(pl.program_id(0) == 0)
    def _():
        s_ref[...] = jnp.zeros_like(s_ref)

    row = lax.broadcasted_iota(jnp.int32, (c_, w_), 0)
    lane = lax.broadcasted_iota(jnp.int32, (c_, w_), 1)
    lj = lane & (c_ - 1)
    head_masks = [(lane >> 6) == h for h in range(DN_HEADS)]
    eye_f = jnp.where(row == lj, 1.0, 0.0)
    shift = DN_BASE.bit_length() - 1
    same_base = (row >> shift) == (lj >> shift)
    off_blk = ({}, {})
    size = DN_BASE
    while size < c_:
        shift = size.bit_length() - 1
        bi, bj = row >> shift, lj >> shift
        off_blk[0][size] = jnp.logical_and(bi == bj + 1, (bi & 1) == 1)
        off_blk[1][size] = jnp.logical_and(bj == bi + 1, (bj & 1) == 1)
        size *= 2
    bd_mask = ((lax.broadcasted_iota(jnp.int32, (w_, w_), 0) >> 6)
               == (lax.broadcasted_iota(jnp.int32, (w_, w_), 1) >> 6))
    nt = (((1,), (1,)), ((), ()))
    tn = (((0,), (0,)), ((), ()))

    def bd(x):
        xb = x.astype(BF16)
        zero = jnp.zeros_like(xb)
        return jnp.concatenate([jnp.where(hm, xb, zero) for hm in head_masks], axis=0)

    def bdot(a, b, dims=None):
        if dims is None:
            return jnp.dot(a.astype(BF16), b, preferred_element_type=F32)
        return lax.dot_general(a.astype(BF16), b, dims, preferred_element_type=F32)

    per_block = DN_BLOCK // c_
    in_refs = ((qkf_ref, pkf_ref, gef_ref, rwf_ref), (qkb_ref, pkb_ref, geb_ref, rwb_ref))
    o_refs = (o0_ref, o1_ref)
    probs = []
    for d in range(N_DIR):
        incl = (lj <= row) if d == 0 else (lj >= row)
        strict = (lj < row) if d == 0 else (lj > row)
        qk_ref, pk_ref, ge_ref, rw_ref = in_refs[d]
        for bi, step in [(bi, step) for bi in range(n_batch) for step in range(per_block)]:
            c = step if d == 0 else per_block - 1 - step
            rows = slice(c_ * c, c_ * (c + 1))
            decay = jnp.where(incl, jnp.exp(jnp.where(incl, ge_ref[bi, rows, :] - rw_ref[bi, 0, 0, c:c + 1, :], 0.0)),
                              0.0)
            probs.append(dict(d=d, bi=bi, step=step, rows=rows, incl=incl, strict=strict, decay=decay,
                              st=N_DIR * bi + d,
                              kbq=jnp.concatenate([pk_ref[bi, 0, rows, 0:w_], qk_ref[bi, rows, 0:w_]], axis=0),
                              k=qk_ref[bi, rows, w_:2 * w_], vb=pk_ref[bi, 0, rows, w_:2 * w_],
                              kbe=pk_ref[bi, 0, rows, 2 * w_:3 * w_], qd=pk_ref[bi, 0, rows, 3 * w_:4 * w_],
                              kt=pk_ref[bi, 0, rows, 4 * w_:5 * w_],
                              gt=jnp.exp(rw_ref[bi, 0, 0, per_block + c:per_block + c + 1, :])))
    for p in probs:
        a1 = bdot(p["kbq"], bd(p["k"]), nt)
        p["lower"] = jnp.where(p["strict"], a1[:c_] * p["decay"], 0.0)
        p["attn"] = jnp.where(p["incl"], a1[c_:] * p["decay"], 0.0)
        p["pw"] = jnp.where(same_base, p["lower"], 0.0)
        p["tm"] = eye_f - p["pw"]
    for p in probs:
        p["pw"] = bdot(p["pw"], bd(p["pw"]))
    for p in probs:
        t = bdot(jnp.concatenate([p["tm"], p["pw"]], axis=0), bd(p["pw"]))
        p["tm"] = p["tm"] + t[:c_]
        p["pw"] = t[c_:]
    for p in probs:
        p["tm"] = p["tm"] + bdot(p["tm"], bd(p["pw"]))
    size = DN_BASE
    while size < c_:
        for p in probs:
            p["tmp"] = bdot(jnp.where(off_blk[p["d"]][size], p["lower"], 0.0), bd(p["tm"]))
        for p in probs:
            p["tm"] = p["tm"] - bdot(p["tm"], bd(p["tmp"]))
        size *= 2
    for p in probs:
        sol = bdot(p["tm"], jnp.concatenate([bd(p["vb"]), bd(p["kbe"])], axis=1))
        p["u"] = sol[:, :w_]
        p["wqd"] = jnp.concatenate([sol[:, w_:].astype(BF16), p["qd"]], axis=0)
    for step in range(per_block):
        cur = [p for p in probs if p["step"] == step]
        for p in cur:
            p["s_old"] = s_ref[p["st"]]
            p["a2"] = bdot(p["wqd"], p["s_old"].astype(BF16))
        for p in cur:
            p["v_new"] = p["u"] - p["a2"][:c_]
        for p in cur:
            o_refs[p["d"]][p["bi"], p["rows"], :] = p["a2"][c_:] + bdot(p["attn"], bd(p["v_new"]))
        for p in cur:
            upd = bdot(p["kt"], p["v_new"].astype(BF16), tn)
            s_ref[p["st"]] = p["s_old"] * p["gt"] + jnp.where(bd_mask, upd, 0.0)


def _deltanet(qk, pk, ge, rws, ctx):
    b, t, _ = qk.shape
    np_ = t // DN_BLOCK
    npc = ctx // DN_BLOCK

    def bwd(p):
        return jnp.where(p < npc, npc - 1 - p, np_ - 1 - (p - npc))

    def specs(d, pos):
        return [pl.BlockSpec((b, DN_BLOCK, 2 * DN_WIDTH), lambda p: (0, pos(p), 0)),
                pl.BlockSpec((b, 1, DN_BLOCK, DN_PACK), lambda p: (0, d, pos(p), 0)),
                pl.BlockSpec((b, DN_BLOCK, DN_WIDTH), lambda p: (0, pos(p), d)),
                pl.BlockSpec((b, 1, 1, DN_ROWS, DN_WIDTH), lambda p: (0, d, pos(p), 0, 0))]

    o_shape = jax.ShapeDtypeStruct((b, t, DN_WIDTH), F32)
    return pl.pallas_call(
        _dn_kernel,
        grid=(np_,),
        in_specs=specs(0, lambda p: p) + specs(1, bwd),
        out_specs=[pl.BlockSpec((b, DN_BLOCK, DN_WIDTH), lambda p: (0, p, 0)),
                   pl.BlockSpec((b, DN_BLOCK, DN_WIDTH), lambda p: (0, bwd(p), 0))],
        out_shape=[o_shape, o_shape],
        scratch_shapes=[pltpu.VMEM((b * N_DIR, DN_WIDTH, DN_WIDTH), F32)],
        compiler_params=_cparams("arbitrary"),
        name="deltanet",
    )(qk, pk, ge, rws, qk, pk, ge, rws)


def _out_kernel(x_ref, c_ref, att_ref, y_ref, lg_ref, o0_ref, o1_ref, dz_ref, gg_ref, gdn_ref, ones_ref, w_ref,
                gt_ref, gn_ref, xo_ref, *, off, split):
    gg = gg_ref[...]
    a_n = _rms(att_ref[0], gg[:, 0:ATT_WIDTH]).astype(BF16)
    lg = lg_ref[0]
    gelu = lg * (0.5 * (1.0 + jnp.tanh(0.7978845608028654 * (lg + 0.044715 * (lg * lg * lg)))))
    l_n = _rms(gelu * y_ref[0], gg[:, ATT_WIDTH:]).astype(BF16)
    o = o0_ref[0] + o1_ref[0]
    ss = _seg_sumsq(o, ones_ref[...])
    dn = ((o * lax.rsqrt(ss * (1.0 / DN_CHUNK) + EPS) * gdn_ref[...]) * _silu(dz_ref[0])).astype(BF16)
    mix = (jnp.dot(a_n, w_ref[0:ATT_WIDTH, :], preferred_element_type=F32)
           + jnp.dot(l_n, w_ref[ATT_WIDTH:ATT_WIDTH + LRU_WIDTH, :], preferred_element_type=F32)
           + jnp.dot(dn, w_ref[ATT_WIDTH + LRU_WIDTH:, :], preferred_element_type=F32))
    x = jnp.where(pl.program_id(1) + off == 0, c_ref[0], x_ref[0]) if split else x_ref[0]
    xo_ref[0] = x + gt_ref[0, 0] * _rms(mix, gn_ref[...])


def _out_proj(stream, att, y, gz, o0, o1, gg, gdn4, ones256, w_out, mod4, gn1, off):
    c_src, x_src, lat_off, t = stream
    b, _, d = x_src.shape
    nb = t // TM - off
    att_off = off - (t - att.shape[1]) // TM

    def tok(width, col=0):
        return pl.BlockSpec((1, TM, width), lambda bi, i: (bi, i + off, col))

    def const(shape):
        return pl.BlockSpec(shape, lambda bi, i: (0,) * len(shape))

    return pl.pallas_call(
        functools.partial(_out_kernel, off=off, split=lat_off > 0),
        grid=(b, nb),
        in_specs=[pl.BlockSpec((1, TM, d), lambda bi, i: (bi, jnp.maximum(i + off - lat_off, 0), 0)),
                  pl.BlockSpec((1, TM, d), lambda bi, i: (bi, 0, 0)),
                  pl.BlockSpec((1, TM, ATT_WIDTH), lambda bi, i: (bi, i + att_off, 0)),
                  tok(LRU_WIDTH), tok(LRU_WIDTH, 0), tok(DN_WIDTH), tok(DN_WIDTH), tok(DN_WIDTH, 1),
                  const((1, ATT_WIDTH + LRU_WIDTH)), const((1, DN_WIDTH)), const((DN_WIDTH, DN_WIDTH)),
                  const((d, d)),
                  pl.BlockSpec((1, 1, 1, d), lambda bi, i: (_mod_row(bi, i + off), 2, 0, 0)),
                  const((1, d))],
        out_specs=pl.BlockSpec((1, TM, d), lambda bi, i: (bi, i, 0)),
        out_shape=jax.ShapeDtypeStruct((b, nb * TM, d), F32),
        compiler_params=_cparams("arbitrary", "arbitrary"),
        name="out_proj",
    )(x_src, c_src, att, y, gz, o0, o1, gz, gg, gdn4, ones256, w_out, mod4, gn1)


def _ffn_kernel(x_ref, sh_ref, sc_ref, gt_ref, gn2_ref, gn3_ref, wi_ref, wo_ref, o_ref):
    x = x_ref[0]
    h = (_rms(x, gn2_ref[...]) * (1.0 + sc_ref[0, 0]) + sh_ref[0, 0]).astype(BF16)
    gu = jnp.dot(h, wi_ref[...], preferred_element_type=F32)
    act = (_silu(gu[:, :FFN_HIDDEN]) * gu[:, FFN_HIDDEN:]).astype(BF16)
    out = jnp.dot(act, wo_ref[...], preferred_element_type=F32)
    o_ref[0] = x + gt_ref[0, 0] * _rms(out, gn3_ref[...])


def _ffn(x1, mod4, gn2, gn3, w_ffn_in, w_ffn_out, off):
    b, rows, d = x1.shape
    nb = rows // TM

    def mod(k):
        return pl.BlockSpec((1, 1, 1, d), lambda bi, i: (_mod_row(bi, i + off), k, 0, 0))

    def const(shape):
        return pl.BlockSpec(shape, lambda bi, i: (0,) * len(shape))

    return pl.pallas_call(
        _ffn_kernel,
        grid=(b, nb),
        in_specs=[pl.BlockSpec((1, TM, d), lambda bi, i: (bi, i, 0)),
                  mod(3), mod(4), mod(5), const((1, d)), const((1, d)),
                  const((d, 2 * FFN_HIDDEN)), const((FFN_HIDDEN, d))],
        out_specs=pl.BlockSpec((1, TM, d), lambda bi, i: (bi, i, 0)),
        out_shape=jax.ShapeDtypeStruct((b, rows, d), F32),
        compiler_params=_cparams("arbitrary", "arbitrary"),
        name="ffn",
    )(x1, mod4, mod4, mod4, gn2, gn3, w_ffn_in, w_ffn_out)


def _block_diag_ones(n):
    i = jnp.arange(n) // HEAD_DIM
    return (i[:, None] == i[None, :]).astype(BF16)


def _chunk_triangles(n):
    i = jnp.arange(n)
    same = (i[:, None] // DN_CHUNK) == (i[None, :] // DN_CHUNK)
    lower = jnp.logical_and(same, i[None, :] <= i[:, None])
    upper = jnp.logical_and(same, i[None, :] >= i[:, None])
    return jnp.stack([lower, upper, same]).astype(BF16)


def _head_spread(first_row):
    r = jnp.arange(LANES)[:, None]
    c = jnp.arange(N_DIR * DN_WIDTH)[None, :]
    return (r == first_row + c // DN_CHUNK).astype(BF16)


def _rope_tables(ctx, n):
    t = jnp.arange(n)
    row = (t // GRID_W).astype(F32)
    col = (t % GRID_W).astype(F32)
    n_freq = HEAD_DIM // 4
    inv_freq = ROPE_THETA ** (-jnp.arange(n_freq, dtype=F32) / n_freq)
    ang = jnp.concatenate([row[:, None] * inv_freq, col[:, None] * inv_freq], axis=-1)
    cos = jnp.concatenate([jnp.ones((ctx, HEAD_DIM // 2), F32), jnp.cos(ang)], axis=0)
    sin = jnp.concatenate([jnp.zeros((ctx, HEAD_DIM // 2), F32), jnp.sin(ang)], axis=0)
    return jnp.tile(jnp.concatenate([cos, cos], axis=1), (1, 2)), jnp.tile(jnp.concatenate([-sin, sin], axis=1), (1, 2))


def _lru_gate_weights(w_r, w_i):
    def dense(w):
        z = jnp.zeros((LRU_BLOCK_DIM, LRU_BLOCK_DIM), F32)
        return [jnp.block([[w[2 * hf], z], [z, w[2 * hf + 1]]]) for hf in range(2)]
    halves = [[], []]
    for d in range(N_DIR):
        for w in (w_r[d], w_i[d]):
            for hf, m in enumerate(dense(w)):
                halves[hf].append(m)
    return jnp.stack([jnp.concatenate(h, axis=1) for h in halves]).astype(BF16)


def _lru_vectors(lam, b_r, b_i):
    rows = [lam[0], lam[1], b_r[0], b_i[0], b_r[1], b_i[1], jnp.zeros_like(lam[0]), jnp.zeros_like(lam[0])]
    v = jnp.stack(rows)
    return jnp.stack([v[:, :LANES], v[:, LANES:]])


def kernel(x, c, ctx, c_ctx, w_ada, b_ada, g_norm, w_in, g_qk, lru_conv_w, lru_conv_b, lru_w_r, lru_b_r, lru_w_i, lru_b_i, lru_lambda, dn_conv_w, dn_a_log, dn_dt_bias, g_dn_out, g_group, w_out, w_ffn_in, w_ffn_out):
    depth = w_ada.shape[0]
    bsz, n, d = x.shape
    m = ctx.shape[1]
    assert m == TM and d == D_MODEL and bsz == 2 and n % (ATT_STEPS_PER_BODY * ATT_TK) == 0

    cvec = jnp.concatenate([c, c_ctx[None, :], jnp.zeros((SUBLANES - bsz - 1, d), F32)], axis=0)
    mods = _modulation(cvec, w_ada, b_ada)
    cosf, sinf = _rope_tables(m, n)
    ones128 = _block_diag_ones(LANES)
    ones256 = _block_diag_ones(DN_WIDTH)
    tri = _chunk_triangles(TM)
    xg = _head_spread(0)
    xb = _head_spread(N_DIR * DN_HEADS)

    stream = (ctx, x, 1, m + n)
    for l in range(depth):
        last = l == depth - 1
        off = 1 if last else 0
        mod4 = mods[l].reshape(SUBLANES, 6, 1, d)
        wl = w_in[l]
        w_in_p = jnp.concatenate([wl[:, 0:768], wl[:, 1280:2048], wl[:, 1024:1280], wl[:, 768:1024],
                                  wl[:, 2048:2320], jnp.zeros((d, IN_COLS_PAD - 2320), F32)], axis=1).astype(BF16)
        cw = jnp.concatenate([dn_conv_w[l], lru_conv_w[l]], axis=1)
        cb = jnp.concatenate([jnp.zeros((3 * DN_WIDTH,), F32), lru_conv_b[l]])[None, :]
        pad = jnp.zeros((LANES - N_DIR * DN_HEADS,), F32)
        dnv = jnp.stack([jnp.concatenate([dn_a_log[l].reshape(-1), pad]),
                         jnp.concatenate([dn_dt_bias[l].reshape(-1), pad])])
        qt, k, vt, gz, xr, qk, pk, ge, rws = _in_proj(
            stream, mod4, g_norm[l, 0][None, :], w_in_p, jnp.tile(g_qk[l], (1, 2)), cosf, sinf, ones128,
            cw, cb, dnv, ones256, tri, xg, xb)
        key_norm_bound = 1.01 * HEAD_DIM ** 0.5 * jnp.max(jnp.abs(g_qk[l, 1]))
        score_bound = key_norm_bound * (LOG2E * jnp.max(jnp.abs(g_qk[l, 0])))
        att = _attention(qt, k, vt, key_norm_bound, score_bound, m, off)

        y = _lru(xr, _lru_gate_weights(lru_w_r[l], lru_w_i[l]),
                 _lru_vectors(lru_lambda[l], lru_b_r[l], lru_b_i[l]), m)
        o0, o1 = _deltanet(qk, pk, ge, rws, m)

        x1 = _out_proj(stream, att, y, gz, o0, o1, g_group[l][None, :], jnp.tile(g_dn_out[l], DN_HEADS)[None, :],
                       ones256, w_out[l].astype(BF16), mod4, g_norm[l, 1][None, :], off)
        xs = _ffn(x1, mod4, g_norm[l, 2][None, :], g_norm[l, 3][None, :],
                  w_ffn_in[l].astype(BF16), w_ffn_out[l].astype(BF16), off)
        stream = (xs, xs, 0, m + n)
    return xs
```

```python
import functools

import jax
import jax.numpy as jnp
from jax import lax
from jax.experimental import pallas as pl
from jax.experimental.pallas import tpu as pltpu

F32 = jnp.float32
BF16 = jnp.bfloat16

D_MODEL = 1024
GRID_W = 64
EPS = 1e-6
ATT_GROUP = 4
ATT_KV_HEADS = 2
HEAD_DIM = 64
ATT_WIDTH = 512
ROPE_THETA = 10000.0
LRU_WIDTH = 256
LRU_BLOCK_DIM = 64
LRU_C = 8.0
DN_HEADS = 4
DN_WIDTH = 256
DN_CHUNK = 64
DN_BASE = 8
DN_BLOCK = 4 * DN_CHUNK
DN_ROWS = 2 * (DN_BLOCK // DN_CHUNK)
DN_PACK = 5 * DN_WIDTH
N_DIR = 2
FFN_HIDDEN = 2816

LANES = 128
SUBLANES = 8
TM = 256
ATT_TK = 1024
ATT_STEPS_PER_BODY = 4
ATT_VROWS = 80
LOG2E = 1.4426950408889634
ATT_FIXED_SHIFT_LIMIT = 48.0
QKV_COLS = 768
CONV_COLS = 1024
REST_COLS = 1664
IN_COLS_PAD = QKV_COLS + REST_COLS
VMEM_LIMIT = 56 * 1024 * 1024


def _cparams(*sem):
    return pltpu.CompilerParams(dimension_semantics=sem, vmem_limit_bytes=VMEM_LIMIT)


def _rms(x, gain):
    ms = jnp.mean(x * x, axis=-1, keepdims=True)
    return x * lax.rsqrt(ms + EPS) * gain


def _seg_sumsq(x, ones_bd):
    sq = x * x
    hi = sq.astype(BF16)
    lo = (sq - hi.astype(F32)).astype(BF16)
    return (jnp.dot(hi, ones_bd, preferred_element_type=F32)
            + jnp.dot(lo, ones_bd, preferred_element_type=F32))


def _silu(x):
    return x * jax.nn.sigmoid(x)


def _softplus(x):
    return jnp.maximum(x, 0.0) + jnp.log1p(jnp.exp(-jnp.abs(x)))


def _mod_row(b, i):
    return jnp.where(i == 0, 2, b)


def _mod_kernel(c_ref, w_ref, b_ref, o_ref):
    s = _silu(c_ref[...])
    o_ref[0] = jnp.dot(s.astype(BF16), w_ref[0].astype(BF16), preferred_element_type=F32) + b_ref[0]


def _modulation(cvec, w_ada, b_ada):
    depth = w_ada.shape[0]
    d = D_MODEL
    return pl.pallas_call(
        _mod_kernel,
        grid=(depth, 6),
        in_specs=[pl.BlockSpec((SUBLANES, d), lambda l, j: (0, 0)),
                  pl.BlockSpec((1, d, d), lambda l, j: (l, 0, j)),
                  pl.BlockSpec((1, 1, d), lambda l, j: (l, 0, j))],
        out_specs=pl.BlockSpec((1, SUBLANES, d), lambda l, j: (l, 0, j)),
        out_shape=jax.ShapeDtypeStruct((depth, SUBLANES, 6 * d), F32),
        compiler_params=_cparams("arbitrary", "arbitrary"),
        name="modulation",
    )(cvec, w_ada, b_ada.reshape(depth, 1, 6 * d))


def _split3(x):
    hi = x.astype(BF16)
    r = x - hi.astype(F32)
    mid = r.astype(BF16)
    lo = (r - mid.astype(F32)).astype(BF16)
    return hi, mid, lo


def _dot_parts(parts, mat, left=False):
    out = None
    for p in parts:
        t = (jnp.dot(mat, p, preferred_element_type=F32) if left else jnp.dot(p, mat, preferred_element_type=F32))
        out = t if out is None else out + t
    return out


def _in_kernel(x_ref, c_ref, xp_ref, xn_ref, sh_ref, sc_ref, gn_ref, w_ref, gqk_ref, cos_ref, sin_ref, ones_ref,
               cw_ref, cb_ref, dnv_ref, ones2_ref, tri_ref, xg_ref, xb_ref,
               qt_ref, k_ref, vt_ref, gz_ref, xr_ref, qk_ref, pk_ref, ge_ref, rows_ref, *, split):
    i = pl.program_id(1)
    nb = pl.num_programs(1)
    c_, w_ = DN_CHUNK, DN_WIDTH
    x = jnp.where(i == 0, c_ref[0], x_ref[0]) if split else x_ref[0]
    xe = jnp.concatenate([x, xp_ref[0], xn_ref[0]], axis=0)
    hb = (_rms(xe, gn_ref[...]) * (1.0 + sc_ref[0, 0]) + sh_ref[0, 0]).astype(BF16)
    hb_cur = hb[0:TM]
    qkv = jnp.dot(hb_cur, w_ref[:, 0:QKV_COLS], preferred_element_type=F32)
    cosf = cos_ref[...]
    sinf = sin_ref[...]
    lane = lax.broadcasted_iota(jnp.int32, (TM, LANES), 1)
    first_half = (lane & 32) == 0
    ones_bd = ones_ref[...]

    def norm_rope(slab, gain):
        ss = _seg_sumsq(slab, ones_bd)
        y = slab * lax.rsqrt(ss * (1.0 / HEAD_DIM) + EPS) * gain
        partner = jnp.where(first_half, pltpu.roll(y, 96, axis=1), pltpu.roll(y, 32, axis=1))
        return y * cosf + partner * sinf

    for s in range(ATT_WIDTH // LANES):
        qs = norm_rope(qkv[:, LANES * s:LANES * (s + 1)], gqk_ref[0:1, :]) * (LOG2E * HEAD_DIM ** -0.5)
        qst = qs.T.astype(BF16)
        for e in range(2):
            g, hh = divmod(2 * s + e, ATT_GROUP)
            qt_ref[0, g, 0, 0:HEAD_DIM, TM * hh:TM * (hh + 1)] = qst[HEAD_DIM * e:HEAD_DIM * (e + 1), :]
    for g in range(ATT_KV_HEADS):
        qt_ref[0, g, 0, HEAD_DIM:, :] = jnp.zeros((LANES - HEAD_DIM, ATT_GROUP * TM), BF16)
    kk = norm_rope(qkv[:, 512:640], gqk_ref[1:2, :])
    one_lane = jnp.where(lane == HEAD_DIM, 1.0, 0.0)
    vvt = qkv[:, 640:768].T.astype(BF16)
    row = lax.broadcasted_iota(jnp.int32, (ATT_VROWS - HEAD_DIM, TM), 0)
    ones_rows = jnp.where(row == 0, 1.0, 0.0).astype(BF16)
    for g in range(ATT_KV_HEADS):
        kg = kk if g == 0 else pltpu.roll(kk, HEAD_DIM, axis=1)
        k_ref[0, g] = jnp.where(lane < HEAD_DIM, kg, one_lane).astype(BF16)
        vt_ref[0, g, 0, 0:HEAD_DIM, :] = vvt[HEAD_DIM * g:HEAD_DIM * (g + 1), :]
        vt_ref[0, g, 0, HEAD_DIM:, :] = ones_rows

    tail = jnp.dot(hb_cur, w_ref[:, QKV_COLS + CONV_COLS:], preferred_element_type=F32)
    gz_ref[0] = tail[:, 0:LRU_WIDTH + DN_WIDTH]
    ab = tail[:, LRU_WIDTH + DN_WIDTH:]

    cin = jnp.dot(hb, w_ref[:, QKV_COLS:QKV_COLS + CONV_COLS], preferred_element_type=F32)
    cur = cin[0:TM]
    prv = jnp.where(i >= 2, cin[TM:TM + SUBLANES], 0.0)
    nxt = jnp.where(jnp.logical_and(i >= 1, i < nb - 1), cin[TM + SUBLANES:], 0.0)
    trow = lax.broadcasted_iota(jnp.int32, cur.shape, 0)
    x_m1 = jnp.where(trow == 0, prv[SUBLANES - 1:SUBLANES, :], pltpu.roll(cur, 1, axis=0))
    x_p1 = jnp.where(trow == TM - 1, nxt[0:1, :], pltpu.roll(cur, TM - 1, axis=0))
    x_p2 = jnp.where(trow == TM - 2, nxt[0:1, :],
                     jnp.where(trow == TM - 1, nxt[1:2, :], pltpu.roll(cur, TM - 2, axis=0)))
    conv = (cb_ref[...] + x_m1 * cw_ref[0:1, :] + cur * cw_ref[1:2, :]
            + x_p1 * cw_ref[2:3, :] + x_p2 * cw_ref[3:4, :])
    xr_ref[0] = conv[:, 3 * w_:]
    act = _silu(conv[:, 0:3 * w_])
    qq = act[:, 0:w_]
    kq = act[:, w_:2 * w_]
    vv = act[:, 2 * w_:3 * w_]
    ones2 = ones2_ref[...]
    qn = qq * lax.rsqrt(_seg_sumsq(qq, ones2) + EPS) * (c_ ** -0.5)
    kn = kq * lax.rsqrt(_seg_sumsq(kq, ones2) + EPS)
    qk_ref[0, :, 0:w_] = qn.astype(BF16)
    qk_ref[0, :, w_:2 * w_] = kn.astype(BF16)

    g = jnp.where(lane < N_DIR * DN_HEADS, -jnp.exp(dnv_ref[0:1, :]) * _softplus(ab + dnv_ref[1:2, :]), 0.0)
    beta = jax.nn.sigmoid(ab)
    g_parts = _split3(g)
    gc = jnp.where(lane < DN_HEADS, _dot_parts(g_parts, tri_ref[0], left=True),
                   _dot_parts(g_parts, tri_ref[1], left=True))
    tot = _dot_parts(g_parts, tri_ref[2], left=True)
    ge = _dot_parts(_split3(gc), xg_ref[...])
    te = _dot_parts(_split3(tot), xg_ref[...])
    be = _dot_parts(_split3(beta)[:2], xb_ref[...])
    ge_ref[0] = ge
    crow = lax.broadcasted_iota(jnp.int32, (c_, w_), 0)
    diag = crow == (lax.broadcasted_iota(jnp.int32, (c_, w_), 1) & (c_ - 1))
    for d in range(N_DIR):
        ge_d = ge[:, w_ * d:w_ * (d + 1)]
        te_d = te[:, w_ * d:w_ * (d + 1)]
        be_d = be[:, w_ * d:w_ * (d + 1)]
        eg = jnp.exp(ge_d)
        kb = kn * be_d
        pk_ref[0, d, :, 0:w_] = kb.astype(BF16)
        pk_ref[0, d, :, w_:2 * w_] = (vv * be_d).astype(BF16)
        pk_ref[0, d, :, 2 * w_:3 * w_] = (kb * eg).astype(BF16)
        pk_ref[0, d, :, 3 * w_:4 * w_] = (qn * eg).astype(BF16)
        pk_ref[0, d, :, 4 * w_:5 * w_] = (kn * jnp.exp(te_d - ge_d)).astype(BF16)
        for pr in range(TM // DN_BLOCK):
            for c in range(DN_BLOCK // c_):
                r0 = DN_BLOCK * pr + c_ * c
                rows_ref[0, d, pr, c:c + 1, :] = jnp.sum(jnp.where(diag, ge_d[r0:r0 + c_, :], 0.0), axis=0,
                                                         keepdims=True)
                rows_ref[0, d, pr, DN_ROWS // 2 + c:DN_ROWS // 2 + c + 1, :] = te_d[r0:r0 + 1, :]


def _in_proj(stream, mod4, gn0, w_in_p, gqk2, cosf, sinf, ones128, cw, cb, dnv, ones256, tri, xg, xb):
    c_src, x_src, lat_off, t = stream
    b, rows_src, d = x_src.shape
    nb = t // TM
    r8 = TM // SUBLANES
    ppb = TM // DN_BLOCK

    def const(shape):
        return pl.BlockSpec(shape, lambda bi, i: (0,) * len(shape))

    def halo(step):
        return pl.BlockSpec((1, SUBLANES, d), lambda bi, i: (
            bi, jnp.clip((i - lat_off) * r8 + step, 0, rows_src // SUBLANES - 1), 0))

    return pl.pallas_call(
        functools.partial(_in_kernel, split=lat_off > 0),
        grid=(b, nb),
        in_specs=[pl.BlockSpec((1, TM, d), lambda bi, i: (bi, jnp.maximum(i - lat_off, 0), 0)),
                  pl.BlockSpec((1, TM, d), lambda bi, i: (bi, 0, 0)),
                  halo(-1), halo(r8),
                  pl.BlockSpec((1, 1, 1, d), lambda bi, i: (_mod_row(bi, i), 0, 0, 0)),
                  pl.BlockSpec((1, 1, 1, d), lambda bi, i: (_mod_row(bi, i), 1, 0, 0)),
                  const((1, d)), const((d, IN_COLS_PAD)), const((2, LANES)),
                  pl.BlockSpec((TM, LANES), lambda bi, i: (i, 0)),
                  pl.BlockSpec((TM, LANES), lambda bi, i: (i, 0)),
                  const((LANES, LANES)),
                  const((4, CONV_COLS)), const((1, CONV_COLS)), const((2, LANES)), const((DN_WIDTH, DN_WIDTH)),
                  const((3, TM, TM)), const((LANES, N_DIR * DN_WIDTH)), const((LANES, N_DIR * DN_WIDTH))],
        out_specs=[pl.BlockSpec((1, ATT_KV_HEADS, 1, LANES, ATT_GROUP * TM), lambda bi, i: (bi, 0, i, 0, 0)),
                   pl.BlockSpec((1, ATT_KV_HEADS, TM, LANES), lambda bi, i: (bi, 0, i, 0)),
                   pl.BlockSpec((1, ATT_KV_HEADS, 1, ATT_VROWS, TM), lambda bi, i: (bi, 0, i, 0, 0)),
                   pl.BlockSpec((1, TM, LRU_WIDTH + DN_WIDTH), lambda bi, i: (bi, i, 0)),
                   pl.BlockSpec((1, TM, LRU_WIDTH), lambda bi, i: (bi, i, 0)),
                   pl.BlockSpec((1, TM, 2 * DN_WIDTH), lambda bi, i: (bi, i, 0)),
                   pl.BlockSpec((1, N_DIR, TM, DN_PACK), lambda bi, i: (bi, 0, i, 0)),
                   pl.BlockSpec((1, TM, N_DIR * DN_WIDTH), lambda bi, i: (bi, i, 0)),
                   pl.BlockSpec((1, N_DIR, ppb, DN_ROWS, DN_WIDTH), lambda bi, i: (bi, 0, i, 0, 0))],
        out_shape=[jax.ShapeDtypeStruct((b, ATT_KV_HEADS, nb, LANES, ATT_GROUP * TM), BF16),
                   jax.ShapeDtypeStruct((b, ATT_KV_HEADS, t, LANES), BF16),
                   jax.ShapeDtypeStruct((b, ATT_KV_HEADS, nb, ATT_VROWS, TM), BF16),
                   jax.ShapeDtypeStruct((b, t, LRU_WIDTH + DN_WIDTH), F32),
                   jax.ShapeDtypeStruct((b, t, LRU_WIDTH), F32),
                   jax.ShapeDtypeStruct((b, t, 2 * DN_WIDTH), BF16),
                   jax.ShapeDtypeStruct((b, N_DIR, t, DN_PACK), BF16),
                   jax.ShapeDtypeStruct((b, t, N_DIR * DN_WIDTH), F32),
                   jax.ShapeDtypeStruct((b, N_DIR, t // DN_BLOCK, DN_ROWS, DN_WIDTH), F32)],
        compiler_params=_cparams("arbitrary", "arbitrary"),
        name="in_proj",
    )(x_src, c_src, x_src, x_src, mod4, mod4, gn0, w_in_p, gqk2, cosf, sinf, ones128,
      cw, cb, dnv, ones256, tri, xg, xb)


def _att_kernel(qt_ref, k_ref, vt_ref, kn_ref, o_ref, m_ref, acc_ref, s_ref, cmax_ref, *, ctx, n_lat, q_off):
    del kn_ref
    i = pl.program_id(2) + q_off
    qt = qt_ref[0, 0, 0]
    nsub = ATT_TK // TM

    def scores(start, size):
        return jnp.dot(k_ref[0, 0, pl.ds(start, size), :], qt, preferred_element_type=F32)

    def pv(pt, blk0, nblk):
        out = None
        for j in range(nblk):
            t = jnp.dot(vt_ref[0, 0, blk0 + j], pt[j * TM:(j + 1) * TM, :], preferred_element_type=F32)
            out = t if out is None else out + t
        return out

    st = scores(0, ctx)
    m0 = jnp.max(st, axis=0, keepdims=True)
    m_ref[...] = m0
    acc_ref[...] = pv(jnp.exp2(st - m0).astype(BF16), 0, ctx // TM)

    n_chunks = n_lat // ATT_TK

    def step(j, cur, nxt):
        jn = jnp.minimum(j + 1, n_chunks - 1)
        sn = scores(pl.multiple_of(ctx + jn * ATT_TK, TM), ATT_TK)
        s_ref[nxt] = sn
        cmax_ref[nxt] = jnp.max(sn, axis=0, keepdims=True)
        st = s_ref[cur]
        m_prev = m_ref[...]
        m_new = jnp.maximum(m_prev, cmax_ref[cur])
        alpha = jnp.exp2(m_prev - m_new)
        pt = jnp.exp2(st - m_new).astype(BF16)
        acc_ref[...] = alpha * acc_ref[...] + pv(pt, ctx // TM + j * nsub, nsub)
        m_ref[...] = m_new

    def body(jj, carry):
        step(2 * jj, 0, 1)
        step(2 * jj + 1, 1, 0)
        return carry

    @pl.when(i > 0)
    def _():
        s0 = scores(ctx, ATT_TK)
        s_ref[0] = s0
        cmax_ref[0] = jnp.max(s0, axis=0, keepdims=True)
        lax.fori_loop(0, n_chunks // 2, body, 0)

    acc = acc_ref[...]
    out_t = acc[0:HEAD_DIM] / acc[HEAD_DIM:HEAD_DIM + 1]
    for h in range(ATT_GROUP):
        o_ref[0, :, HEAD_DIM * h:HEAD_DIM * (h + 1)] = out_t[:, TM * h:TM * (h + 1)].T


def _att_fixed_kernel(qt_ref, k_ref, vt_ref, kn_ref, o_ref, acc_ref, p_ref, qa_ref, *, ctx, n_lat, q_off):
    i = pl.program_id(2) + q_off
    nsub = ATT_TK // TM
    n_chunks = n_lat // ATT_TK
    qt = qt_ref[0, 0, 0]
    q32 = qt[0:HEAD_DIM, :].astype(F32)
    r = jnp.sqrt(jnp.sum(q32 * q32, axis=0, keepdims=True)) * kn_ref[...]
    tile_row = lax.broadcasted_iota(jnp.int32, (2 * SUBLANES, ATT_GROUP * TM), 0)
    qa_ref[...] = qt
    qa_ref[HEAD_DIM:HEAD_DIM + 2 * SUBLANES, :] = jnp.where(tile_row == 0, -r, 0.0).astype(BF16)

    def probs(start, size):
        s = jnp.dot(k_ref[0, 0, pl.ds(start, size), :], qa_ref[...], preferred_element_type=F32)
        return jnp.exp2(s).astype(BF16)

    def pv(pt, blk0, nblk):
        out = None
        for j in range(nblk):
            t = jnp.dot(vt_ref[0, 0, blk0 + j], pt[j * TM:(j + 1) * TM, :], preferred_element_type=F32)
            out = t if out is None else out + t
        return out

    def step(j, cur, nxt, prefetch=True):
        if prefetch:
            p_ref[nxt] = probs(pl.multiple_of(ctx + (j + 1) * ATT_TK, TM), ATT_TK)
        acc_ref[...] += pv(p_ref[cur], ctx // TM + j * nsub, nsub)

    def body(jj, carry, last=False):
        for u in range(ATT_STEPS_PER_BODY):
            step(ATT_STEPS_PER_BODY * jj + u, (u + 1) % 2, u % 2,
                 prefetch=not (last and u == ATT_STEPS_PER_BODY - 1))
        return carry

    @pl.when(i == 0)
    def _():
        acc_ref[...] = pv(probs(0, ctx), 0, ctx // TM)

    @pl.when(i > 0)
    def _():
        p_ctx = probs(0, ctx)
        p_ref[1] = probs(ctx, ATT_TK)
        acc_ref[...] = pv(p_ctx, 0, ctx // TM)
        n_bodies = n_chunks // ATT_STEPS_PER_BODY
        lax.fori_loop(0, n_bodies - 1, body, 0)
        body(n_bodies - 1, 0, last=True)

    acc = acc_ref[...]
    out_t = acc[0:HEAD_DIM] / acc[HEAD_DIM:HEAD_DIM + 1]
    for h in range(ATT_GROUP):
        o_ref[0, :, HEAD_DIM * h:HEAD_DIM * (h + 1)] = out_t[:, TM * h:TM * (h + 1)].T


def _attention(qt, k, vt, key_norm_bound, score_bound, ctx, q_off):
    b, _, t, _ = k.shape
    nb = t // TM
    nq = nb - q_off
    gw = ATT_GROUP * HEAD_DIM
    cols = ATT_GROUP * TM

    def call(body, scratch):
        return pl.pallas_call(
            functools.partial(body, ctx=ctx, n_lat=t - ctx, q_off=q_off),
            grid=(b, ATT_KV_HEADS, nq),
            in_specs=[pl.BlockSpec((1, 1, 1, LANES, cols), lambda bi, g, i: (bi, g, i + q_off, 0, 0)),
                      pl.BlockSpec((1, 1, t, LANES), lambda bi, g, i: (bi, g, 0, 0)),
                      pl.BlockSpec((1, 1, nb, ATT_VROWS, TM), lambda bi, g, i: (bi, g, 0, 0, 0)),
                      pl.BlockSpec((1, 1), lambda bi, g, i: (0, 0))],
            out_specs=pl.BlockSpec((1, TM, gw), lambda bi, g, i: (bi, i, g)),
            out_shape=jax.ShapeDtypeStruct((b, nq * TM, ATT_WIDTH), F32),
            scratch_shapes=scratch,
            compiler_params=_cparams("arbitrary", "arbitrary", "arbitrary"),
            name="attention",
        )

    online = call(_att_kernel, [pltpu.VMEM((1, cols), F32), pltpu.VMEM((ATT_VROWS, cols), F32),
                                pltpu.VMEM((2, ATT_TK, cols), F32), pltpu.VMEM((2, 1, cols), F32)])
    fixed = call(_att_fixed_kernel, [pltpu.VMEM((ATT_VROWS, cols), F32), pltpu.VMEM((2, ATT_TK, cols), BF16),
                                     pltpu.VMEM((LANES, cols), BF16)])
    return lax.cond(score_bound <= ATT_FIXED_SHIFT_LIMIT, fixed, online, qt, k, vt,
                    jnp.reshape(key_norm_bound, (1, 1)))


def _lru_kernel(xr_ref, wg_ref, vec_ref, y_ref, a0, u0, a1, u1, *, ctx, n):
    nseg = SUBLANES
    vec = vec_ref[0]
    a_refs = (a0, a1)
    u_refs = (u0, u1)

    def gates(x, d):
        g = jnp.dot(x.astype(BF16), wg_ref[0, :, 2 * LANES * d:2 * LANES * (d + 1)],
                    preferred_element_type=F32)
        r = 0.5 * jnp.tanh(0.5 * (g[:, :LANES] + vec[2 + 2 * d:3 + 2 * d])) + 0.5
        ig = 0.5 * jnp.tanh(0.5 * (g[:, LANES:] + vec[3 + 2 * d:4 + 2 * d])) + 0.5
        log_a = -LRU_C * r * _softplus(-vec[d:d + 1])
        a = jnp.exp(log_a)
        mult = jnp.sqrt(-jnp.tanh(log_a) * (a * a + 1.0))
        return a, mult * (ig * x)

    def run(row0, seg, base, init):
        pitch = seg + SUBLANES
        for j in range(nseg):
            x = xr_ref[0, row0 + j * seg:row0 + (j + 1) * seg, :]
            for d in range(N_DIR):
                a, u = gates(x, d)
                a_refs[d][base + j * pitch:base + j * pitch + seg, :] = a
                u_refs[d][base + j * pitch:base + j * pitch + seg, :] = u

        def step(t, carry):
            out = []
            for d in range(N_DIR):
                h, p = carry[2 * d], carry[2 * d + 1]
                r = base + (t if d == 0 else seg - 1 - t)
                idx = pl.ds(r, nseg, stride=pitch)
                a = a_refs[d][idx, :]
                h = a * h + u_refs[d][idx, :]
                p = a * p
                u_refs[d][idx, :] = h
                a_refs[d][idx, :] = p
                out += [h, p]
            return tuple(out)

        z = jnp.zeros((nseg, LANES), F32)
        o = jnp.ones((nseg, LANES), F32)
        fin = lax.fori_loop(0, seg, step, (z, o, z, o), unroll=8)
        carries, finals = [], []
        for d in range(N_DIR):
            hl, pl_ = fin[2 * d], fin[2 * d + 1]
            order = list(range(nseg)) if d == 0 else list(range(nseg - 1, -1, -1))
            c = init[d]
            cs = {}
            for j in order:
                cs[j] = c
                c = hl[j:j + 1] + pl_[j:j + 1] * c
            carries.append(cs)
            finals.append(c)
        for j in range(nseg):
            sl = slice(base + j * pitch, base + j * pitch + seg)
            y = (u0[sl, :] + a0[sl, :] * carries[0][j]) + (u1[sl, :] + a1[sl, :] * carries[1][j])
            y_ref[0, row0 + j * seg:row0 + (j + 1) * seg, :] = y
        return finals

    zero = jnp.zeros((1, LANES), F32)
    seg_c = ctx // nseg
    fin_c = run(0, seg_c, 0, [zero, zero])
    run(ctx, n // nseg, nseg * (seg_c + SUBLANES), fin_c)


def _lru(xr, wg, vec, ctx):
    b, t, _ = xr.shape
    n = t - ctx
    rows = SUBLANES * (ctx // SUBLANES + SUBLANES) + SUBLANES * (n // SUBLANES + SUBLANES)
    kern = functools.partial(_lru_kernel, ctx=ctx, n=n)
    return pl.pallas_call(
        kern,
        grid=(b, LRU_WIDTH // LANES),
        in_specs=[pl.BlockSpec((1, t, LANES), lambda bi, hf: (bi, 0, hf)),
                  pl.BlockSpec((1, LANES, 4 * LANES), lambda bi, hf: (hf, 0, 0)),
                  pl.BlockSpec((1, SUBLANES, LANES), lambda bi, hf: (hf, 0, 0))],
        out_specs=pl.BlockSpec((1, t, LANES), lambda bi, hf: (bi, 0, hf)),
        out_shape=jax.ShapeDtypeStruct((b, t, LRU_WIDTH), F32),
        scratch_shapes=[pltpu.VMEM((rows, LANES), F32)] * 4,
        compiler_params=_cparams("arbitrary", "arbitrary"),
        name="rglru",
    )(xr, wg, vec)


def _dn_kernel(qkf_ref, pkf_ref, gef_ref, rwf_ref, qkb_ref, pkb_ref, geb_ref, rwb_ref, o0_ref, o1_ref, s_ref):
    c_, w_ = DN_CHUNK, DN_WIDTH
    n_batch = o0_ref.shape[0]

    @pl.when(pl.program_id(0) == 0)
    def _():
        s_ref[...] = jnp.zeros_like(s_ref)

    row = lax.broadcasted_iota(jnp.int32, (c_, w_), 0)
    lane = lax.broadcasted_iota(jnp.int32, (c_, w_), 1)
    lj = lane & (c_ - 1)
    head_masks = [(lane >> 6) == h for h in range(DN_HEADS)]
    eye_f = jnp.where(row == lj, 1.0, 0.0)
    shift = DN_BASE.bit_length() - 1
    same_base = (row >> shift) == (lj >> shift)
    off_blk = ({}, {})
    size = DN_BASE
    while size < c_:
        shift = size.bit_length() - 1
        bi, bj = row >> shift, lj >> shift
        off_blk[0][size] = jnp.logical_and(bi == bj + 1, (bi & 1) == 1)
        off_blk[1][size] = jnp.logical_and(bj == bi + 1, (bj & 1) == 1)
        size *= 2
    bd_mask = ((lax.broadcasted_iota(jnp.int32, (w_, w_), 0) >> 6)
               == (lax.broadcasted_iota(jnp.int32, (w_, w_), 1) >> 6))
    nt = (((1,), (1,)), ((), ()))
    tn = (((0,), (0,)), ((), ()))

    def bd(x):
        xb = x.astype(BF16)
        zero = jnp.zeros_like(xb)
        return jnp.concatenate([jnp.where(hm, xb, zero) for hm in head_masks], axis=0)

    def bdot(a, b, dims=None):
        if dims is None:
            return jnp.dot(a.astype(BF16), b, preferred_element_type=F32)
        return lax.dot_general(a.astype(BF16), b, dims, preferred_element_type=F32)

    per_block = DN_BLOCK // c_
    in_refs = ((qkf_ref, pkf_ref, gef_ref, rwf_ref), (qkb_ref, pkb_ref, geb_ref, rwb_ref))
    o_refs = (o0_ref, o1_ref)
    probs = []
    for d in range(N_DIR):
        incl = (lj <= row) if d == 0 else (lj >= row)
        strict = (lj < row) if d == 0 else (lj > row)
        qk_ref, pk_ref, ge_ref, rw_ref = in_refs[d]
        for bi, step in [(bi, step) for bi in range(n_batch) for step in range(per_block)]:
            c = step if d == 0 else per_block - 1 - step
            rows = slice(c_ * c, c_ * (c + 1))
            decay = jnp.where(incl, jnp.exp(jnp.where(incl, ge_ref[bi, rows, :] - rw_ref[bi, 0, 0, c:c + 1, :], 0.0)),
                              0.0)
            probs.append(dict(d=d, bi=bi, step=step, rows=rows, incl=incl, strict=strict, decay=decay,
                              st=N_DIR * bi + d,
                              kbq=jnp.concatenate([pk_ref[bi, 0, rows, 0:w_], qk_ref[bi, rows, 0:w_]], axis=0),
                              k=qk_ref[bi, rows, w_:2 * w_], vb=pk_ref[bi, 0, rows, w_:2 * w_],
                              kbe=pk_ref[bi, 0, rows, 2 * w_:3 * w_], qd=pk_ref[bi, 0, rows, 3 * w_:4 * w_],
                              kt=pk_ref[bi, 0, rows, 4 * w_:5 * w_],
                              gt=jnp.exp(rw_ref[bi, 0, 0, per_block + c:per_block + c + 1, :])))
    for p in probs:
        a1 = bdot(p["kbq"], bd(p["k"]), nt)
        p["lower"] = jnp.where(p["strict"], a1[:c_] * p["decay"], 0.0)
        p["attn"] = jnp.where(p["incl"], a1[c_:] * p["decay"], 0.0)
        p["pw"] = jnp.where(same_base, p["lower"], 0.0)
        p["tm"] = eye_f - p["pw"]
    for p in probs:
        p["pw"] = bdot(p["pw"], bd(p["pw"]))
    for p in probs:
        t = bdot(jnp.concatenate([p["tm"], p["pw"]], axis=0), bd(p["pw"]))
        p["tm"] = p["tm"] + t[:c_]
        p["pw"] = t[c_:]
    for p in probs:
        p["tm"] = p["tm"] + bdot(p["tm"], bd(p["pw"]))
    size = DN_BASE
    while size < c_:
        for p in probs:
            p["tmp"] = bdot(jnp.where(off_blk[p["d"]][size], p["lower"], 0.0), bd(p["tm"]))
        for p in probs:
            p["tm"] = p["tm"] - bdot(p["tm"], bd(p["tmp"]))
        size *= 2
    for p in probs:
        sol = bdot(p["tm"], jnp.concatenate([bd(p["vb"]), bd(p["kbe"])], axis=1))
        p["u"] = sol[:, :w_]
        p["wqd"] = jnp.concatenate([sol[:, w_:].astype(BF16), p["qd"]], axis=0)
    for step in range(per_block):
        cur = [p for p in probs if p["step"] == step]
        for p in cur:
            p["s_old"] = s_ref[p["st"]]
            p["a2"] = bdot(p["wqd"], p["s_old"].astype(BF16))
        for p in cur:
            p["v_new"] = p["u"] - p["a2"][:c_]
        for p in cur:
            o_refs[p["d"]][p["bi"], p["rows"], :] = p["a2"][c_:] + bdot(p["attn"], bd(p["v_new"]))
        for p in cur:
            upd = bdot(p["kt"], p["v_new"].astype(BF16), tn)
            s_ref[p["st"]] = p["s_old"] * p["gt"] + jnp.where(bd_mask, upd, 0.0)


def _deltanet(qk, pk, ge, rws, ctx):
    b, t, _ = qk.shape
    np_ = t // DN_BLOCK
    npc = ctx // DN_BLOCK

    def bwd(p):
        return jnp.where(p < npc, npc - 1 - p, np_ - 1 - (p - npc))

    def specs(d, pos):
        return [pl.BlockSpec((b, DN_BLOCK, 2 * DN_WIDTH), lambda p: (0, pos(p), 0)),
                pl.BlockSpec((b, 1, DN_BLOCK, DN_PACK), lambda p: (0, d, pos(p), 0)),
                pl.BlockSpec((b, DN_BLOCK, DN_WIDTH), lambda p: (0, pos(p), d)),
                pl.BlockSpec((b, 1, 1, DN_ROWS, DN_WIDTH), lambda p: (0, d, pos(p), 0, 0))]

    o_shape = jax.ShapeDtypeStruct((b, t, DN_WIDTH), F32)
    return pl.pallas_call(
        _dn_kernel,
        grid=(np_,),
        in_specs=specs(0, lambda p: p) + specs(1, bwd),
        out_specs=[pl.BlockSpec((b, DN_BLOCK, DN_WIDTH), lambda p: (0, p, 0)),
                   pl.BlockSpec((b, DN_BLOCK, DN_WIDTH), lambda p: (0, bwd(p), 0))],
        out_shape=[o_shape, o_shape],
        scratch_shapes=[pltpu.VMEM((b * N_DIR, DN_WIDTH, DN_WIDTH), F32)],
        compiler_params=_cparams("arbitrary"),
        name="deltanet",
    )(qk, pk, ge, rws, qk, pk, ge, rws)


def _out_kernel(x_ref, c_ref, att_ref, y_ref, lg_ref, o0_ref, o1_ref, dz_ref, gg_ref, gdn_ref, ones_ref, w_ref,
                gt_ref, gn_ref, xo_ref, *, off, split):
    gg = gg_ref[...]
    a_n = _rms(att_ref[0], gg[:, 0:ATT_WIDTH]).astype(BF16)
    lg = lg_ref[0]
    gelu = lg * (0.5 * (1.0 + jnp.tanh(0.7978845608028654 * (lg + 0.044715 * (lg * lg * lg)))))
    l_n = _rms(gelu * y_ref[0], gg[:, ATT_WIDTH:]).astype(BF16)
    o = o0_ref[0] + o1_ref[0]
    ss = _seg_sumsq(o, ones_ref[...])
    dn = ((o * lax.rsqrt(ss * (1.0 / DN_CHUNK) + EPS) * gdn_ref[...]) * _silu(dz_ref[0])).astype(BF16)
    mix = (jnp.dot(a_n, w_ref[0:ATT_WIDTH, :], preferred_element_type=F32)
           + jnp.dot(l_n, w_ref[ATT_WIDTH:ATT_WIDTH + LRU_WIDTH, :], preferred_element_type=F32)
           + jnp.dot(dn, w_ref[ATT_WIDTH + LRU_WIDTH:, :], preferred_element_type=F32))
    x = jnp.where(pl.program_id(1) + off == 0, c_ref[0], x_ref[0]) if split else x_ref[0]
    xo_ref[0] = x + gt_ref[0, 0] * _rms(mix, gn_ref[...])


def _out_proj(stream, att, y, gz, o0, o1, gg, gdn4, ones256, w_out, mod4, gn1, off):
    c_src, x_src, lat_off, t = stream
    b, _, d = x_src.shape
    nb = t // TM - off
    att_off = off - (t - att.shape[1]) // TM

    def tok(width, col=0):
        return pl.BlockSpec((1, TM, width), lambda bi, i: (bi, i + off, col))

    def const(shape):
        return pl.BlockSpec(shape, lambda bi, i: (0,) * len(shape))

    return pl.pallas_call(
        functools.partial(_out_kernel, off=off, split=lat_off > 0),
        grid=(b, nb),
        in_specs=[pl.BlockSpec((1, TM, d), lambda bi, i: (bi, jnp.maximum(i + off - lat_off, 0), 0)),
                  pl.BlockSpec((1, TM, d), lambda bi, i: (bi, 0, 0)),
                  pl.BlockSpec((1, TM, ATT_WIDTH), lambda bi, i: (bi, i + att_off, 0)),
                  tok(LRU_WIDTH), tok(LRU_WIDTH, 0), tok(DN_WIDTH), tok(DN_WIDTH), tok(DN_WIDTH, 1),
                  const((1, ATT_WIDTH + LRU_WIDTH)), const((1, DN_WIDTH)), const((DN_WIDTH, DN_WIDTH)),
                  const((d, d)),
                  pl.BlockSpec((1, 1, 1, d), lambda bi, i: (_mod_row(bi, i + off), 2, 0, 0)),
                  const((1, d))],
        out_specs=pl.BlockSpec((1, TM, d), lambda bi, i: (bi, i, 0)),
        out_shape=jax.ShapeDtypeStruct((b, nb * TM, d), F32),
        compiler_params=_cparams("arbitrary", "arbitrary"),
        name="out_proj",
    )(x_src, c_src, att, y, gz, o0, o1, gz, gg, gdn4, ones256, w_out, mod4, gn1)


def _ffn_kernel(x_ref, sh_ref, sc_ref, gt_ref, gn2_ref, gn3_ref, wi_ref, wo_ref, o_ref):
    x = x_ref[0]
    h = (_rms(x, gn2_ref[...]) * (1.0 + sc_ref[0, 0]) + sh_ref[0, 0]).astype(BF16)
    gu = jnp.dot(h, wi_ref[...], preferred_element_type=F32)
    act = (_silu(gu[:, :FFN_HIDDEN]) * gu[:, FFN_HIDDEN:]).astype(BF16)
    out = jnp.dot(act, wo_ref[...], preferred_element_type=F32)
    o_ref[0] = x + gt_ref[0, 0] * _rms(out, gn3_ref[...])


def _ffn(x1, mod4, gn2, gn3, w_ffn_in, w_ffn_out, off):
    b, rows, d = x1.shape
    nb = rows // TM

    def mod(k):
        return pl.BlockSpec((1, 1, 1, d), lambda bi, i: (_mod_row(bi, i + off), k, 0, 0))

    def const(shape):
        return pl.BlockSpec(shape, lambda bi, i: (0,) * len(shape))

    return pl.pallas_call(
        _ffn_kernel,
        grid=(b, nb),
        in_specs=[pl.BlockSpec((1, TM, d), lambda bi, i: (bi, i, 0)),
                  mod(3), mod(4), mod(5), const((1, d)), const((1, d)),
                  const((d, 2 * FFN_HIDDEN)), const((FFN_HIDDEN, d))],
        out_specs=pl.BlockSpec((1, TM, d), lambda bi, i: (bi, i, 0)),
        out_shape=jax.ShapeDtypeStruct((b, rows, d), F32),
        compiler_params=_cparams("arbitrary", "arbitrary"),
        name="ffn",
    )(x1, mod4, mod4, mod4, gn2, gn3, w_ffn_in, w_ffn_out)


def _block_diag_ones(n):
    i = jnp.arange(n) // HEAD_DIM
    return (i[:, None] == i[None, :]).astype(BF16)


def _chunk_triangles(n):
    i = jnp.arange(n)
    same = (i[:, None] // DN_CHUNK) == (i[None, :] // DN_CHUNK)
    lower = jnp.logical_and(same, i[None, :] <= i[:, None])
    upper = jnp.logical_and(same, i[None, :] >= i[:, None])
    return jnp.stack([lower, upper, same]).astype(BF16)


def _head_spread(first_row):
    r = jnp.arange(LANES)[:, None]
    c = jnp.arange(N_DIR * DN_WIDTH)[None, :]
    return (r == first_row + c // DN_CHUNK).astype(BF16)


def _rope_tables(ctx, n):
    t = jnp.arange(n)
    row = (t // GRID_W).astype(F32)
    col = (t % GRID_W).astype(F32)
    n_freq = HEAD_DIM // 4
    inv_freq = ROPE_THETA ** (-jnp.arange(n_freq, dtype=F32) / n_freq)
    ang = jnp.concatenate([row[:, None] * inv_freq, col[:, None] * inv_freq], axis=-1)
    cos = jnp.concatenate([jnp.ones((ctx, HEAD_DIM // 2), F32), jnp.cos(ang)], axis=0)
    sin = jnp.concatenate([jnp.zeros((ctx, HEAD_DIM // 2), F32), jnp.sin(ang)], axis=0)
    return jnp.tile(jnp.concatenate([cos, cos], axis=1), (1, 2)), jnp.tile(jnp.concatenate([-sin, sin], axis=1), (1, 2))


def _lru_gate_weights(w_r, w_i):
    def dense(w):
        z = jnp.zeros((LRU_BLOCK_DIM, LRU_BLOCK_DIM), F32)
        return [jnp.block([[w[2 * hf], z], [z, w[2 * hf + 1]]]) for hf in range(2)]
    halves = [[], []]
    for d in range(N_DIR):
        for w in (w_r[d], w_i[d]):
            for hf, m in enumerate(dense(w)):
                halves[hf].append(m)
    return jnp.stack([jnp.concatenate(h, axis=1) for h in halves]).astype(BF16)


def _lru_vectors(lam, b_r, b_i):
    rows = [lam[0], lam[1], b_r[0], b_i[0], b_r[1], b_i[1], jnp.zeros_like(lam[0]), jnp.zeros_like(lam[0])]
    v = jnp.stack(rows)
    return jnp.stack([v[:, :LANES], v[:, LANES:]])


def kernel(x, c, ctx, c_ctx, w_ada, b_ada, g_norm, w_in, g_qk, lru_conv_w, lru_conv_b, lru_w_r, lru_b_r, lru_w_i, lru_b_i, lru_lambda, dn_conv_w, dn_a_log, dn_dt_bias, g_dn_out, g_group, w_out, w_ffn_in, w_ffn_out):
    depth = w_ada.shape[0]
    bsz, n, d = x.shape
    m = ctx.shape[1]
    assert m == TM and d == D_MODEL and bsz == 2 and n % (ATT_STEPS_PER_BODY * ATT_TK) == 0

    cvec = jnp.concatenate([c, c_ctx[None, :], jnp.zeros((SUBLANES - bsz - 1, d), F32)], axis=0)
    mods = _modulation(cvec, w_ada, b_ada)
    cosf, sinf = _rope_tables(m, n)
    ones128 = _block_diag_ones(LANES)
    ones256 = _block_diag_ones(DN_WIDTH)
    tri = _chunk_triangles(TM)
    xg = _head_spread(0)
    xb = _head_spread(N_DIR * DN_HEADS)

    stream = (ctx, x, 1, m + n)
    for l in range(depth):
        last = l == depth - 1
        off = 1 if last else 0
        mod4 = mods[l].reshape(SUBLANES, 6, 1, d)
        wl = w_in[l]
        w_in_p = jnp.concatenate([wl[:, 0:768], wl[:, 1280:2048], wl[:, 1024:1280], wl[:, 768:1024],
                                  wl[:, 2048:2320], jnp.zeros((d, IN_COLS_PAD - 2320), F32)], axis=1).astype(BF16)
        cw = jnp.concatenate([dn_conv_w[l], lru_conv_w[l]], axis=1)
        cb = jnp.concatenate([jnp.zeros((3 * DN_WIDTH,), F32), lru_conv_b[l]])[None, :]
        pad = jnp.zeros((LANES - N_DIR * DN_HEADS,), F32)
        dnv = jnp.stack([jnp.concatenate([dn_a_log[l].reshape(-1), pad]),
                         jnp.concatenate([dn_dt_bias[l].reshape(-1), pad])])
        qt, k, vt, gz, xr, qk, pk, ge, rws = _in_proj(
            stream, mod4, g_norm[l, 0][None, :], w_in_p, jnp.tile(g_qk[l], (1, 2)), cosf, sinf, ones128,
            cw, cb, dnv, ones256, tri, xg, xb)
        key_norm_bound = 1.01 * HEAD_DIM ** 0.5 * jnp.max(jnp.abs(g_qk[l, 1]))
        score_bound = key_norm_bound * (LOG2E * jnp.max(jnp.abs(g_qk[l, 0])))
        att = _attention(qt, k, vt, key_norm_bound, score_bound, m, off)

        y = _lru(xr, _lru_gate_weights(lru_w_r[l], lru_w_i[l]),
                 _lru_vectors(lru_lambda[l], lru_b_r[l], lru_b_i[l]), m)
        o0, o1 = _deltanet(qk, pk, ge, rws, m)

        x1 = _out_proj(stream, att, y, gz, o0, o1, g_group[l][None, :], jnp.tile(g_dn_out[l], DN_HEADS)[None, :],
                       ones256, w_out[l].astype(BF16), mod4, g_norm[l, 1][None, :], off)
        xs = _ffn(x1, mod4, g_norm[l, 2][None, :], g_norm[l, 3][None, :],
                  w_ffn_in[l].astype(BF16), w_ffn_out[l].astype(BF16), off)
        stream = (xs, xs, 0, m + n)
    return xs
```

```python
import functools

import jax
import jax.numpy as jnp
from jax import lax
from jax.experimental import pallas as pl
from jax.experimental.pallas import tpu as pltpu

F32 = jnp.float32
BF16 = jnp.bfloat16

D_MODEL = 1024
GRID_W = 64
EPS = 1e-6
ATT_GROUP = 4
ATT_KV_HEADS = 2
HEAD_DIM = 64
ATT_WIDTH = 512
ROPE_THETA = 10000.0
LRU_WIDTH = 256
LRU_BLOCK_DIM = 64
LRU_C = 8.0
DN_HEADS = 4
DN_WIDTH = 256
DN_CHUNK = 64
DN_BASE = 16
DN_BLOCK = 4 * DN_CHUNK
DN_ROWS = 2 * (DN_BLOCK // DN_CHUNK)
DN_PACK = 5 * DN_WIDTH
N_DIR = 2
FFN_HIDDEN = 2816

LANES = 128
SUBLANES = 8
TM = 256
ATT_TK = 1024
ATT_STEPS_PER_BODY = 4
ATT_VROWS = 80
LOG2E = 1.4426950408889634
ATT_FIXED_SHIFT_LIMIT = 48.0
QKV_COLS = 768
CONV_COLS = 1024
REST_COLS = 1664
IN_COLS_PAD = QKV_COLS + REST_COLS
VMEM_LIMIT = 56 * 1024 * 1024


def _cparams(*sem):
    return pltpu.CompilerParams(dimension_semantics=sem, vmem_limit_bytes=VMEM_LIMIT)


def _rms(x, gain):
    ms = jnp.mean(x * x, axis=-1, keepdims=True)
    return x * lax.rsqrt(ms + EPS) * gain


def _seg_sumsq(x, ones_bd):
    sq = x * x
    hi = sq.astype(BF16)
    lo = (sq - hi.astype(F32)).astype(BF16)
    return (jnp.dot(hi, ones_bd, preferred_element_type=F32)
            + jnp.dot(lo, ones_bd, preferred_element_type=F32))


def _silu(x):
    return x * jax.nn.sigmoid(x)


def _softplus(x):
    return jnp.maximum(x, 0.0) + jnp.log1p(jnp.exp(-jnp.abs(x)))


def _mod_row(b, i):
    return jnp.where(i == 0, 2, b)


def _mod_kernel(c_ref, w_ref, b_ref, o_ref):
    s = _silu(c_ref[...])
    o_ref[0] = jnp.dot(s.astype(BF16), w_ref[0].astype(BF16), preferred_element_type=F32) + b_ref[0]


def _modulation(cvec, w_ada, b_ada):
    depth = w_ada.shape[0]
    d = D_MODEL
    return pl.pallas_call(
        _mod_kernel,
        grid=(depth, 6),
        in_specs=[pl.BlockSpec((SUBLANES, d), lambda l, j: (0, 0)),
                  pl.BlockSpec((1, d, d), lambda l, j: (l, 0, j)),
                  pl.BlockSpec((1, 1, d), lambda l, j: (l, 0, j))],
        out_specs=pl.BlockSpec((1, SUBLANES, d), lambda l, j: (l, 0, j)),
        out_shape=jax.ShapeDtypeStruct((depth, SUBLANES, 6 * d), F32),
        compiler_params=_cparams("arbitrary", "arbitrary"),
        name="modulation",
    )(cvec, w_ada, b_ada.reshape(depth, 1, 6 * d))


def _split3(x):
    hi = x.astype(BF16)
    r = x - hi.astype(F32)
    mid = r.astype(BF16)
    lo = (r - mid.astype(F32)).astype(BF16)
    return hi, mid, lo


def _dot_parts(parts, mat, left=False):
    out = None
    for p in parts:
        t = (jnp.dot(mat, p, preferred_element_type=F32) if left else jnp.dot(p, mat, preferred_element_type=F32))
        out = t if out is None else out + t
    return out


def _in_kernel(x_ref, c_ref, xp_ref, xn_ref, sh_ref, sc_ref, gn_ref, w_ref, gqk_ref, cos_ref, sin_ref, ones_ref,
               cw_ref, cb_ref, dnv_ref, ones2_ref, tri_ref, xg_ref, xb_ref,
               qt_ref, k_ref, vt_ref, gz_ref, xr_ref, qk_ref, pk_ref, ge_ref, rows_ref, *, split):
    i = pl.program_id(1)
    nb = pl.num_programs(1)
    c_, w_ = DN_CHUNK, DN_WIDTH
    x = jnp.where(i == 0, c_ref[0], x_ref[0]) if split else x_ref[0]
    xe = jnp.concatenate([x, xp_ref[0], xn_ref[0]], axis=0)
    hb = (_rms(xe, gn_ref[...]) * (1.0 + sc_ref[0, 0]) + sh_ref[0, 0]).astype(BF16)
    hb_cur = hb[0:TM]
    qkv = jnp.dot(hb_cur, w_ref[:, 0:QKV_COLS], preferred_element_type=F32)
    cosf = cos_ref[...]
    sinf = sin_ref[...]
    lane = lax.broadcasted_iota(jnp.int32, (TM, LANES), 1)
    first_half = (lane & 32) == 0
    ones_bd = ones_ref[...]

    def norm_rope(slab, gain):
        ss = _seg_sumsq(slab, ones_bd)
        y = slab * lax.rsqrt(ss * (1.0 / HEAD_DIM) + EPS) * gain
        partner = jnp.where(first_half, pltpu.roll(y, 96, axis=1), pltpu.roll(y, 32, axis=1))
        return y * cosf + partner * sinf

    for s in range(ATT_WIDTH // LANES):
        qs = norm_rope(qkv[:, LANES * s:LANES * (s + 1)], gqk_ref[0:1, :]) * (LOG2E * HEAD_DIM ** -0.5)
        qst = qs.T.astype(BF16)
        for e in range(2):
            g, hh = divmod(2 * s + e, ATT_GROUP)
            qt_ref[0, g, 0, 0:HEAD_DIM, TM * hh:TM * (hh + 1)] = qst[HEAD_DIM * e:HEAD_DIM * (e + 1), :]
    for g in range(ATT_KV_HEADS):
        qt_ref[0, g, 0, HEAD_DIM:, :] = jnp.zeros((LANES - HEAD_DIM, ATT_GROUP * TM), BF16)
    kk = norm_rope(qkv[:, 512:640], gqk_ref[1:2, :])
    one_lane = jnp.where(lane == HEAD_DIM, 1.0, 0.0)
    vvt = qkv[:, 640:768].T.astype(BF16)
    row = lax.broadcasted_iota(jnp.int32, (ATT_VROWS - HEAD_DIM, TM), 0)
    ones_rows = jnp.where(row == 0, 1.0, 0.0).astype(BF16)
    for g in range(ATT_KV_HEADS):
        kg = kk if g == 0 else pltpu.roll(kk, HEAD_DIM, axis=1)
        k_ref[0, g] = jnp.where(lane < HEAD_DIM, kg, one_lane).astype(BF16)
        vt_ref[0, g, 0, 0:HEAD_DIM, :] = vvt[HEAD_DIM * g:HEAD_DIM * (g + 1), :]
        vt_ref[0, g, 0, HEAD_DIM:, :] = ones_rows

    tail = jnp.dot(hb_cur, w_ref[:, QKV_COLS + CONV_COLS:], preferred_element_type=F32)
    gz_ref[0] = tail[:, 0:LRU_WIDTH + DN_WIDTH]
    ab = tail[:, LRU_WIDTH + DN_WIDTH:]

    cin = jnp.dot(hb, w_ref[:, QKV_COLS:QKV_COLS + CONV_COLS], preferred_element_type=F32)
    cur = cin[0:TM]
    prv = jnp.where(i >= 2, cin[TM:TM + SUBLANES], 0.0)
    nxt = jnp.where(jnp.logical_and(i >= 1, i < nb - 1), cin[TM + SUBLANES:], 0.0)
    trow = lax.broadcasted_iota(jnp.int32, cur.shape, 0)
    x_m1 = jnp.where(trow == 0, prv[SUBLANES - 1:SUBLANES, :], pltpu.roll(cur, 1, axis=0))
    x_p1 = jnp.where(trow == TM - 1, nxt[0:1, :], pltpu.roll(cur, TM - 1, axis=0))
    x_p2 = jnp.where(trow == TM - 2, nxt[0:1, :],
                     jnp.where(trow == TM - 1, nxt[1:2, :], pltpu.roll(cur, TM - 2, axis=0)))
    conv = (cb_ref[...] + x_m1 * cw_ref[0:1, :] + cur * cw_ref[1:2, :]
            + x_p1 * cw_ref[2:3, :] + x_p2 * cw_ref[3:4, :])
    xr_ref[0] = conv[:, 3 * w_:]
    act = _silu(conv[:, 0:3 * w_])
    qq = act[:, 0:w_]
    kq = act[:, w_:2 * w_]
    vv = act[:, 2 * w_:3 * w_]
    ones2 = ones2_ref[...]
    qn = qq * lax.rsqrt(_seg_sumsq(qq, ones2) + EPS) * (c_ ** -0.5)
    kn = kq * lax.rsqrt(_seg_sumsq(kq, ones2) + EPS)
    qk_ref[0, :, 0:w_] = qn.astype(BF16)
    qk_ref[0, :, w_:2 * w_] = kn.astype(BF16)

    g = jnp.where(lane < N_DIR * DN_HEADS, -jnp.exp(dnv_ref[0:1, :]) * _softplus(ab + dnv_ref[1:2, :]), 0.0)
    beta = jax.nn.sigmoid(ab)
    g_parts = _split3(g)
    gc = jnp.where(lane < DN_HEADS, _dot_parts(g_parts, tri_ref[0], left=True),
                   _dot_parts(g_parts, tri_ref[1], left=True))
    tot = _dot_parts(g_parts, tri_ref[2], left=True)
    ge = _dot_parts(_split3(gc), xg_ref[...])
    te = _dot_parts(_split3(tot), xg_ref[...])
    be = _dot_parts(_split3(beta)[:2], xb_ref[...])
    ge_ref[0] = ge
    crow = lax.broadcasted_iota(jnp.int32, (c_, w_), 0)
    diag = crow == (lax.broadcasted_iota(jnp.int32, (c_, w_), 1) & (c_ - 1))
    for d in range(N_DIR):
        ge_d = ge[:, w_ * d:w_ * (d + 1)]
        te_d = te[:, w_ * d:w_ * (d + 1)]
        be_d = be[:, w_ * d:w_ * (d + 1)]
        eg = jnp.exp(ge_d)
        kb = kn * be_d
        pk_ref[0, d, :, 0:w_] = kb.astype(BF16)
        pk_ref[0, d, :, w_:2 * w_] = (vv * be_d).astype(BF16)
        pk_ref[0, d, :, 2 * w_:3 * w_] = (kb * eg).astype(BF16)
        pk_ref[0, d, :, 3 * w_:4 * w_] = (qn * eg).astype(BF16)
        pk_ref[0, d, :, 4 * w_:5 * w_] = (kn * jnp.exp(te_d - ge_d)).astype(BF16)
        for pr in range(TM // DN_BLOCK):
            for c in range(DN_BLOCK // c_):
                r0 = DN_BLOCK * pr + c_ * c
                rows_ref[0, d, pr, c:c + 1, :] = jnp.sum(jnp.where(diag, ge_d[r0:r0 + c_, :], 0.0), axis=0,
                                                         keepdims=True)
                rows_ref[0, d, pr, DN_ROWS // 2 + c:DN_ROWS // 2 + c + 1, :] = te_d[r0:r0 + 1, :]


def _in_proj(stream, mod4, gn0, w_in_p, gqk2, cosf, sinf, ones128, cw, cb, dnv, ones256, tri, xg, xb):
    c_src, x_src, lat_off, t = stream
    b, rows_src, d = x_src.shape
    nb = t // TM
    r8 = TM // SUBLANES
    ppb = TM // DN_BLOCK

    def const(shape):
        return pl.BlockSpec(shape, lambda bi, i: (0,) * len(shape))

    def halo(step):
        return pl.BlockSpec((1, SUBLANES, d), lambda bi, i: (
            bi, jnp.clip((i - lat_off) * r8 + step, 0, rows_src // SUBLANES - 1), 0))

    return pl.pallas_call(
        functools.partial(_in_kernel, split=lat_off > 0),
        grid=(b, nb),
        in_specs=[pl.BlockSpec((1, TM, d), lambda bi, i: (bi, jnp.maximum(i - lat_off, 0), 0)),
                  pl.BlockSpec((1, TM, d), lambda bi, i: (bi, 0, 0)),
                  halo(-1), halo(r8),
                  pl.BlockSpec((1, 1, 1, d), lambda bi, i: (_mod_row(bi, i), 0, 0, 0)),
                  pl.BlockSpec((1, 1, 1, d), lambda bi, i: (_mod_row(bi, i), 1, 0, 0)),
                  const((1, d)), const((d, IN_COLS_PAD)), const((2, LANES)),
                  pl.BlockSpec((TM, LANES), lambda bi, i: (i, 0)),
                  pl.BlockSpec((TM, LANES), lambda bi, i: (i, 0)),
                  const((LANES, LANES)),
                  const((4, CONV_COLS)), const((1, CONV_COLS)), const((2, LANES)), const((DN_WIDTH, DN_WIDTH)),
                  const((3, TM, TM)), const((LANES, N_DIR * DN_WIDTH)), const((LANES, N_DIR * DN_WIDTH))],
        out_specs=[pl.BlockSpec((1, ATT_KV_HEADS, 1, LANES, ATT_GROUP * TM), lambda bi, i: (bi, 0, i, 0, 0)),
                   pl.BlockSpec((1, ATT_KV_HEADS, TM, LANES), lambda bi, i: (bi, 0, i, 0)),
                   pl.BlockSpec((1, ATT_KV_HEADS, 1, ATT_VROWS, TM), lambda bi, i: (bi, 0, i, 0, 0)),
                   pl.BlockSpec((1, TM, LRU_WIDTH + DN_WIDTH), lambda bi, i: (bi, i, 0)),
                   pl.BlockSpec((1, TM, LRU_WIDTH), lambda bi, i: (bi, i, 0)),
                   pl.BlockSpec((1, TM, 2 * DN_WIDTH), lambda bi, i: (bi, i, 0)),
                   pl.BlockSpec((1, N_DIR, TM, DN_PACK), lambda bi, i: (bi, 0, i, 0)),
                   pl.BlockSpec((1, TM, N_DIR * DN_WIDTH), lambda bi, i: (bi, i, 0)),
                   pl.BlockSpec((1, N_DIR, ppb, DN_ROWS, DN_WIDTH), lambda bi, i: (bi, 0, i, 0, 0))],
        out_shape=[jax.ShapeDtypeStruct((b, ATT_KV_HEADS, nb, LANES, ATT_GROUP * TM), BF16),
                   jax.ShapeDtypeStruct((b, ATT_KV_HEADS, t, LANES), BF16),
                   jax.ShapeDtypeStruct((b, ATT_KV_HEADS, nb, ATT_VROWS, TM), BF16),
                   jax.ShapeDtypeStruct((b, t, LRU_WIDTH + DN_WIDTH), F32),
                   jax.ShapeDtypeStruct((b, t, LRU_WIDTH), F32),
                   jax.ShapeDtypeStruct((b, t, 2 * DN_WIDTH), BF16),
                   jax.ShapeDtypeStruct((b, N_DIR, t, DN_PACK), BF16),
                   jax.ShapeDtypeStruct((b, t, N_DIR * DN_WIDTH), F32),
                   jax.ShapeDtypeStruct((b, N_DIR, t // DN_BLOCK, DN_ROWS, DN_WIDTH), F32)],
        compiler_params=_cparams("arbitrary", "arbitrary"),
        name="in_proj",
    )(x_src, c_src, x_src, x_src, mod4, mod4, gn0, w_in_p, gqk2, cosf, sinf, ones128,
      cw, cb, dnv, ones256, tri, xg, xb)


def _att_kernel(qt_ref, k_ref, vt_ref, kn_ref, o_ref, m_ref, acc_ref, s_ref, cmax_ref, *, ctx, n_lat, q_off):
    del kn_ref
    i = pl.program_id(2) + q_off
    qt = qt_ref[0, 0, 0]
    nsub = ATT_TK // TM

    def scores(start, size):
        return jnp.dot(k_ref[0, 0, pl.ds(start, size), :], qt, preferred_element_type=F32)

    def pv(pt, blk0, nblk):
        out = None
        for j in range(nblk):
            t = jnp.dot(vt_ref[0, 0, blk0 + j], pt[j * TM:(j + 1) * TM, :], preferred_element_type=F32)
            out = t if out is None else out + t
        return out

    st = scores(0, ctx)
    m0 = jnp.max(st, axis=0, keepdims=True)
    m_ref[...] = m0
    acc_ref[...] = pv(jnp.exp2(st - m0).astype(BF16), 0, ctx // TM)

    n_chunks = n_lat // ATT_TK

    def step(j, cur, nxt):
        jn = jnp.minimum(j + 1, n_chunks - 1)
        sn = scores(pl.multiple_of(ctx + jn * ATT_TK, TM), ATT_TK)
        s_ref[nxt] = sn
        cmax_ref[nxt] = jnp.max(sn, axis=0, keepdims=True)
        st = s_ref[cur]
        m_prev = m_ref[...]
        m_new = jnp.maximum(m_prev, cmax_ref[cur])
        alpha = jnp.exp2(m_prev - m_new)
        pt = jnp.exp2(st - m_new).astype(BF16)
        acc_ref[...] = alpha * acc_ref[...] + pv(pt, ctx // TM + j * nsub, nsub)
        m_ref[...] = m_new

    def body(jj, carry):
        step(2 * jj, 0, 1)
        step(2 * jj + 1, 1, 0)
        return carry

    @pl.when(i > 0)
    def _():
        s0 = scores(ctx, ATT_TK)
        s_ref[0] = s0
        cmax_ref[0] = jnp.max(s0, axis=0, keepdims=True)
        lax.fori_loop(0, n_chunks // 2, body, 0)

    acc = acc_ref[...]
    out_t = acc[0:HEAD_DIM] / acc[HEAD_DIM:HEAD_DIM + 1]
    for h in range(ATT_GROUP):
        o_ref[0, :, HEAD_DIM * h:HEAD_DIM * (h + 1)] = out_t[:, TM * h:TM * (h + 1)].T


def _att_fixed_kernel(qt_ref, k_ref, vt_ref, kn_ref, o_ref, acc_ref, p_ref, qa_ref, *, ctx, n_lat, q_off):
    i = pl.program_id(2) + q_off
    nsub = ATT_TK // TM
    n_chunks = n_lat // ATT_TK
    qt = qt_ref[0, 0, 0]
    q32 = qt[0:HEAD_DIM, :].astype(F32)
    r = jnp.sqrt(jnp.sum(q32 * q32, axis=0, keepdims=True)) * kn_ref[...]
    tile_row = lax.broadcasted_iota(jnp.int32, (2 * SUBLANES, ATT_GROUP * TM), 0)
    qa_ref[...] = qt
    qa_ref[HEAD_DIM:HEAD_DIM + 2 * SUBLANES, :] = jnp.where(tile_row == 0, -r, 0.0).astype(BF16)

    def probs(start, size):
        s = jnp.dot(k_ref[0, 0, pl.ds(start, size), :], qa_ref[...], preferred_element_type=F32)
        return jnp.exp2(s).astype(BF16)

    def pv(pt, blk0, nblk):
        out = None
        for j in range(nblk):
            t = jnp.dot(vt_ref[0, 0, blk0 + j], pt[j * TM:(j + 1) * TM, :], preferred_element_type=F32)
            out = t if out is None else out + t
        return out

    def step(j, cur, nxt, prefetch=True):
        if prefetch:
            p_ref[nxt] = probs(pl.multiple_of(ctx + (j + 1) * ATT_TK, TM), ATT_TK)
        acc_ref[...] += pv(p_ref[cur], ctx // TM + j * nsub, nsub)

    def body(jj, carry, last=False):
        for u in range(ATT_STEPS_PER_BODY):
            step(ATT_STEPS_PER_BODY * jj + u, (u + 1) % 2, u % 2,
                 prefetch=not (last and u == ATT_STEPS_PER_BODY - 1))
        return carry

    @pl.when(i == 0)
    def _():
        acc_ref[...] = pv(probs(0, ctx), 0, ctx // TM)

    @pl.when(i > 0)
    def _():
        p_ctx = probs(0, ctx)
        p_ref[1] = probs(ctx, ATT_TK)
        acc_ref[...] = pv(p_ctx, 0, ctx // TM)
        n_bodies = n_chunks // ATT_STEPS_PER_BODY
        lax.fori_loop(0, n_bodies - 1, body, 0)
        body(n_bodies - 1, 0, last=True)

    acc = acc_ref[...]
    out_t = acc[0:HEAD_DIM] / acc[HEAD_DIM:HEAD_DIM + 1]
    for h in range(ATT_GROUP):
        o_ref[0, :, HEAD_DIM * h:HEAD_DIM * (h + 1)] = out_t[:, TM * h:TM * (h + 1)].T


def _attention(qt, k, vt, key_norm_bound, score_bound, ctx, q_off):
    b, _, t, _ = k.shape
    nb = t // TM
    nq = nb - q_off
    gw = ATT_GROUP * HEAD_DIM
    cols = ATT_GROUP * TM

    def call(body, scratch):
        return pl.pallas_call(
            functools.partial(body, ctx=ctx, n_lat=t - ctx, q_off=q_off),
            grid=(b, ATT_KV_HEADS, nq),
            in_specs=[pl.BlockSpec((1, 1, 1, LANES, cols), lambda bi, g, i: (bi, g, i + q_off, 0, 0)),
                      pl.BlockSpec((1, 1, t, LANES), lambda bi, g, i: (bi, g, 0, 0)),
                      pl.BlockSpec((1, 1, nb, ATT_VROWS, TM), lambda bi, g, i: (bi, g, 0, 0, 0)),
                      pl.BlockSpec((1, 1), lambda bi, g, i: (0, 0))],
            out_specs=pl.BlockSpec((1, TM, gw), lambda bi, g, i: (bi, i, g)),
            out_shape=jax.ShapeDtypeStruct((b, nq * TM, ATT_WIDTH), F32),
            scratch_shapes=scratch,
            compiler_params=_cparams("arbitrary", "arbitrary", "arbitrary"),
            name="attention",
        )

    online = call(_att_kernel, [pltpu.VMEM((1, cols), F32), pltpu.VMEM((ATT_VROWS, cols), F32),
                                pltpu.VMEM((2, ATT_TK, cols), F32), pltpu.VMEM((2, 1, cols), F32)])
    fixed = call(_att_fixed_kernel, [pltpu.VMEM((ATT_VROWS, cols), F32), pltpu.VMEM((2, ATT_TK, cols), BF16),
                                     pltpu.VMEM((LANES, cols), BF16)])
    return lax.cond(score_bound <= ATT_FIXED_SHIFT_LIMIT, fixed, online, qt, k, vt,
                    jnp.reshape(key_norm_bound, (1, 1)))


def _lru_kernel(xr_ref, wg_ref, vec_ref, y_ref, a0, u0, a1, u1, *, ctx, n):
    nseg = SUBLANES
    vec = vec_ref[0]
    a_refs = (a0, a1)
    u_refs = (u0, u1)

    def gates(x, d):
        g = jnp.dot(x.astype(BF16), wg_ref[0, :, 2 * LANES * d:2 * LANES * (d + 1)],
                    preferred_element_type=F32)
        r = 0.5 * jnp.tanh(0.5 * (g[:, :LANES] + vec[2 + 2 * d:3 + 2 * d])) + 0.5
        ig = 0.5 * jnp.tanh(0.5 * (g[:, LANES:] + vec[3 + 2 * d:4 + 2 * d])) + 0.5
        log_a = -LRU_C * r * _softplus(-vec[d:d + 1])
        a = jnp.exp(log_a)
        mult = jnp.sqrt(-jnp.tanh(log_a) * (a * a + 1.0))
        return a, mult * (ig * x)

    def run(row0, seg, base, init):
        pitch = seg + SUBLANES
        for j in range(nseg):
            x = xr_ref[0, row0 + j * seg:row0 + (j + 1) * seg, :]
            for d in range(N_DIR):
                a, u = gates(x, d)
                a_refs[d][base + j * pitch:base + j * pitch + seg, :] = a
                u_refs[d][base + j * pitch:base + j * pitch + seg, :] = u

        def step(t, carry):
            out = []
            for d in range(N_DIR):
                h, p = carry[2 * d], carry[2 * d + 1]
                r = base + (t if d == 0 else seg - 1 - t)
                idx = pl.ds(r, nseg, stride=pitch)
                a = a_refs[d][idx, :]
                h = a * h + u_refs[d][idx, :]
                p = a * p
                u_refs[d][idx, :] = h
                a_refs[d][idx, :] = p
                out += [h, p]
            return tuple(out)

        z = jnp.zeros((nseg, LANES), F32)
        o = jnp.ones((nseg, LANES), F32)
        fin = lax.fori_loop(0, seg, step, (z, o, z, o), unroll=8)
        carries, finals = [], []
        for d in range(N_DIR):
            hl, pl_ = fin[2 * d], fin[2 * d + 1]
            order = list(range(nseg)) if d == 0 else list(range(nseg - 1, -1, -1))
            c = init[d]
            cs = {}
            for j in order:
                cs[j] = c
                c = hl[j:j + 1] + pl_[j:j + 1] * c
            carries.append(cs)
            finals.append(c)
        for j in range(nseg):
            sl = slice(base + j * pitch, base + j * pitch + seg)
            y = (u0[sl, :] + a0[sl, :] * carries[0][j]) + (u1[sl, :] + a1[sl, :] * carries[1][j])
            y_ref[0, row0 + j * seg:row0 + (j + 1) * seg, :] = y
        return finals

    zero = jnp.zeros((1, LANES), F32)
    seg_c = ctx // nseg
    fin_c = run(0, seg_c, 0, [zero, zero])
    run(ctx, n // nseg, nseg * (seg_c + SUBLANES), fin_c)


def _lru(xr, wg, vec, ctx):
    b, t, _ = xr.shape
    n = t - ctx
    rows = SUBLANES * (ctx // SUBLANES + SUBLANES) + SUBLANES * (n // SUBLANES + SUBLANES)
    kern = functools.partial(_lru_kernel, ctx=ctx, n=n)
    return pl.pallas_call(
        kern,
        grid=(b, LRU_WIDTH // LANES),
        in_specs=[pl.BlockSpec((1, t, LANES), lambda bi, hf: (bi, 0, hf)),
                  pl.BlockSpec((1, LANES, 4 * LANES), lambda bi, hf: (hf, 0, 0)),
                  pl.BlockSpec((1, SUBLANES, LANES), lambda bi, hf: (hf, 0, 0))],
        out_specs=pl.BlockSpec((1, t, LANES), lambda bi, hf: (bi, 0, hf)),
        out_shape=jax.ShapeDtypeStruct((b, t, LRU_WIDTH), F32),
        scratch_shapes=[pltpu.VMEM((rows, LANES), F32)] * 4,
        compiler_params=_cparams("arbitrary", "arbitrary"),
        name="rglru",
    )(xr, wg, vec)


def _dn_kernel(qkf_ref, pkf_ref, gef_ref, rwf_ref, qkb_ref, pkb_ref, geb_ref, rwb_ref, o0_ref, o1_ref, s_ref):
    c_, w_ = DN_CHUNK, DN_WIDTH
    n_batch = o0_ref.shape[0]

    @pl.when(pl.program_id(0) == 0)
    def _():
        s_ref[...] = jnp.zeros_like(s_ref)

    row = lax.broadcasted_iota(jnp.int32, (c_, w_), 0)
    lane = lax.broadcasted_iota(jnp.int32, (c_, w_), 1)
    lj = lane & (c_ - 1)
    head_masks = [(lane >> 6) == h for h in range(DN_HEADS)]
    eye_f = jnp.where(row == lj, 1.0, 0.0)
    shift = DN_BASE.bit_length() - 1
    same_base = (row >> shift) == (lj >> shift)
    off_blk = ({}, {})
    size = DN_BASE
    while size < c_:
        shift = size.bit_length() - 1
        bi, bj = row >> shift, lj >> shift
        off_blk[0][size] = jnp.logical_and(bi == bj + 1, (bi & 1) == 1)
        off_blk[1][size] = jnp.logical_and(bj == bi + 1, (bj & 1) == 1)
        size *= 2
    bd_mask = ((lax.broadcasted_iota(jnp.int32, (w_, w_), 0) >> 6)
               == (lax.broadcasted_iota(jnp.int32, (w_, w_), 1) >> 6))
    nt = (((1,), (1,)), ((), ()))
    tn = (((0,), (0,)), ((), ()))

    def bd(x):
        xb = x.astype(BF16)
        zero = jnp.zeros_like(xb)
        return jnp.concatenate([jnp.where(hm, xb, zero) for hm in head_masks], axis=0)

    def bdot(a, b, dims=None):
        if dims is None:
            return jnp.dot(a.astype(BF16), b, preferred_element_type=F32)
        return lax.dot_general(a.astype(BF16), b, dims, preferred_element_type=F32)

    per_block = DN_BLOCK // c_
    in_refs = ((qkf_ref, pkf_ref, gef_ref, rwf_ref), (qkb_ref, pkb_ref, geb_ref, rwb_ref))
    o_refs = (o0_ref, o1_ref)
    probs = []
    for d in range(N_DIR):
        incl = (lj <= row) if d == 0 else (lj >= row)
        strict = (lj < row) if d == 0 else (lj > row)
        qk_ref, pk_ref, ge_ref, rw_ref = in_refs[d]
        for bi, step in [(bi, step) for bi in range(n_batch) for step in range(per_block)]:
            c = step if d == 0 else per_block - 1 - step
            rows = slice(c_ * c, c_ * (c + 1))
            decay = jnp.where(incl, jnp.exp(jnp.where(incl, ge_ref[bi, rows, :] - rw_ref[bi, 0, 0, c:c + 1, :], 0.0)),
                              0.0)
            probs.append(dict(d=d, bi=bi, step=step, rows=rows, incl=incl, strict=strict, decay=decay,
                              st=N_DIR * bi + d,
                              kbq=jnp.concatenate([pk_ref[bi, 0, rows, 0:w_], qk_ref[bi, rows, 0:w_]], axis=0),
                              k=qk_ref[bi, rows, w_:2 * w_], vb=pk_ref[bi, 0, rows, w_:2 * w_],
                              kbe=pk_ref[bi, 0, rows, 2 * w_:3 * w_], qd=pk_ref[bi, 0, rows, 3 * w_:4 * w_],
                              kt=pk_ref[bi, 0, rows, 4 * w_:5 * w_],
                              gt=jnp.exp(rw_ref[bi, 0, 0, per_block + c:per_block + c + 1, :])))
    for p in probs:
        a1 = bdot(p["kbq"], bd(p["k"]), nt)
        p["lower"] = jnp.where(p["strict"], a1[:c_] * p["decay"], 0.0)
        p["attn"] = jnp.where(p["incl"], a1[c_:] * p["decay"], 0.0)
        p["pw"] = jnp.where(same_base, p["lower"], 0.0)
        p["tm"] = eye_f - p["pw"]
    for p in probs:
        p["pw"] = bdot(p["pw"], bd(p["pw"]))
    for _ in range(DN_BASE.bit_length() - 3):
        for p in probs:
            t = bdot(jnp.concatenate([p["tm"], p["pw"]], axis=0), bd(p["pw"]))
            p["tm"] = p["tm"] + t[:c_]
            p["pw"] = t[c_:]
    for p in probs:
        p["tm"] = p["tm"] + bdot(p["tm"], bd(p["pw"]))
    size = DN_BASE
    while size < c_:
        for p in probs:
            p["tmp"] = bdot(jnp.where(off_blk[p["d"]][size], p["lower"], 0.0), bd(p["tm"]))
        for p in probs:
            p["tm"] = p["tm"] - bdot(p["tm"], bd(p["tmp"]))
        size *= 2
    for p in probs:
        sol = bdot(p["tm"], jnp.concatenate([bd(p["vb"]), bd(p["kbe"])], axis=1))
        p["u"] = sol[:, :w_]
        p["wqd"] = jnp.concatenate([sol[:, w_:].astype(BF16), p["qd"]], axis=0)
    for step in range(per_block):
        cur = [p for p in probs if p["step"] == step]
        for p in cur:
            p["s_old"] = s_ref[p["st"]]
            p["a2"] = bdot(p["wqd"], p["s_old"].astype(BF16))
        for p in cur:
            p["v_new"] = p["u"] - p["a2"][:c_]
        for p in cur:
            o_refs[p["d"]][p["bi"], p["rows"], :] = p["a2"][c_:] + bdot(p["attn"], bd(p["v_new"]))
        for p in cur:
            upd = bdot(p["kt"], p["v_new"].astype(BF16), tn)
            s_ref[p["st"]] = p["s_old"] * p["gt"] + jnp.where(bd_mask, upd, 0.0)


def _deltanet(qk, pk, ge, rws, ctx):
    b, t, _ = qk.shape
    np_ = t // DN_BLOCK
    npc = ctx // DN_BLOCK

    def bwd(p):
        return jnp.where(p < npc, npc - 1 - p, np_ - 1 - (p - npc))

    def specs(d, pos):
        return [pl.BlockSpec((b, DN_BLOCK, 2 * DN_WIDTH), lambda p: (0, pos(p), 0)),
                pl.BlockSpec((b, 1, DN_BLOCK, DN_PACK), lambda p: (0, d, pos(p), 0)),
                pl.BlockSpec((b, DN_BLOCK, DN_WIDTH), lambda p: (0, pos(p), d)),
                pl.BlockSpec((b, 1, 1, DN_ROWS, DN_WIDTH), lambda p: (0, d, pos(p), 0, 0))]

    o_shape = jax.ShapeDtypeStruct((b, t, DN_WIDTH), F32)
    return pl.pallas_call(
        _dn_kernel,
        grid=(np_,),
        in_specs=specs(0, lambda p: p) + specs(1, bwd),
        out_specs=[pl.BlockSpec((b, DN_BLOCK, DN_WIDTH), lambda p: (0, p, 0)),
                   pl.BlockSpec((b, DN_BLOCK, DN_WIDTH), lambda p: (0, bwd(p), 0))],
        out_shape=[o_shape, o_shape],
        scratch_shapes=[pltpu.VMEM((b * N_DIR, DN_WIDTH, DN_WIDTH), F32)],
        compiler_params=_cparams("arbitrary"),
        name="deltanet",
    )(qk, pk, ge, rws, qk, pk, ge, rws)


def _out_kernel(x_ref, c_ref, att_ref, y_ref, lg_ref, o0_ref, o1_ref, dz_ref, gg_ref, gdn_ref, ones_ref, w_ref,
                gt_ref, gn_ref, xo_ref, *, off, split):
    gg = gg_ref[...]
    a_n = _rms(att_ref[0], gg[:, 0:ATT_WIDTH]).astype(BF16)
    lg = lg_ref[0]
    gelu = lg * (0.5 * (1.0 + jnp.tanh(0.7978845608028654 * (lg + 0.044715 * (lg * lg * lg)))))
    l_n = _rms(gelu * y_ref[0], gg[:, ATT_WIDTH:]).astype(BF16)
    o = o0_ref[0] + o1_ref[0]
    ss = _seg_sumsq(o, ones_ref[...])
    dn = ((o * lax.rsqrt(ss * (1.0 / DN_CHUNK) + EPS) * gdn_ref[...]) * _silu(dz_ref[0])).astype(BF16)
    mix = (jnp.dot(a_n, w_ref[0:ATT_WIDTH, :], preferred_element_type=F32)
           + jnp.dot(l_n, w_ref[ATT_WIDTH:ATT_WIDTH + LRU_WIDTH, :], preferred_element_type=F32)
           + jnp.dot(dn, w_ref[ATT_WIDTH + LRU_WIDTH:, :], preferred_element_type=F32))
    x = jnp.where(pl.program_id(1) + off == 0, c_ref[0], x_ref[0]) if split else x_ref[0]
    xo_ref[0] = x + gt_ref[0, 0] * _rms(mix, gn_ref[...])


def _out_proj(stream, att, y, gz, o0, o1, gg, gdn4, ones256, w_out, mod4, gn1, off):
    c_src, x_src, lat_off, t = stream
    b, _, d = x_src.shape
    nb = t // TM - off
    att_off = off - (t - att.shape[1]) // TM

    def tok(width, col=0):
        return pl.BlockSpec((1, TM, width), lambda bi, i: (bi, i + off, col))

    def const(shape):
        return pl.BlockSpec(shape, lambda bi, i: (0,) * len(shape))

    return pl.pallas_call(
        functools.partial(_out_kernel, off=off, split=lat_off > 0),
        grid=(b, nb),
        in_specs=[pl.BlockSpec((1, TM, d), lambda bi, i: (bi, jnp.maximum(i + off - lat_off, 0), 0)),
                  pl.BlockSpec((1, TM, d), lambda bi, i: (bi, 0, 0)),
                  pl.BlockSpec((1, TM, ATT_WIDTH), lambda bi, i: (bi, i + att_off, 0)),
                  tok(LRU_WIDTH), tok(LRU_WIDTH, 0), tok(DN_WIDTH), tok(DN_WIDTH), tok(DN_WIDTH, 1),
                  const((1, ATT_WIDTH + LRU_WIDTH)), const((1, DN_WIDTH)), const((DN_WIDTH, DN_WIDTH)),
                  const((d, d)),
                  pl.BlockSpec((1, 1, 1, d), lambda bi, i: (_mod_row(bi, i + off), 2, 0, 0)),
                  const((1, d))],
        out_specs=pl.BlockSpec((1, TM, d), lambda bi, i: (bi, i, 0)),
        out_shape=jax.ShapeDtypeStruct((b, nb * TM, d), F32),
        compiler_params=_cparams("arbitrary", "arbitrary"),
        name="out_proj",
    )(x_src, c_src, att, y, gz, o0, o1, gz, gg, gdn4, ones256, w_out, mod4, gn1)


def _ffn_kernel(x_ref, sh_ref, sc_ref, gt_ref, gn2_ref, gn3_ref, wi_ref, wo_ref, o_ref):
    x = x_ref[0]
    h = (_rms(x, gn2_ref[...]) * (1.0 + sc_ref[0, 0]) + sh_ref[0, 0]).astype(BF16)
    gu = jnp.dot(h, wi_ref[...], preferred_element_type=F32)
    act = (_silu(gu[:, :FFN_HIDDEN]) * gu[:, FFN_HIDDEN:]).astype(BF16)
    out = jnp.dot(act, wo_ref[...], preferred_element_type=F32)
    o_ref[0] = x + gt_ref[0, 0] * _rms(out, gn3_ref[...])


def _ffn(x1, mod4, gn2, gn3, w_ffn_in, w_ffn_out, off):
    b, rows, d = x1.shape
    nb = rows // TM

    def mod(k):
        return pl.BlockSpec((1, 1, 1, d), lambda bi, i: (_mod_row(bi, i + off), k, 0, 0))

    def const(shape):
        return pl.BlockSpec(shape, lambda bi, i: (0,) * len(shape))

    return pl.pallas_call(
        _ffn_kernel,
        grid=(b, nb),
        in_specs=[pl.BlockSpec((1, TM, d), lambda bi, i: (bi, i, 0)),
                  mod(3), mod(4), mod(5), const((1, d)), const((1, d)),
                  const((d, 2 * FFN_HIDDEN)), const((FFN_HIDDEN, d))],
        out_specs=pl.BlockSpec((1, TM, d), lambda bi, i: (bi, i, 0)),
        out_shape=jax.ShapeDtypeStruct((b, rows, d), F32),
        compiler_params=_cparams("arbitrary", "arbitrary"),
        name="ffn",
    )(x1, mod4, mod4, mod4, gn2, gn3, w_ffn_in, w_ffn_out)


def _block_diag_ones(n):
    i = jnp.arange(n) // HEAD_DIM
    return (i[:, None] == i[None, :]).astype(BF16)


def _chunk_triangles(n):
    i = jnp.arange(n)
    same = (i[:, None] // DN_CHUNK) == (i[None, :] // DN_CHUNK)
    lower = jnp.logical_and(same, i[None, :] <= i[:, None])
    upper = jnp.logical_and(same, i[None, :] >= i[:, None])
    return jnp.stack([lower, upper, same]).astype(BF16)


def _head_spread(first_row):
    r = jnp.arange(LANES)[:, None]
    c = jnp.arange(N_DIR * DN_WIDTH)[None, :]
    return (r == first_row + c // DN_CHUNK).astype(BF16)


def _rope_tables(ctx, n):
    t = jnp.arange(n)
    row = (t // GRID_W).astype(F32)
    col = (t % GRID_W).astype(F32)
    n_freq = HEAD_DIM // 4
    inv_freq = ROPE_THETA ** (-jnp.arange(n_freq, dtype=F32) / n_freq)
    ang = jnp.concatenate([row[:, None] * inv_freq, col[:, None] * inv_freq], axis=-1)
    cos = jnp.concatenate([jnp.ones((ctx, HEAD_DIM // 2), F32), jnp.cos(ang)], axis=0)
    sin = jnp.concatenate([jnp.zeros((ctx, HEAD_DIM // 2), F32), jnp.sin(ang)], axis=0)
    return jnp.tile(jnp.concatenate([cos, cos], axis=1), (1, 2)), jnp.tile(jnp.concatenate([-sin, sin], axis=1), (1, 2))


def _lru_gate_weights(w_r, w_i):
    def dense(w):
        z = jnp.zeros((LRU_BLOCK_DIM, LRU_BLOCK_DIM), F32)
        return [jnp.block([[w[2 * hf], z], [z, w[2 * hf + 1]]]) for hf in range(2)]
    halves = [[], []]
    for d in range(N_DIR):
        for w in (w_r[d], w_i[d]):
            for hf, m in enumerate(dense(w)):
                halves[hf].append(m)
    return jnp.stack([jnp.concatenate(h, axis=1) for h in halves]).astype(BF16)


def _lru_vectors(lam, b_r, b_i):
    rows = [lam[0], lam[1], b_r[0], b_i[0], b_r[1], b_i[1], jnp.zeros_like(lam[0]), jnp.zeros_like(lam[0])]
    v = jnp.stack(rows)
    return jnp.stack([v[:, :LANES], v[:, LANES:]])


def kernel(x, c, ctx, c_ctx, w_ada, b_ada, g_norm, w_in, g_qk, lru_conv_w, lru_conv_b, lru_w_r, lru_b_r, lru_w_i, lru_b_i, lru_lambda, dn_conv_w, dn_a_log, dn_dt_bias, g_dn_out, g_group, w_out, w_ffn_in, w_ffn_out):
    depth = w_ada.shape[0]
    bsz, n, d = x.shape
    m = ctx.shape[1]
    assert m == TM and d == D_MODEL and bsz == 2 and n % (ATT_STEPS_PER_BODY * ATT_TK) == 0

    cvec = jnp.concatenate([c, c_ctx[None, :], jnp.zeros((SUBLANES - bsz - 1, d), F32)], axis=0)
    mods = _modulation(cvec, w_ada, b_ada)
    cosf, sinf = _rope_tables(m, n)
    ones128 = _block_diag_ones(LANES)
    ones256 = _block_diag_ones(DN_WIDTH)
    tri = _chunk_triangles(TM)
    xg = _head_spread(0)
    xb = _head_spread(N_DIR * DN_HEADS)

    stream = (ctx, x, 1, m + n)
    for l in range(depth):
        last = l == depth - 1
        off = 1 if last else 0
        mod4 = mods[l].reshape(SUBLANES, 6, 1, d)
        wl = w_in[l]
        w_in_p = jnp.concatenate([wl[:, 0:768], wl[:, 1280:2048], wl[:, 1024:1280], wl[:, 768:1024],
                                  wl[:, 2048:2320], jnp.zeros((d, IN_COLS_PAD - 2320), F32)], axis=1).astype(BF16)
        cw = jnp.concatenate([dn_conv_w[l], lru_conv_w[l]], axis=1)
        cb = jnp.concatenate([jnp.zeros((3 * DN_WIDTH,), F32), lru_conv_b[l]])[None, :]
        pad = jnp.zeros((LANES - N_DIR * DN_HEADS,), F32)
        dnv = jnp.stack([jnp.concatenate([dn_a_log[l].reshape(-1), pad]),
                         jnp.concatenate([dn_dt_bias[l].reshape(-1), pad])])
        qt, k, vt, gz, xr, qk, pk, ge, rws = _in_proj(
            stream, mod4, g_norm[l, 0][None, :], w_in_p, jnp.tile(g_qk[l], (1, 2)), cosf, sinf, ones128,
            cw, cb, dnv, ones256, tri, xg, xb)
        key_norm_bound = 1.01 * HEAD_DIM ** 0.5 * jnp.max(jnp.abs(g_qk[l, 1]))
        score_bound = key_norm_bound * (LOG2E * jnp.max(jnp.abs(g_qk[l, 0])))
        att = _attention(qt, k, vt, key_norm_bound, score_bound, m, off)

        y = _lru(xr, _lru_gate_weights(lru_w_r[l], lru_w_i[l]),
                 _lru_vectors(lru_lambda[l], lru_b_r[l], lru_b_i[l]), m)
        o0, o1 = _deltanet(qk, pk, ge, rws, m)

        x1 = _out_proj(stream, att, y, gz, o0, o1, g_group[l][None, :], jnp.tile(g_dn_out[l], DN_HEADS)[None, :],
                       ones256, w_out[l].astype(BF16), mod4, g_norm[l, 1][None, :], off)
        xs = _ffn(x1, mod4, g_norm[l, 2][None, :], g_norm[l, 3][None, :],
                  w_ffn_in[l].astype(BF16), w_ffn_out[l].astype(BF16), off)
        stream = (xs, xs, 0, m + n)
    return xs
```

```python
import functools

import jax
import jax.numpy as jnp
from jax import lax
from jax.experimental import pallas as pl
from jax.experimental.pallas import tpu as pltpu

F32 = jnp.float32
BF16 = jnp.bfloat16

D_MODEL = 1024
GRID_W = 64
EPS = 1e-6
ATT_GROUP = 4
ATT_KV_HEADS = 2
HEAD_DIM = 64
ATT_WIDTH = 512
ROPE_THETA = 10000.0
LRU_WIDTH = 256
LRU_BLOCK_DIM = 64
LRU_C = 8.0
DN_HEADS = 4
DN_WIDTH = 256
DN_CHUNK = 64
DN_BASE = 16
DN_BLOCK = 4 * DN_CHUNK
DN_ROWS = 2 * (DN_BLOCK // DN_CHUNK)
DN_PACK = 5 * DN_WIDTH
N_DIR = 2
FFN_HIDDEN = 2816

LANES = 128
SUBLANES = 8
TM = 256
ATT_TK = 512
ATT_STEPS_PER_BODY = 8
ATT_VROWS = 80
LOG2E = 1.4426950408889634
ATT_FIXED_SHIFT_LIMIT = 48.0
QKV_COLS = 768
CONV_COLS = 1024
REST_COLS = 1664
IN_COLS_PAD = QKV_COLS + REST_COLS
VMEM_LIMIT = 56 * 1024 * 1024


def _cparams(*sem):
    return pltpu.CompilerParams(dimension_semantics=sem, vmem_limit_bytes=VMEM_LIMIT)


def _rms(x, gain):
    ms = jnp.mean(x * x, axis=-1, keepdims=True)
    return x * lax.rsqrt(ms + EPS) * gain


def _seg_sumsq(x, ones_bd):
    sq = x * x
    hi = sq.astype(BF16)
    lo = (sq - hi.astype(F32)).astype(BF16)
    return (jnp.dot(hi, ones_bd, preferred_element_type=F32)
            + jnp.dot(lo, ones_bd, preferred_element_type=F32))


def _silu(x):
    return x * jax.nn.sigmoid(x)


def _softplus(x):
    return jnp.maximum(x, 0.0) + jnp.log1p(jnp.exp(-jnp.abs(x)))


def _mod_row(b, i):
    return jnp.where(i == 0, 2, b)


def _mod_kernel(c_ref, w_ref, b_ref, o_ref):
    s = _silu(c_ref[...])
    o_ref[0] = jnp.dot(s.astype(BF16), w_ref[0].astype(BF16), preferred_element_type=F32) + b_ref[0]


def _modulation(cvec, w_ada, b_ada):
    depth = w_ada.shape[0]
    d = D_MODEL
    return pl.pallas_call(
        _mod_kernel,
        grid=(depth, 6),
        in_specs=[pl.BlockSpec((SUBLANES, d), lambda l, j: (0, 0)),
                  pl.BlockSpec((1, d, d), lambda l, j: (l, 0, j)),
                  pl.BlockSpec((1, 1, d), lambda l, j: (l, 0, j))],
        out_specs=pl.BlockSpec((1, SUBLANES, d), lambda l, j: (l, 0, j)),
        out_shape=jax.ShapeDtypeStruct((depth, SUBLANES, 6 * d), F32),
        compiler_params=_cparams("arbitrary", "arbitrary"),
        name="modulation",
    )(cvec, w_ada, b_ada.reshape(depth, 1, 6 * d))


def _split3(x):
    hi = x.astype(BF16)
    r = x - hi.astype(F32)
    mid = r.astype(BF16)
    lo = (r - mid.astype(F32)).astype(BF16)
    return hi, mid, lo


def _dot_parts(parts, mat, left=False):
    out = None
    for p in parts:
        t = (jnp.dot(mat, p, preferred_element_type=F32) if left else jnp.dot(p, mat, preferred_element_type=F32))
        out = t if out is None else out + t
    return out


def _in_kernel(x_ref, c_ref, xp_ref, xn_ref, sh_ref, sc_ref, gn_ref, w_ref, gqk_ref, cos_ref, sin_ref, ones_ref,
               cw_ref, cb_ref, dnv_ref, ones2_ref, tri_ref, xg_ref, xb_ref,
               qt_ref, k_ref, vt_ref, gz_ref, xr_ref, qk_ref, pk_ref, ge_ref, rows_ref, *, split):
    i = pl.program_id(1)
    nb = pl.num_programs(1)
    c_, w_ = DN_CHUNK, DN_WIDTH
    x = jnp.where(i == 0, c_ref[0], x_ref[0]) if split else x_ref[0]
    xe = jnp.concatenate([x, xp_ref[0], xn_ref[0]], axis=0)
    hb = (_rms(xe, gn_ref[...]) * (1.0 + sc_ref[0, 0]) + sh_ref[0, 0]).astype(BF16)
    hb_cur = hb[0:TM]
    cin = jnp.dot(hb, w_ref[:, QKV_COLS:QKV_COLS + CONV_COLS], preferred_element_type=F32)
    tail = jnp.dot(hb_cur, w_ref[:, QKV_COLS + CONV_COLS:], preferred_element_type=F32)
    qkv = jnp.dot(hb_cur, w_ref[:, 0:QKV_COLS], preferred_element_type=F32)
    cosf = cos_ref[...]
    sinf = sin_ref[...]
    lane = lax.broadcasted_iota(jnp.int32, (TM, LANES), 1)
    first_half = (lane & 32) == 0
    ones_bd = ones_ref[...]

    def norm_rope(slab, gain):
        ss = _seg_sumsq(slab, ones_bd)
        y = slab * lax.rsqrt(ss * (1.0 / HEAD_DIM) + EPS) * gain
        partner = jnp.where(first_half, pltpu.roll(y, 96, axis=1), pltpu.roll(y, 32, axis=1))
        return y * cosf + partner * sinf

    for s in range(ATT_WIDTH // LANES):
        qs = norm_rope(qkv[:, LANES * s:LANES * (s + 1)], gqk_ref[0:1, :]) * (LOG2E * HEAD_DIM ** -0.5)
        qst = qs.T.astype(BF16)
        for e in range(2):
            g, hh = divmod(2 * s + e, ATT_GROUP)
            qt_ref[0, g, 0, 0:HEAD_DIM, TM * hh:TM * (hh + 1)] = qst[HEAD_DIM * e:HEAD_DIM * (e + 1), :]
    for g in range(ATT_KV_HEADS):
        qt_ref[0, g, 0, HEAD_DIM:, :] = jnp.zeros((LANES - HEAD_DIM, ATT_GROUP * TM), BF16)
    kk = norm_rope(qkv[:, 512:640], gqk_ref[1:2, :])
    one_lane = jnp.where(lane == HEAD_DIM, 1.0, 0.0)
    vvt = qkv[:, 640:768].T.astype(BF16)
    row = lax.broadcasted_iota(jnp.int32, (ATT_VROWS - HEAD_DIM, TM), 0)
    ones_rows = jnp.where(row == 0, 1.0, 0.0).astype(BF16)
    for g in range(ATT_KV_HEADS):
        kg = kk if g == 0 else pltpu.roll(kk, HEAD_DIM, axis=1)
        k_ref[0, g] = jnp.where(lane < HEAD_DIM, kg, one_lane).astype(BF16)
        vt_ref[0, g, 0, 0:HEAD_DIM, :] = vvt[HEAD_DIM * g:HEAD_DIM * (g + 1), :]
        vt_ref[0, g, 0, HEAD_DIM:, :] = ones_rows

    gz_ref[0] = tail[:, 0:LRU_WIDTH + DN_WIDTH]
    ab = tail[:, LRU_WIDTH + DN_WIDTH:]

    cur = cin[0:TM]
    prv = jnp.where(i >= 2, cin[TM:TM + SUBLANES], 0.0)
    nxt = jnp.where(jnp.logical_and(i >= 1, i < nb - 1), cin[TM + SUBLANES:], 0.0)
    trow = lax.broadcasted_iota(jnp.int32, cur.shape, 0)
    x_m1 = jnp.where(trow == 0, prv[SUBLANES - 1:SUBLANES, :], pltpu.roll(cur, 1, axis=0))
    x_p1 = jnp.where(trow == TM - 1, nxt[0:1, :], pltpu.roll(cur, TM - 1, axis=0))
    x_p2 = jnp.where(trow == TM - 2, nxt[0:1, :],
                     jnp.where(trow == TM - 1, nxt[1:2, :], pltpu.roll(cur, TM - 2, axis=0)))
    conv = (cb_ref[...] + x_m1 * cw_ref[0:1, :] + cur * cw_ref[1:2, :]
            + x_p1 * cw_ref[2:3, :] + x_p2 * cw_ref[3:4, :])
    xr_ref[0] = conv[:, 3 * w_:]
    act = _silu(conv[:, 0:3 * w_])
    qq = act[:, 0:w_]
    kq = act[:, w_:2 * w_]
    vv = act[:, 2 * w_:3 * w_]
    ones2 = ones2_ref[...]
    qn = qq * lax.rsqrt(_seg_sumsq(qq, ones2) + EPS) * (c_ ** -0.5)
    kn = kq * lax.rsqrt(_seg_sumsq(kq, ones2) + EPS)
    qk_ref[0, :, 0:w_] = qn.astype(BF16)
    qk_ref[0, :, w_:2 * w_] = kn.astype(BF16)

    g = jnp.where(lane < N_DIR * DN_HEADS, -jnp.exp(dnv_ref[0:1, :]) * _softplus(ab + dnv_ref[1:2, :]), 0.0)
    beta = jax.nn.sigmoid(ab)
    g_parts = _split3(g)
    gc = jnp.where(lane < DN_HEADS, _dot_parts(g_parts, tri_ref[0], left=True),
                   _dot_parts(g_parts, tri_ref[1], left=True))
    tot = _dot_parts(g_parts, tri_ref[2], left=True)
    ge = _dot_parts(_split3(gc), xg_ref[...])
    te = _dot_parts(_split3(tot), xg_ref[...])
    be = _dot_parts(_split3(beta)[:2], xb_ref[...])
    ge_ref[0] = ge
    crow = lax.broadcasted_iota(jnp.int32, (c_, w_), 0)
    diag = crow == (lax.broadcasted_iota(jnp.int32, (c_, w_), 1) & (c_ - 1))
    for d in range(N_DIR):
        ge_d = ge[:, w_ * d:w_ * (d + 1)]
        te_d = te[:, w_ * d:w_ * (d + 1)]
        be_d = be[:, w_ * d:w_ * (d + 1)]
        eg = jnp.exp(ge_d)
        kb = kn * be_d
        pk_ref[0, d, :, 0:w_] = kb.astype(BF16)
        pk_ref[0, d, :, w_:2 * w_] = (vv * be_d).astype(BF16)
        pk_ref[0, d, :, 2 * w_:3 * w_] = (kb * eg).astype(BF16)
        pk_ref[0, d, :, 3 * w_:4 * w_] = (qn * eg).astype(BF16)
        pk_ref[0, d, :, 4 * w_:5 * w_] = (kn * jnp.exp(te_d - ge_d)).astype(BF16)
        for pr in range(TM // DN_BLOCK):
            for c in range(DN_BLOCK // c_):
                r0 = DN_BLOCK * pr + c_ * c
                rows_ref[0, d, pr, c:c + 1, :] = jnp.sum(jnp.where(diag, ge_d[r0:r0 + c_, :], 0.0), axis=0,
                                                         keepdims=True)
                rows_ref[0, d, pr, DN_ROWS // 2 + c:DN_ROWS // 2 + c + 1, :] = te_d[r0:r0 + 1, :]


def _in_proj(stream, mod4, gn0, w_in_p, gqk2, cosf, sinf, ones128, cw, cb, dnv, ones256, tri, xg, xb):
    c_src, x_src, lat_off, t = stream
    b, rows_src, d = x_src.shape
    nb = t // TM
    r8 = TM // SUBLANES
    ppb = TM // DN_BLOCK

    def const(shape):
        return pl.BlockSpec(shape, lambda bi, i: (0,) * len(shape))

    def halo(step):
        return pl.BlockSpec((1, SUBLANES, d), lambda bi, i: (
            bi, jnp.clip((i - lat_off) * r8 + step, 0, rows_src // SUBLANES - 1), 0))

    return pl.pallas_call(
        functools.partial(_in_kernel, split=lat_off > 0),
        grid=(b, nb),
        in_specs=[pl.BlockSpec((1, TM, d), lambda bi, i: (bi, jnp.maximum(i - lat_off, 0), 0)),
                  pl.BlockSpec((1, TM, d), lambda bi, i: (bi, 0, 0)),
                  halo(-1), halo(r8),
                  pl.BlockSpec((1, 1, 1, d), lambda bi, i: (_mod_row(bi, i), 0, 0, 0)),
                  pl.BlockSpec((1, 1, 1, d), lambda bi, i: (_mod_row(bi, i), 1, 0, 0)),
                  const((1, d)), const((d, IN_COLS_PAD)), const((2, LANES)),
                  pl.BlockSpec((TM, LANES), lambda bi, i: (i, 0)),
                  pl.BlockSpec((TM, LANES), lambda bi, i: (i, 0)),
                  const((LANES, LANES)),
                  const((4, CONV_COLS)), const((1, CONV_COLS)), const((2, LANES)), const((DN_WIDTH, DN_WIDTH)),
                  const((3, TM, TM)), const((LANES, N_DIR * DN_WIDTH)), const((LANES, N_DIR * DN_WIDTH))],
        out_specs=[pl.BlockSpec((1, ATT_KV_HEADS, 1, LANES, ATT_GROUP * TM), lambda bi, i: (bi, 0, i, 0, 0)),
                   pl.BlockSpec((1, ATT_KV_HEADS, TM, LANES), lambda bi, i: (bi, 0, i, 0)),
                   pl.BlockSpec((1, ATT_KV_HEADS, 1, ATT_VROWS, TM), lambda bi, i: (bi, 0, i, 0, 0)),
                   pl.BlockSpec((1, TM, LRU_WIDTH + DN_WIDTH), lambda bi, i: (bi, i, 0)),
                   pl.BlockSpec((1, TM, LRU_WIDTH), lambda bi, i: (bi, i, 0)),
                   pl.BlockSpec((1, TM, 2 * DN_WIDTH), lambda bi, i: (bi, i, 0)),
                   pl.BlockSpec((1, N_DIR, TM, DN_PACK), lambda bi, i: (bi, 0, i, 0)),
                   pl.BlockSpec((1, TM, N_DIR * DN_WIDTH), lambda bi, i: (bi, i, 0)),
                   pl.BlockSpec((1, N_DIR, ppb, DN_ROWS, DN_WIDTH), lambda bi, i: (bi, 0, i, 0, 0))],
        out_shape=[jax.ShapeDtypeStruct((b, ATT_KV_HEADS, nb, LANES, ATT_GROUP * TM), BF16),
                   jax.ShapeDtypeStruct((b, ATT_KV_HEADS, t, LANES), BF16),
                   jax.ShapeDtypeStruct((b, ATT_KV_HEADS, nb, ATT_VROWS, TM), BF16),
                   jax.ShapeDtypeStruct((b, t, LRU_WIDTH + DN_WIDTH), F32),
                   jax.ShapeDtypeStruct((b, t, LRU_WIDTH), F32),
                   jax.ShapeDtypeStruct((b, t, 2 * DN_WIDTH), BF16),
                   jax.ShapeDtypeStruct((b, N_DIR, t, DN_PACK), BF16),
                   jax.ShapeDtypeStruct((b, t, N_DIR * DN_WIDTH), F32),
                   jax.ShapeDtypeStruct((b, N_DIR, t // DN_BLOCK, DN_ROWS, DN_WIDTH), F32)],
        compiler_params=_cparams("arbitrary", "arbitrary"),
        name="in_proj",
    )(x_src, c_src, x_src, x_src, mod4, mod4, gn0, w_in_p, gqk2, cosf, sinf, ones128,
      cw, cb, dnv, ones256, tri, xg, xb)


def _att_kernel(qt_ref, k_ref, vt_ref, kn_ref, o_ref, m_ref, acc_ref, s_ref, cmax_ref, *, ctx, n_lat, q_off):
    del kn_ref
    i = pl.program_id(2) + q_off
    qt = qt_ref[0, 0, 0]
    nsub = ATT_TK // TM

    def scores(start, size):
        return jnp.dot(k_ref[0, 0, pl.ds(start, size), :], qt, preferred_element_type=F32)

    def pv(pt, blk0, nblk):
        out = None
        for j in range(nblk):
            t = jnp.dot(vt_ref[0, 0, blk0 + j], pt[j * TM:(j + 1) * TM, :], preferred_element_type=F32)
            out = t if out is None else out + t
        return out

    st = scores(0, ctx)
    m0 = jnp.max(st, axis=0, keepdims=True)
    m_ref[...] = m0
    acc_ref[...] = pv(jnp.exp2(st - m0).astype(BF16), 0, ctx // TM)

    n_chunks = n_lat // ATT_TK

    def step(j, cur, nxt):
        jn = jnp.minimum(j + 1, n_chunks - 1)
        sn = scores(pl.multiple_of(ctx + jn * ATT_TK, TM), ATT_TK)
        s_ref[nxt] = sn
        cmax_ref[nxt] = jnp.max(sn, axis=0, keepdims=True)
        st = s_ref[cur]
        m_prev = m_ref[...]
        m_new = jnp.maximum(m_prev, cmax_ref[cur])
        alpha = jnp.exp2(m_prev - m_new)
        pt = jnp.exp2(st - m_new).astype(BF16)
        acc_ref[...] = alpha * acc_ref[...] + pv(pt, ctx // TM + j * nsub, nsub)
        m_ref[...] = m_new

    def body(jj, carry):
        step(2 * jj, 0, 1)
        step(2 * jj + 1, 1, 0)
        return carry

    @pl.when(i > 0)
    def _():
        s0 = scores(ctx, ATT_TK)
        s_ref[0] = s0
        cmax_ref[0] = jnp.max(s0, axis=0, keepdims=True)
        lax.fori_loop(0, n_chunks // 2, body, 0)

    acc = acc_ref[...]
    out_t = acc[0:HEAD_DIM] / acc[HEAD_DIM:HEAD_DIM + 1]
    for h in range(ATT_GROUP):
        o_ref[0, :, HEAD_DIM * h:HEAD_DIM * (h + 1)] = out_t[:, TM * h:TM * (h + 1)].T


def _att_fixed_kernel(qt_ref, k_ref, vt_ref, kn_ref, o_ref, acc_ref, p_ref, qa_ref, *, ctx, n_lat, q_off):
    i = pl.program_id(2) + q_off
    nsub = ATT_TK // TM
    n_chunks = n_lat // ATT_TK
    qt = qt_ref[0, 0, 0]
    q32 = qt[0:HEAD_DIM, :].astype(F32)
    r = jnp.sqrt(jnp.sum(q32 * q32, axis=0, keepdims=True)) * kn_ref[...]
    tile_row = lax.broadcasted_iota(jnp.int32, (2 * SUBLANES, ATT_GROUP * TM), 0)
    qa_ref[...] = qt
    qa_ref[HEAD_DIM:HEAD_DIM + 2 * SUBLANES, :] = jnp.where(tile_row == 0, -r, 0.0).astype(BF16)

    def probs(start, size):
        s = jnp.dot(k_ref[0, 0, pl.ds(start, size), :], qa_ref[...], preferred_element_type=F32)
        return jnp.exp2(s).astype(BF16)

    def pv(pt, blk0, nblk):
        out = None
        for j in range(nblk):
            t = jnp.dot(vt_ref[0, 0, blk0 + j], pt[j * TM:(j + 1) * TM, :], preferred_element_type=F32)
            out = t if out is None else out + t
        return out

    def step(j, cur, nxt, prefetch=True):
        if prefetch:
            p_ref[nxt] = probs(pl.multiple_of(ctx + (j + 1) * ATT_TK, TM), ATT_TK)
        acc_ref[...] += pv(p_ref[cur], ctx // TM + j * nsub, nsub)

    def body(jj, carry, last=False):
        for u in range(ATT_STEPS_PER_BODY):
            step(ATT_STEPS_PER_BODY * jj + u, (u + 1) % 2, u % 2,
                 prefetch=not (last and u == ATT_STEPS_PER_BODY - 1))
        return carry

    @pl.when(i == 0)
    def _():
        acc_ref[...] = pv(probs(0, ctx), 0, ctx // TM)

    @pl.when(i > 0)
    def _():
        p_ctx = probs(0, ctx)
        p_ref[1] = probs(ctx, ATT_TK)
        acc_ref[...] = pv(p_ctx, 0, ctx // TM)
        n_bodies = n_chunks // ATT_STEPS_PER_BODY
        lax.fori_loop(0, n_bodies - 1, body, 0)
        body(n_bodies - 1, 0, last=True)

    acc = acc_ref[...]
    out_t = acc[0:HEAD_DIM] / acc[HEAD_DIM:HEAD_DIM + 1]
    for h in range(ATT_GROUP):
        o_ref[0, :, HEAD_DIM * h:HEAD_DIM * (h + 1)] = out_t[:, TM * h:TM * (h + 1)].T


def _attention(qt, k, vt, key_norm_bound, score_bound, ctx, q_off):
    b, _, t, _ = k.shape
    nb = t // TM
    nq = nb - q_off
    gw = ATT_GROUP * HEAD_DIM
    cols = ATT_GROUP * TM

    def call(body, scratch):
        return pl.pallas_call(
            functools.partial(body, ctx=ctx, n_lat=t - ctx, q_off=q_off),
            grid=(b, ATT_KV_HEADS, nq),
            in_specs=[pl.BlockSpec((1, 1, 1, LANES, cols), lambda bi, g, i: (bi, g, i + q_off, 0, 0)),
                      pl.BlockSpec((1, 1, t, LANES), lambda bi, g, i: (bi, g, 0, 0)),
                      pl.BlockSpec((1, 1, nb, ATT_VROWS, TM), lambda bi, g, i: (bi, g, 0, 0, 0)),
                      pl.BlockSpec((1, 1), lambda bi, g, i: (0, 0))],
            out_specs=pl.BlockSpec((1, TM, gw), lambda bi, g, i: (bi, i, g)),
            out_shape=jax.ShapeDtypeStruct((b, nq * TM, ATT_WIDTH), F32),
            scratch_shapes=scratch,
            compiler_params=_cparams("arbitrary", "arbitrary", "arbitrary"),
            name="attention",
        )

    online = call(_att_kernel, [pltpu.VMEM((1, cols), F32), pltpu.VMEM((ATT_VROWS, cols), F32),
                                pltpu.VMEM((2, ATT_TK, cols), F32), pltpu.VMEM((2, 1, cols), F32)])
    fixed = call(_att_fixed_kernel, [pltpu.VMEM((ATT_VROWS, cols), F32), pltpu.VMEM((2, ATT_TK, cols), BF16),
                                     pltpu.VMEM((LANES, cols), BF16)])
    return lax.cond(score_bound <= ATT_FIXED_SHIFT_LIMIT, fixed, online, qt, k, vt,
                    jnp.reshape(key_norm_bound, (1, 1)))


def _lru_kernel(xr_ref, wg_ref, vec_ref, y_ref, a0, u0, a1, u1, *, ctx, n):
    nseg = SUBLANES
    vec = vec_ref[0]
    a_refs = (a0, a1)
    u_refs = (u0, u1)

    def gates(x, d):
        g = jnp.dot(x.astype(BF16), wg_ref[0, :, 2 * LANES * d:2 * LANES * (d + 1)],
                    preferred_element_type=F32)
        r = 0.5 * jnp.tanh(0.5 * (g[:, :LANES] + vec[2 + 2 * d:3 + 2 * d])) + 0.5
        ig = 0.5 * jnp.tanh(0.5 * (g[:, LANES:] + vec[3 + 2 * d:4 + 2 * d])) + 0.5
        log_a = -LRU_C * r * _softplus(-vec[d:d + 1])
        a = jnp.exp(log_a)
        mult = jnp.sqrt(-jnp.tanh(log_a) * (a * a + 1.0))
        return a, mult * (ig * x)

    def run(row0, seg, base, init):
        pitch = seg + SUBLANES
        for j in range(nseg):
            x = xr_ref[0, row0 + j * seg:row0 + (j + 1) * seg, :]
            for d in range(N_DIR):
                a, u = gates(x, d)
                a_refs[d][base + j * pitch:base + j * pitch + seg, :] = a
                u_refs[d][base + j * pitch:base + j * pitch + seg, :] = u

        def step(t, carry):
            out = []
            for d in range(N_DIR):
                h, p = carry[2 * d], carry[2 * d + 1]
                r = base + (t if d == 0 else seg - 1 - t)
                idx = pl.ds(r, nseg, stride=pitch)
                a = a_refs[d][idx, :]
                h = a * h + u_refs[d][idx, :]
                p = a * p
                u_refs[d][idx, :] = h
                a_refs[d][idx, :] = p
                out += [h, p]
            return tuple(out)

        z = jnp.zeros((nseg, LANES), F32)
        o = jnp.ones((nseg, LANES), F32)
        fin = lax.fori_loop(0, seg, step, (z, o, z, o), unroll=8)
        carries, finals = [], []
        for d in range(N_DIR):
            hl, pl_ = fin[2 * d], fin[2 * d + 1]
            order = list(range(nseg)) if d == 0 else list(range(nseg - 1, -1, -1))
            c = init[d]
            cs = {}
            for j in order:
                cs[j] = c
                c = hl[j:j + 1] + pl_[j:j + 1] * c
            carries.append(cs)
            finals.append(c)
        for j in range(nseg):
            sl = slice(base + j * pitch, base + j * pitch + seg)
            y = (u0[sl, :] + a0[sl, :] * carries[0][j]) + (u1[sl, :] + a1[sl, :] * carries[1][j])
            y_ref[0, row0 + j * seg:row0 + (j + 1) * seg, :] = y
        return finals

    zero = jnp.zeros((1, LANES), F32)
    seg_c = ctx // nseg
    fin_c = run(0, seg_c, 0, [zero, zero])
    run(ctx, n // nseg, nseg * (seg_c + SUBLANES), fin_c)


def _lru(xr, wg, vec, ctx):
    b, t, _ = xr.shape
    n = t - ctx
    rows = SUBLANES * (ctx // SUBLANES + SUBLANES) + SUBLANES * (n // SUBLANES + SUBLANES)
    kern = functools.partial(_lru_kernel, ctx=ctx, n=n)
    return pl.pallas_call(
        kern,
        grid=(b, LRU_WIDTH // LANES),
        in_specs=[pl.BlockSpec((1, t, LANES), lambda bi, hf: (bi, 0, hf)),
                  pl.BlockSpec((1, LANES, 4 * LANES), lambda bi, hf: (hf, 0, 0)),
                  pl.BlockSpec((1, SUBLANES, LANES), lambda bi, hf: (hf, 0, 0))],
        out_specs=pl.BlockSpec((1, t, LANES), lambda bi, hf: (bi, 0, hf)),
        out_shape=jax.ShapeDtypeStruct((b, t, LRU_WIDTH), F32),
        scratch_shapes=[pltpu.VMEM((rows, LANES), F32)] * 4,
        compiler_params=_cparams("arbitrary", "arbitrary"),
        name="rglru",
    )(xr, wg, vec)


def _dn_kernel(qkf_ref, pkf_ref, gef_ref, rwf_ref, qkb_ref, pkb_ref, geb_ref, rwb_ref, o0_ref, o1_ref, s_ref):
    c_, w_ = DN_CHUNK, DN_WIDTH
    n_batch = o0_ref.shape[0]

    @pl.when(pl.program_id(0) == 0)
    def _():
        s_ref[...] = jnp.zeros_like(s_ref)

    row = lax.broadcasted_iota(jnp.int32, (c_, w_), 0)
    lane = lax.broadcasted_iota(jnp.int32, (c_, w_), 1)
    lj = lane & (c_ - 1)
    head_masks = [(lane >> 6) == h for h in range(DN_HEADS)]
    eye_f = jnp.where(row == lj, 1.0, 0.0)
    shift = DN_BASE.bit_length() - 1
    same_base = (row >> shift) == (lj >> shift)
    off_blk = ({}, {})
    size = DN_BASE
    while size < c_:
        shift = size.bit_length() - 1
        bi, bj = row >> shift, lj >> shift
        off_blk[0][size] = jnp.logical_and(bi == bj + 1, (bi & 1) == 1)
        off_blk[1][size] = jnp.logical_and(bj == bi + 1, (bj & 1) == 1)
        size *= 2
    bd_mask = ((lax.broadcasted_iota(jnp.int32, (w_, w_), 0) >> 6)
               == (lax.broadcasted_iota(jnp.int32, (w_, w_), 1) >> 6))
    nt = (((1,), (1,)), ((), ()))
    tn = (((0,), (0,)), ((), ()))

    def bd(x):
        xb = x.astype(BF16)
        zero = jnp.zeros_like(xb)
        return jnp.concatenate([jnp.where(hm, xb, zero) for hm in head_masks], axis=0)

    def bdot(a, b, dims=None):
        if dims is None:
            return jnp.dot(a.astype(BF16), b, preferred_element_type=F32)
        return lax.dot_general(a.astype(BF16), b, dims, preferred_element_type=F32)

    per_block = DN_BLOCK // c_
    in_refs = ((qkf_ref, pkf_ref, gef_ref, rwf_ref), (qkb_ref, pkb_ref, geb_ref, rwb_ref))
    o_refs = (o0_ref, o1_ref)
    probs = []
    for d in range(N_DIR):
        incl = (lj <= row) if d == 0 else (lj >= row)
        strict = (lj < row) if d == 0 else (lj > row)
        qk_ref, pk_ref, ge_ref, rw_ref = in_refs[d]
        for bi, step in [(bi, step) for bi in range(n_batch) for step in range(per_block)]:
            c = step if d == 0 else per_block - 1 - step
            rows = slice(c_ * c, c_ * (c + 1))
            decay = jnp.where(incl, jnp.exp(jnp.where(incl, ge_ref[bi, rows, :] - rw_ref[bi, 0, 0, c:c + 1, :], 0.0)),
                              0.0)
            probs.append(dict(d=d, bi=bi, step=step, rows=rows, incl=incl, strict=strict, decay=decay,
                              st=N_DIR * bi + d,
                              kbq=jnp.concatenate([pk_ref[bi, 0, rows, 0:w_], qk_ref[bi, rows, 0:w_]], axis=0),
                              k=qk_ref[bi, rows, w_:2 * w_], vb=pk_ref[bi, 0, rows, w_:2 * w_],
                              kbe=pk_ref[bi, 0, rows, 2 * w_:3 * w_], qd=pk_ref[bi, 0, rows, 3 * w_:4 * w_],
                              kt=pk_ref[bi, 0, rows, 4 * w_:5 * w_],
                              gt=jnp.exp(rw_ref[bi, 0, 0, per_block + c:per_block + c + 1, :])))
    for p in probs:
        a1 = bdot(p["kbq"], bd(p["k"]), nt)
        p["lower"] = jnp.where(p["strict"], a1[:c_] * p["decay"], 0.0)
        p["attn"] = jnp.where(p["incl"], a1[c_:] * p["decay"], 0.0)
        p["pw"] = jnp.where(same_base, p["lower"], 0.0)
        p["tm"] = eye_f - p["pw"]
    for p in probs:
        p["pw"] = bdot(p["pw"], bd(p["pw"]))
    for _ in range(DN_BASE.bit_length() - 3):
        for p in probs:
            t = bdot(jnp.concatenate([p["tm"], p["pw"]], axis=0), bd(p["pw"]))
            p["tm"] = p["tm"] + t[:c_]
            p["pw"] = t[c_:]
    for p in probs:
        p["tm"] = p["tm"] + bdot(p["tm"], bd(p["pw"]))
    size = DN_BASE
    while size < c_:
        for p in probs:
            p["tmp"] = bdot(jnp.where(off_blk[p["d"]][size], p["lower"], 0.0), bd(p["tm"]))
        for p in probs:
            p["tm"] = p["tm"] - bdot(p["tm"], bd(p["tmp"]))
        size *= 2
    for p in probs:
        sol = bdot(p["tm"], jnp.concatenate([bd(p["vb"]), bd(p["kbe"])], axis=1))
        p["u"] = sol[:, :w_]
        p["wqd"] = jnp.concatenate([sol[:, w_:].astype(BF16), p["qd"]], axis=0)
    for step in range(per_block):
        cur = [p for p in probs if p["step"] == step]
        for p in cur:
            p["s_old"] = s_ref[p["st"]]
            p["a2"] = bdot(p["wqd"], p["s_old"].astype(BF16))
        for p in cur:
            p["v_new"] = p["u"] - p["a2"][:c_]
        for p in cur:
            o_refs[p["d"]][p["bi"], p["rows"], :] = p["a2"][c_:] + bdot(p["attn"], bd(p["v_new"]))
        for p in cur:
            upd = bdot(p["kt"], p["v_new"].astype(BF16), tn)
            s_ref[p["st"]] = p["s_old"] * p["gt"] + jnp.where(bd_mask, upd, 0.0)


def _deltanet(qk, pk, ge, rws, ctx):
    b, t, _ = qk.shape
    np_ = t // DN_BLOCK
    npc = ctx // DN_BLOCK

    def bwd(p):
        return jnp.where(p < npc, npc - 1 - p, np_ - 1 - (p - npc))

    def specs(d, pos):
        return [pl.BlockSpec((b, DN_BLOCK, 2 * DN_WIDTH), lambda p: (0, pos(p), 0)),
                pl.BlockSpec((b, 1, DN_BLOCK, DN_PACK), lambda p: (0, d, pos(p), 0)),
                pl.BlockSpec((b, DN_BLOCK, DN_WIDTH), lambda p: (0, pos(p), d)),
                pl.BlockSpec((b, 1, 1, DN_ROWS, DN_WIDTH), lambda p: (0, d, pos(p), 0, 0))]

    o_shape = jax.ShapeDtypeStruct((b, t, DN_WIDTH), F32)
    return pl.pallas_call(
        _dn_kernel,
        grid=(np_,),
        in_specs=specs(0, lambda p: p) + specs(1, bwd),
        out_specs=[pl.BlockSpec((b, DN_BLOCK, DN_WIDTH), lambda p: (0, p, 0)),
                   pl.BlockSpec((b, DN_BLOCK, DN_WIDTH), lambda p: (0, bwd(p), 0))],
        out_shape=[o_shape, o_shape],
        scratch_shapes=[pltpu.VMEM((b * N_DIR, DN_WIDTH, DN_WIDTH), F32)],
        compiler_params=_cparams("arbitrary"),
        name="deltanet",
    )(qk, pk, ge, rws, qk, pk, ge, rws)


def _out_kernel(x_ref, c_ref, att_ref, y_ref, lg_ref, o0_ref, o1_ref, dz_ref, gg_ref, gdn_ref, ones_ref, w_ref,
                gt_ref, gn_ref, xo_ref, *, off, split):
    gg = gg_ref[...]
    a_n = _rms(att_ref[0], gg[:, 0:ATT_WIDTH]).astype(BF16)
    lg = lg_ref[0]
    gelu = lg * (0.5 * (1.0 + jnp.tanh(0.7978845608028654 * (lg + 0.044715 * (lg * lg * lg)))))
    l_n = _rms(gelu * y_ref[0], gg[:, ATT_WIDTH:]).astype(BF16)
    o = o0_ref[0] + o1_ref[0]
    ss = _seg_sumsq(o, ones_ref[...])
    dn = ((o * lax.rsqrt(ss * (1.0 / DN_CHUNK) + EPS) * gdn_ref[...]) * _silu(dz_ref[0])).astype(BF16)
    mix = jnp.dot(jnp.concatenate([a_n, l_n, dn], axis=1), w_ref[...], preferred_element_type=F32)
    x = jnp.where(pl.program_id(1) + off == 0, c_ref[0], x_ref[0]) if split else x_ref[0]
    xo_ref[0] = x + gt_ref[0, 0] * _rms(mix, gn_ref[...])


def _out_proj(stream, att, y, gz, o0, o1, gg, gdn4, ones256, w_out, mod4, gn1, off):
    c_src, x_src, lat_off, t = stream
    b, _, d = x_src.shape
    nb = t // TM - off
    att_off = off - (t - att.shape[1]) // TM

    def tok(width, col=0):
        return pl.BlockSpec((1, TM, width), lambda bi, i: (bi, i + off, col))

    def const(shape):
        return pl.BlockSpec(shape, lambda bi, i: (0,) * len(shape))

    return pl.pallas_call(
        functools.partial(_out_kernel, off=off, split=lat_off > 0),
        grid=(b, nb),
        in_specs=[pl.BlockSpec((1, TM, d), lambda bi, i: (bi, jnp.maximum(i + off - lat_off, 0), 0)),
                  pl.BlockSpec((1, TM, d), lambda bi, i: (bi, 0, 0)),
                  pl.BlockSpec((1, TM, ATT_WIDTH), lambda bi, i: (bi, i + att_off, 0)),
                  tok(LRU_WIDTH), tok(LRU_WIDTH, 0), tok(DN_WIDTH), tok(DN_WIDTH), tok(DN_WIDTH, 1),
                  const((1, ATT_WIDTH + LRU_WIDTH)), const((1, DN_WIDTH)), const((DN_WIDTH, DN_WIDTH)),
                  const((d, d)),
                  pl.BlockSpec((1, 1, 1, d), lambda bi, i: (_mod_row(bi, i + off), 2, 0, 0)),
                  const((1, d))],
        out_specs=pl.BlockSpec((1, TM, d), lambda bi, i: (bi, i, 0)),
        out_shape=jax.ShapeDtypeStruct((b, nb * TM, d), F32),
        compiler_params=_cparams("arbitrary", "arbitrary"),
        name="out_proj",
    )(x_src, c_src, att, y, gz, o0, o1, gz, gg, gdn4, ones256, w_out, mod4, gn1)


def _ffn_kernel(x_ref, sh_ref, sc_ref, gt_ref, gn2_ref, gn3_ref, wi_ref, wo_ref, o_ref):
    x = x_ref[0]
    h = (_rms(x, gn2_ref[...]) * (1.0 + sc_ref[0, 0]) + sh_ref[0, 0]).astype(BF16)
    gu = jnp.dot(h, wi_ref[...], preferred_element_type=F32)
    act = (_silu(gu[:, :FFN_HIDDEN]) * gu[:, FFN_HIDDEN:]).astype(BF16)
    out = jnp.dot(act, wo_ref[...], preferred_element_type=F32)
    o_ref[0] = x + gt_ref[0, 0] * _rms(out, gn3_ref[...])


def _ffn(x1, mod4, gn2, gn3, w_ffn_in, w_ffn_out, off):
    b, rows, d = x1.shape
    nb = rows // TM

    def mod(k):
        return pl.BlockSpec((1, 1, 1, d), lambda bi, i: (_mod_row(bi, i + off), k, 0, 0))

    def const(shape):
        return pl.BlockSpec(shape, lambda bi, i: (0,) * len(shape))

    return pl.pallas_call(
        _ffn_kernel,
        grid=(b, nb),
        in_specs=[pl.BlockSpec((1, TM, d), lambda bi, i: (bi, i, 0)),
                  mod(3), mod(4), mod(5), const((1, d)), const((1, d)),
                  const((d, 2 * FFN_HIDDEN)), const((FFN_HIDDEN, d))],
        out_specs=pl.BlockSpec((1, TM, d), lambda bi, i: (bi, i, 0)),
        out_shape=jax.ShapeDtypeStruct((b, rows, d), F32),
        compiler_params=_cparams("arbitrary", "arbitrary"),
        name="ffn",
    )(x1, mod4, mod4, mod4, gn2, gn3, w_ffn_in, w_ffn_out)


def _block_diag_ones(n):
    i = jnp.arange(n) // HEAD_DIM
    return (i[:, None] == i[None, :]).astype(BF16)


def _chunk_triangles(n):
    i = jnp.arange(n)
    same = (i[:, None] // DN_CHUNK) == (i[None, :] // DN_CHUNK)
    lower = jnp.logical_and(same, i[None, :] <= i[:, None])
    upper = jnp.logical_and(same, i[None, :] >= i[:, None])
    return jnp.stack([lower, upper, same]).astype(BF16)


def _head_spread(first_row):
    r = jnp.arange(LANES)[:, None]
    c = jnp.arange(N_DIR * DN_WIDTH)[None, :]
    return (r == first_row + c // DN_CHUNK).astype(BF16)


def _rope_tables(ctx, n):
    t = jnp.arange(n)
    row = (t // GRID_W).astype(F32)
    col = (t % GRID_W).astype(F32)
    n_freq = HEAD_DIM // 4
    inv_freq = ROPE_THETA ** (-jnp.arange(n_freq, dtype=F32) / n_freq)
    ang = jnp.concatenate([row[:, None] * inv_freq, col[:, None] * inv_freq], axis=-1)
    cos = jnp.concatenate([jnp.ones((ctx, HEAD_DIM // 2), F32), jnp.cos(ang)], axis=0)
    sin = jnp.concatenate([jnp.zeros((ctx, HEAD_DIM // 2), F32), jnp.sin(ang)], axis=0)
    return jnp.tile(jnp.concatenate([cos, cos], axis=1), (1, 2)), jnp.tile(jnp.concatenate([-sin, sin], axis=1), (1, 2))


def _lru_gate_weights(w_r, w_i):
    def dense(w):
        z = jnp.zeros((LRU_BLOCK_DIM, LRU_BLOCK_DIM), F32)
        return [jnp.block([[w[2 * hf], z], [z, w[2 * hf + 1]]]) for hf in range(2)]
    halves = [[], []]
    for d in range(N_DIR):
        for w in (w_r[d], w_i[d]):
            for hf, m in enumerate(dense(w)):
                halves[hf].append(m)
    return jnp.stack([jnp.concatenate(h, axis=1) for h in halves]).astype(BF16)


def _lru_vectors(lam, b_r, b_i):
    rows = [lam[0], lam[1], b_r[0], b_i[0], b_r[1], b_i[1], jnp.zeros_like(lam[0]), jnp.zeros_like(lam[0])]
    v = jnp.stack(rows)
    return jnp.stack([v[:, :LANES], v[:, LANES:]])


def kernel(x, c, ctx, c_ctx, w_ada, b_ada, g_norm, w_in, g_qk, lru_conv_w, lru_conv_b, lru_w_r, lru_b_r, lru_w_i, lru_b_i, lru_lambda, dn_conv_w, dn_a_log, dn_dt_bias, g_dn_out, g_group, w_out, w_ffn_in, w_ffn_out):
    depth = w_ada.shape[0]
    bsz, n, d = x.shape
    m = ctx.shape[1]
    assert m == TM and d == D_MODEL and bsz == 2 and n % (ATT_STEPS_PER_BODY * ATT_TK) == 0

    cvec = jnp.concatenate([c, c_ctx[None, :], jnp.zeros((SUBLANES - bsz - 1, d), F32)], axis=0)
    mods = _modulation(cvec, w_ada, b_ada)
    cosf, sinf = _rope_tables(m, n)
    ones128 = _block_diag_ones(LANES)
    ones256 = _block_diag_ones(DN_WIDTH)
    tri = _chunk_triangles(TM)
    xg = _head_spread(0)
    xb = _head_spread(N_DIR * DN_HEADS)

    stream = (ctx, x, 1, m + n)
    for l in range(depth):
        last = l == depth - 1
        off = 1 if last else 0
        mod4 = mods[l].reshape(SUBLANES, 6, 1, d)
        wl = w_in[l]
        w_in_p = jnp.concatenate([wl[:, 0:768], wl[:, 1280:2048], wl[:, 1024:1280], wl[:, 768:1024],
                                  wl[:, 2048:2320], jnp.zeros((d, IN_COLS_PAD - 2320), F32)], axis=1).astype(BF16)
        cw = jnp.concatenate([dn_conv_w[l], lru_conv_w[l]], axis=1)
        cb = jnp.concatenate([jnp.zeros((3 * DN_WIDTH,), F32), lru_conv_b[l]])[None, :]
        pad = jnp.zeros((LANES - N_DIR * DN_HEADS,), F32)
        dnv = jnp.stack([jnp.concatenate([dn_a_log[l].reshape(-1), pad]),
                         jnp.concatenate([dn_dt_bias[l].reshape(-1), pad])])
        qt, k, vt, gz, xr, qk, pk, ge, rws = _in_proj(
            stream, mod4, g_norm[l, 0][None, :], w_in_p, jnp.tile(g_qk[l], (1, 2)), cosf, sinf, ones128,
            cw, cb, dnv, ones256, tri, xg, xb)
        key_norm_bound = 1.01 * HEAD_DIM ** 0.5 * jnp.max(jnp.abs(g_qk[l, 1]))
        score_bound = key_norm_bound * (LOG2E * jnp.max(jnp.abs(g_qk[l, 0])))
        att = _attention(qt, k, vt, key_norm_bound, score_bound, m, off)

        y = _lru(xr, _lru_gate_weights(lru_w_r[l], lru_w_i[l]),
                 _lru_vectors(lru_lambda[l], lru_b_r[l], lru_b_i[l]), m)
        o0, o1 = _deltanet(qk, pk, ge, rws, m)

        x1 = _out_proj(stream, att, y, gz, o0, o1, g_group[l][None, :], jnp.tile(g_dn_out[l], DN_HEADS)[None, :],
                       ones256, w_out[l].astype(BF16), mod4, g_norm[l, 1][None, :], off)
        xs = _ffn(x1, mod4, g_norm[l, 2][None, :], g_norm[l, 3][None, :],
                  w_ffn_in[l].astype(BF16), w_ffn_out[l].astype(BF16), off)
        stream = (xs, xs, 0, m + n)
    return xs
```
